```python
import jax, jax.numpy as jnp
from jax import lax
import numpy as np

D_MODEL = 1024
BATCH = 16
SEQ = 2048
DEPTH = 1

D_MIX = D_MODEL
A_HEADS = 8
A_HEAD_DIM = 64
D_A = A_HEADS * A_HEAD_DIM
DILATED_BRANCHES = ((128, 1), (512, 4), (2048, 16))
BLK = 128
B_HEADS = 8
B_NOPE_DIM = 64
B_ROPE_DIM = 32
B_V_DIM = 64
D_B = B_HEADS * B_V_DIM
Q_LORA = 384
KV_LORA = 256
ROPE_THETA = 10000.0
P_IN = 3 * D_A + Q_LORA + KV_LORA + B_ROPE_DIM
N_BUCKETS = 32
MAX_DISTANCE = 2048
D_FF = ((-(-8 * D_MODEL // 3) + 255) // 256) * 256
N_MOD = 6
EPS = 1e-6
NEG = -1e30

kernel_name = 'hybrid_dilated_mla_adaln_layer'


def _rmsnorm(x, g):
    xf = x.astype(jnp.float32)
    y = xf * lax.rsqrt(jnp.mean(xf * xf, axis=-1, keepdims=True) + EPS)
    return (y * g.astype(jnp.float32)).astype(x.dtype)


def _t5_bucket(dist):
    max_exact = N_BUCKETS // 2
    d = np.maximum(dist, 1).astype(np.float64)
    large = max_exact + (np.log(d / max_exact) / np.log(MAX_DISTANCE / max_exact)
                         * (N_BUCKETS - max_exact)).astype(np.int64)
    large = np.minimum(large, N_BUCKETS - 1)
    return np.where(dist < max_exact, dist, large).astype(np.int32)


def _dilated_branch(q, k, v, rel_bias, window, dilation):
    B, S, H, E = q.shape
    span = window // dilation
    n = S // dilation
    nb = -(-n // BLK)
    n_pad = nb * BLK

    def to_residue(t):
        t = t.reshape(B, n, dilation, H, E).transpose(0, 2, 3, 1, 4)
        return jnp.pad(t, ((0, 0), (0, 0), (0, 0), (0, n_pad - n), (0, 0)))

    def band(t):
        t = jnp.pad(to_residue(t), ((0, 0), (0, 0), (0, 0), (BLK, 0), (0, 0)))
        t = t.reshape(B, dilation, H, nb + 1, BLK, E)
        return jnp.concatenate([t[:, :, :, :-1], t[:, :, :, 1:]], axis=4)

    qb = to_residue(q).reshape(B, dilation, H, nb, BLK, E)
    kb, vb = band(k), band(v)
    a = np.arange(BLK)[:, None]
    bk = np.arange(2 * BLK)[None, :]
    steps = BLK + a - bk
    valid = (steps >= 0) & (steps <= span)
    first = valid & (bk >= BLK)
    mask = np.concatenate([first[None], np.broadcast_to(valid, (nb - 1,) + valid.shape)], axis=0)
    bucket = _t5_bucket(np.clip(steps, 0, span) * dilation)
    bias = jnp.transpose(rel_bias[bucket], (2, 0, 1)).astype(jnp.float32)

    logits = jnp.einsum('bdhnqe,bdhnke->bdhnqk', qb, kb,
                        preferred_element_type=jnp.float32) * (E ** -0.5)
    logits = logits + bias[None, None, :, None]
    logits = jnp.where(jnp.asarray(mask)[None, None, None], logits, NEG)
    m = jnp.max(logits, axis=-1, keepdims=True)
    p = jnp.exp(logits - m)
    s = jnp.sum(p, axis=-1, keepdims=True)
    o = jnp.einsum('bdhnqk,bdhnke->bdhnqe', p, vb.astype(jnp.float32)) / s
    lse = (m + jnp.log(s))[..., 0]

    def from_residue(t):
        t = t.reshape((B, dilation, H, n_pad) + t.shape[5:])[:, :, :, :n]
        t = jnp.moveaxis(t, 3, 1)
        return t.reshape((B, S, H) + t.shape[4:])

    return from_residue(o), from_residue(lse)


def _rope(t):
    S, R = t.shape[1], t.shape[-1]
    half = R // 2
    inv = ROPE_THETA ** (-jnp.arange(half, dtype=jnp.float32) / half)
    ang = jnp.arange(S, dtype=jnp.float32)[:, None] * inv[None, :]
    cos, sin = jnp.cos(ang)[None, :, None], jnp.sin(ang)[None, :, None]
    t1, t2 = t[..., :half].astype(jnp.float32), t[..., half:].astype(jnp.float32)
    return jnp.concatenate([t1 * cos - t2 * sin, t1 * sin + t2 * cos], axis=-1).astype(t.dtype)


def _mla_attention(q_nope, q_rope, k_nope, k_rope, v):
    B, S, H, _ = q_nope.shape
    nq = S // BLK
    scale = (B_NOPE_DIM + B_ROPE_DIM) ** -0.5
    qn = jnp.moveaxis(q_nope.reshape(B, nq, BLK, H, -1), 1, 0)
    qr = jnp.moveaxis(q_rope.reshape(B, nq, BLK, H, -1), 1, 0)
    kpos = jnp.arange(S)

    def block(args):
        qn_b, qr_b, i = args
        logits = (jnp.einsum('bqhe,bkhe->bhqk', qn_b, k_nope, preferred_element_type=jnp.float32)
                  + jnp.einsum('bqhe,bke->bhqk', qr_b, k_rope, preferred_element_type=jnp.float32)) * scale
        qpos = i * BLK + jnp.arange(BLK)
        logits = jnp.where(qpos[:, None] >= kpos[None, :], logits, NEG)
        p = jax.nn.softmax(logits, axis=-1)
        return jnp.einsum('bhqk,bkhe->bqhe', p.astype(v.dtype), v)

    o = lax.map(block, (qn, qr, jnp.arange(nq)))
    return jnp.moveaxis(o, 0, 1).reshape(B, S, H, -1)


def _mixer(h, w_in, g_cq, w_uq, g_ckv, w_ukv, rel_bias, g_out_a, g_out_b, w_out):
    B, S, _ = h.shape
    proj = h @ w_in
    i1, i2, i3 = D_A, 2 * D_A, 3 * D_A
    i4, i5 = i3 + Q_LORA, i3 + Q_LORA + KV_LORA
    qa, ka, va, cq, ckv, kr = jnp.split(proj, [i1, i2, i3, i4, i5], axis=-1)
    qa = qa.reshape(B, S, A_HEADS, A_HEAD_DIM)
    ka = ka.reshape(B, S, A_HEADS, A_HEAD_DIM)
    va = va.reshape(B, S, A_HEADS, A_HEAD_DIM)
    branches = [_dilated_branch(qa, ka, va, rel_bias, w, d) for (w, d) in DILATED_BRANCHES]
    o_stack = jnp.stack([br[0] for br in branches])
    lse = jnp.stack([br[1] for br in branches])
    wts = jax.nn.softmax(lse, axis=0)
    out_a = jnp.einsum('rbsh,rbshe->bshe', wts, o_stack).reshape(B, S, D_A).astype(h.dtype)
    q = (_rmsnorm(cq, g_cq) @ w_uq).reshape(B, S, B_HEADS, B_NOPE_DIM + B_ROPE_DIM)
    q_nope, q_rope = q[..., :B_NOPE_DIM], _rope(q[..., B_NOPE_DIM:])
    kv = (_rmsnorm(ckv, g_ckv) @ w_ukv).reshape(B, S, B_HEADS, B_NOPE_DIM + B_V_DIM)
    k_nope, v = kv[..., :B_NOPE_DIM], kv[..., B_NOPE_DIM:]
    k_rope = _rope(kr[:, :, None, :])[:, :, 0]
    out_b = _mla_attention(q_nope, q_rope, k_nope, k_rope, v).reshape(B, S, D_B)
    y = jnp.concatenate([_rmsnorm(out_a, g_out_a), _rmsnorm(out_b, g_out_b)], axis=-1)
    return y @ w_out


def _swiglu(h, w_ffn_in, w_ffn_out):
    g, u = jnp.split(h @ w_ffn_in, 2, axis=-1)
    return (jax.nn.silu(g) * u) @ w_ffn_out


def setup_inputs(seed: int = 0) -> dict:
    key = jax.random.key(seed)
    ks = jax.random.split(key, 20)
    f32 = jnp.float32

    def nrm(k, shape, scale):
        return jax.random.normal(k, shape, f32) * scale

    def gain(k, shape):
        return 1.0 + 0.05 * jax.random.normal(k, shape, f32)

    L = DEPTH
    return {
        'x': nrm(ks[0], (BATCH, SEQ, D_MODEL), 1.0),
        'c': nrm(ks[1], (BATCH, D_MODEL), 1.0),
        'w_ada': nrm(ks[2], (L, D_MODEL, N_MOD * D_MODEL), 0.5 * D_MODEL ** -0.5),
        'b_ada': nrm(ks[3], (L, N_MOD * D_MODEL), 0.01),
        'g_norm1': gain(ks[4], (L, D_MODEL)),
        'w_in': nrm(ks[5], (L, D_MODEL, P_IN), D_MODEL ** -0.5),
        'g_cq': gain(ks[6], (L, Q_LORA)),
        'w_uq': nrm(ks[7], (L, Q_LORA, B_HEADS * (B_NOPE_DIM + B_ROPE_DIM)), Q_LORA ** -0.5),
        'g_ckv': gain(ks[8], (L, KV_LORA)),
        'w_ukv': nrm(ks[9], (L, KV_LORA, B_HEADS * (B_NOPE_DIM + B_V_DIM)), KV_LORA ** -0.5),
        'rel_bias': nrm(ks[10], (N_BUCKETS, A_HEADS), 0.5),
        'g_out_a': gain(ks[11], (L, D_A)),
        'g_out_b': gain(ks[12], (L, D_B)),
        'w_out': nrm(ks[13], (L, D_MIX, D_MODEL), D_MIX ** -0.5),
        'g_norm2': gain(ks[14], (L, D_MODEL)),
        'w_ffn_in': nrm(ks[15], (L, D_MODEL, 2 * D_FF), D_MODEL ** -0.5),
        'w_ffn_out': nrm(ks[16], (L, D_FF, D_MODEL), D_FF ** -0.5),
        'g_final': gain(ks[17], (D_MODEL,)),
    }


def reference(x, c, w_ada, b_ada, g_norm1, w_in, g_cq, w_uq, g_ckv, w_ukv, rel_bias,
              g_out_a, g_out_b, w_out, g_norm2, w_ffn_in, w_ffn_out, g_final):
    cond = jax.nn.silu(c)
    for l in range(DEPTH):
        mod = (cond @ w_ada[l] + b_ada[l])[:, None, :]
        sh1, sc1, g1, sh2, sc2, g2 = jnp.split(mod, N_MOD, axis=-1)
        h = _rmsnorm(x, g_norm1[l]) * (1.0 + sc1) + sh1
        x = x + g1 * _mixer(h, w_in[l], g_cq[l], w_uq[l], g_ckv[l], w_ukv[l], rel_bias,
                            g_out_a[l], g_out_b[l], w_out[l])
        h = _rmsnorm(x, g_norm2[l]) * (1.0 + sc2) + sh2
        x = x + g2 * _swiglu(h, w_ffn_in[l], w_ffn_out[l])
    return _rmsnorm(x, g_final)
```

```python
import functools

import jax
import jax.numpy as jnp
import numpy as np
from jax import lax
from jax.experimental import pallas as pl
from jax.experimental.pallas import tpu as pltpu

F32 = jnp.float32
BF16 = jnp.bfloat16

D_MODEL = 1024
SEQ = 2048
A_HEADS = 8
A_HEAD_DIM = 64
D_A = A_HEADS * A_HEAD_DIM
DILATIONS = (1, 4, 16)
BLK = 128
B_HEADS = 8
B_NOPE_DIM = 64
B_ROPE_DIM = 32
B_V_DIM = 64
D_B = B_HEADS * B_V_DIM
Q_LORA = 384
KV_LORA = 256
ROPE_THETA = 10000.0
N_BUCKETS = 32
MAX_DISTANCE = 2048
D_FF = 2816
N_MOD = 6
EPS = 1e-6
NEG = -1e30

LANES = 128
N_PAIRS = A_HEADS // 2
VMEM_LIMIT = 56 * 1024 * 1024

_C_QA, _C_KA, _C_VA = 0, D_A, 2 * D_A
_C_CQ = 3 * D_A
_C_CKV = _C_CQ + Q_LORA
_C_KR = _C_CKV + KV_LORA
_C_KROT = _C_KR + LANES
P_IN2 = _C_KROT + LANES

TM_PROJ = 512
TM_POST = 512
FF_CHUNK = 256
QB_MLA = 256


def _nt_dot(a, b):
    return lax.dot_general(a, b, (((1,), (1,)), ((), ())), preferred_element_type=F32)


def _rms(x, g):
    return x * lax.rsqrt(jnp.mean(x * x, axis=-1, keepdims=True) + EPS) * g


def _mod_kernel(c_ref, w_ref, b_ref, o_ref):
    cond = jax.nn.silu(c_ref[...])
    o_ref[...] = jnp.dot(cond.astype(BF16), w_ref[...].astype(BF16),
                         preferred_element_type=F32) + b_ref[...]


def _modulation(c, w_ada, b_ada):
    bsz = c.shape[0]
    n = w_ada.shape[1]
    bn = 1536
    return pl.pallas_call(
        _mod_kernel,
        grid=(n // bn,),
        in_specs=[pl.BlockSpec((bsz, D_MODEL), lambda j: (0, 0)),
                  pl.BlockSpec((D_MODEL, bn), lambda j: (0, j)),
                  pl.BlockSpec((1, bn), lambda j: (0, j))],
        out_specs=pl.BlockSpec((bsz, bn), lambda j: (0, j)),
        out_shape=jax.ShapeDtypeStruct((bsz, n), F32),
        compiler_params=pltpu.CompilerParams(vmem_limit_bytes=VMEM_LIMIT),
        name="adaln_mod",
    )(c, w_ada, b_ada.reshape(1, n))


def _proj_kernel(x_ref, sh_ref, sc_ref, g1_ref, cos_ref, sin_ref, win_ref, gcq_ref, wuq_ref,
                 gckv_ref, wukv_ref,
                 qa_ref, ka_ref, va_ref, qn_ref, qr_ref, kn_ref, kr_ref, vb_ref):
    h = _rms(x_ref[0], g1_ref[...]) * (1.0 + sc_ref[0]) + sh_ref[0]
    hb = h.astype(BF16)

    def proj(lo, hi):
        return jnp.dot(hb, win_ref[:, lo:hi], preferred_element_type=F32)

    qa_ref[0] = proj(_C_QA, _C_KA).astype(BF16)
    ka_ref[0] = proj(_C_KA, _C_VA).astype(BF16)
    va_ref[0] = proj(_C_VA, _C_CQ).astype(BF16)

    cos = cos_ref[...]
    sin = sin_ref[...]
    kr_ref[0] = (proj(_C_KR, _C_KROT) * cos + proj(_C_KROT, P_IN2) * sin).astype(BF16)

    cq = _rms(proj(_C_CQ, _C_CKV), gcq_ref[...]).astype(BF16)
    q = jnp.dot(cq, wuq_ref[...], preferred_element_type=F32)
    qn_ref[0] = q[:, :D_B].astype(BF16)
    cos2 = jnp.concatenate([cos, cos], axis=1)
    sin2 = jnp.concatenate([sin, sin], axis=1)
    n_r = B_HEADS * B_ROPE_DIM
    qr_ref[0] = (q[:, D_B:D_B + n_r] * cos2 + q[:, D_B + n_r:] * sin2).astype(BF16)

    ckv = _rms(proj(_C_CKV, _C_KR), gckv_ref[...]).astype(BF16)
    kv = jnp.dot(ckv, wukv_ref[...], preferred_element_type=F32)
    kn_ref[0] = kv[:, :D_B].astype(BF16)
    vb_ref[0] = kv[:, D_B:].astype(BF16)


def _const_spec(shape):
    nd = len(shape)
    return pl.BlockSpec(shape, lambda *_: (0,) * nd, pipeline_mode=pl.Buffered(1))


def _projections(x, sh1, sc1, g1, cos_t, sin_t, win, gcq, wuq, gckv, wukv):
    bsz = x.shape[0]
    tm = TM_PROJ
    tok = lambda w: pl.BlockSpec((1, tm, w), lambda b, j: (b, j, 0))
    per_b = pl.BlockSpec((1, 1, D_MODEL), lambda b, j: (b, 0, 0))
    tab = pl.BlockSpec((tm, LANES), lambda b, j: (j, 0))
    widths = (D_A, D_A, D_A, D_B, B_HEADS * B_ROPE_DIM, D_B, LANES, D_B)
    return pl.pallas_call(
        _proj_kernel,
        grid=(bsz, SEQ // tm),
        in_specs=[tok(D_MODEL), per_b, per_b, _const_spec((1, D_MODEL)), tab, tab,
                  _const_spec(win.shape), _const_spec((1, Q_LORA)), _const_spec(wuq.shape),
                  _const_spec((1, KV_LORA)), _const_spec(wukv.shape)],
        out_specs=[tok(w) for w in widths],
        out_shape=[jax.ShapeDtypeStruct((bsz, SEQ, w), BF16) for w in widths],
        compiler_params=pltpu.CompilerParams(
            dimension_semantics=("parallel", "parallel"), vmem_limit_bytes=VMEM_LIMIT),
        name="in_proj",
    )(x, sh1, sc1, g1, cos_t, sin_t, win, gcq, wuq, gckv, wukv)


def _stack_rows(ref, c, row):
    idx = (c,) if c is not None else ()
    return jnp.concatenate([ref[idx + (0, pl.ds(row, BLK), slice(None))],
                            ref[idx + (1, pl.ds(row, BLK), slice(None))]], axis=0)


def _softmax_step(qst, kb, vb, tab, state):
    s = _nt_dot(qst, kb) + tab
    nk = s.shape[1]
    m_cur = jnp.max(s, axis=1, keepdims=True)
    if state is None:
        m_new = jnp.broadcast_to(m_cur, (2 * BLK, LANES))
    else:
        m0, l0, acc0 = state
        m_new = jnp.maximum(m0, m_cur)
    m_wide = m_new if nk == LANES else jnp.concatenate([m_new] * (nk // LANES), axis=1)
    p = jnp.exp(s - m_wide)
    l_new = jnp.sum(p, axis=1, keepdims=True)
    acc = jnp.dot(p.astype(BF16), vb, preferred_element_type=F32)
    if state is None:
        l_new = jnp.broadcast_to(l_new, (2 * BLK, LANES))
    else:
        alpha = jnp.exp(m0 - m_new)
        l_new = alpha * l0 + l_new
        acc = alpha * acc0 + acc
    return m_new, l_new, acc


def _dil_kernel(q_ref, k_ref, v_ref, tab_ref, o_ref,
                fa, fb, qs1, qs4, qs16, k4, k16, v4, v16, st4, st1):
    lane = lax.broadcasted_iota(jnp.int32, (BLK, LANES), 1)
    head_a = lane < A_HEAD_DIM
    n4 = SEQ // 4

    def split_heads(dst, row, x):
        zero = jnp.zeros_like(x)
        dst[0, pl.ds(row, BLK), :] = jnp.where(head_a, x, zero)
        dst[1, pl.ds(row, BLK), :] = jnp.where(head_a, zero, x)

    def to_residue_orders(src_ref):
        fa[...] = src_ref[0].astype(F32)
        for r in range(4):
            fb[r * n4:(r + 1) * n4, :] = fa[pl.ds(r, n4, stride=4), :]

    def residue16_block(j):
        r4, q4 = j % 4, j // 4
        return fb[pl.ds(r4 * n4 + q4, BLK, stride=4), :]

    to_residue_orders(q_ref)
    for j in range(SEQ // BLK):
        split_heads(qs1, j * BLK, q_ref[0, j * BLK:(j + 1) * BLK, :])
        split_heads(qs4, j * BLK, fb[j * BLK:(j + 1) * BLK, :].astype(BF16))
        split_heads(qs16, j * BLK, residue16_block(j).astype(BF16))
    to_residue_orders(k_ref)
    k4[...] = fb[...].astype(BF16)
    for j in range(SEQ // BLK):
        k16[j * BLK:(j + 1) * BLK, :] = residue16_block(j).astype(BF16)
    to_residue_orders(v_ref)
    v4[...] = fb[...].astype(BF16)
    for j in range(SEQ // BLK):
        v16[j * BLK:(j + 1) * BLK, :] = residue16_block(j).astype(BF16)

    def store_state(dst, start, stride, state):
        for c, x in enumerate(state):
            rows = pl.ds(start, BLK, stride=stride) if stride > 1 else pl.ds(start, BLK)
            dst[c, 0, rows, :] = x[:BLK]
            dst[c, 1, rows, :] = x[BLK:]

    def load_state(src, row):
        return tuple(_stack_rows(src, c, row) for c in range(3))

    def branch16(j, carry):
        row = pl.multiple_of(j * BLK, BLK)
        state = _softmax_step(_stack_rows(qs16, None, row), k16[pl.ds(row, BLK), :],
                              v16[pl.ds(row, BLK), :], tab_ref[0, 2, :, BLK:], None)
        store_state(st4, (j % 4) * n4 + j // 4, 4, state)
        return carry

    lax.fori_loop(0, SEQ // BLK, branch16, 0)

    def branch4(r, carry):
        base = pl.multiple_of(r * n4, BLK)
        state = _softmax_step(_stack_rows(qs4, None, base), k4[pl.ds(base, BLK), :],
                              v4[pl.ds(base, BLK), :], tab_ref[0, 1, :, BLK:], load_state(st4, base))
        store_state(st1, r, 4, state)

        def inner(n, c2):
            row = pl.multiple_of(base + n * BLK, BLK)
            krow = pl.multiple_of(row - BLK, BLK)
            st = _softmax_step(_stack_rows(qs4, None, row), k4[pl.ds(krow, 2 * BLK), :],
                               v4[pl.ds(krow, 2 * BLK), :], tab_ref[0, 1], load_state(st4, row))
            store_state(st1, r + 4 * BLK * n, 4, st)
            return c2

        lax.fori_loop(1, n4 // BLK, inner, 0)
        return carry

    lax.fori_loop(0, 4, branch4, 0)

    def finish(row, state):
        _, l, acc = state
        o = acc / l
        o_ref[0, pl.ds(row, BLK), :] = jnp.where(head_a, o[:BLK], o[BLK:]).astype(BF16)

    finish(0, _softmax_step(_stack_rows(qs1, None, 0), k_ref[0, 0:BLK, :], v_ref[0, 0:BLK, :],
                            tab_ref[0, 0, :, BLK:], load_state(st1, 0)))

    def branch1(n, carry):
        row = pl.multiple_of(n * BLK, BLK)
        krow = pl.multiple_of(row - BLK, BLK)
        finish(row, _softmax_step(_stack_rows(qs1, None, row), k_ref[0, pl.ds(krow, 2 * BLK), :],
                                  v_ref[0, pl.ds(krow, 2 * BLK), :], tab_ref[0, 0],
                                  load_state(st1, row)))
        return carry

    lax.fori_loop(1, SEQ // BLK, branch1, 0)


def _dilated_attention(qa, ka, va, tables):
    bsz = qa.shape[0]
    pair = pl.BlockSpec((1, SEQ, LANES), lambda p, b: (b, 0, p))
    seq_f32 = pltpu.VMEM((SEQ, LANES), F32)
    seq_bf16 = pltpu.VMEM((SEQ, LANES), BF16)
    split_bf16 = pltpu.VMEM((2, SEQ, LANES), BF16)
    state = pltpu.VMEM((3, 2, SEQ, LANES), F32)
    return pl.pallas_call(
        _dil_kernel,
        grid=(N_PAIRS, bsz),
        in_specs=[pair, pair, pair,
                  pl.BlockSpec((1, 3, 2 * BLK, 2 * BLK), lambda p, b: (p, 0, 0, 0))],
        out_specs=pair,
        out_shape=jax.ShapeDtypeStruct((bsz, SEQ, D_A), BF16),
        scratch_shapes=[seq_f32, seq_f32, split_bf16, split_bf16, split_bf16,
                        seq_bf16, seq_bf16, seq_bf16, seq_bf16, state, state],
        compiler_params=pltpu.CompilerParams(
            dimension_semantics=("parallel", "parallel"), vmem_limit_bytes=VMEM_LIMIT),
        name="dilated_attn",
    )(qa, ka, va, tables)


def _mla_kernel(qn_ref, qr_ref, kn_ref, kr_ref, v_ref, o_ref, m_scr, l_scr, acc_scr):
    p = pl.program_id(1)
    i = pl.program_id(2)
    qb = QB_MLA
    scale = (B_NOPE_DIM + B_ROPE_DIM) ** -0.5
    lane = lax.broadcasted_iota(jnp.int32, (qb, LANES), 1)
    head_a = lane < B_NOPE_DIM
    slot = lane // B_ROPE_DIM
    slot_a = 2 * (p % 2)
    qn = qn_ref[0]
    qr = qr_ref[0]
    zero = jnp.zeros_like(qn)
    q_a = jnp.concatenate([jnp.where(head_a, qn, zero), jnp.where(slot == slot_a, qr, zero)], axis=1)
    q_b = jnp.concatenate([jnp.where(head_a, zero, qn), jnp.where(slot == slot_a + 1, qr, zero)], axis=1)
    qst = jnp.concatenate([q_a, q_b], axis=0)

    m_scr[...] = jnp.full(m_scr.shape, NEG, F32)
    l_scr[...] = jnp.zeros(l_scr.shape, F32)
    acc_scr[...] = jnp.zeros(acc_scr.shape, F32)

    def step(j, masked):
        row = pl.multiple_of(j * qb, qb)
        kc = jnp.concatenate([kn_ref[0, pl.ds(row, qb), :], kr_ref[0, pl.ds(row, qb), :]], axis=1)
        s = _nt_dot(qst, kc) * scale
        if masked:
            qpos = lax.broadcasted_iota(jnp.int32, (2 * qb, qb), 0) % qb
            kpos = lax.broadcasted_iota(jnp.int32, (2 * qb, qb), 1)
            s = jnp.where(qpos >= kpos, s, NEG)
        m0 = m_scr[...]
        m_new = jnp.maximum(m0, jnp.max(s, axis=1, keepdims=True))
        alpha = jnp.exp(m0 - m_new)
        pr = jnp.exp(s - jnp.concatenate([m_new] * (qb // LANES), axis=1))
        l_scr[...] = alpha * l_scr[...] + jnp.sum(pr, axis=1, keepdims=True)
        acc_scr[...] = alpha * acc_scr[...] + jnp.dot(pr.astype(BF16), v_ref[0, pl.ds(row, qb), :],
                                                      preferred_element_type=F32)
        m_scr[...] = m_new

    def full_step(j, carry):
        step(j, False)
        return carry

    lax.fori_loop(0, i, full_step, 0)
    step(i, True)
    o = acc_scr[...] / l_scr[...]
    o_ref[0] = jnp.where(head_a, o[:qb], o[qb:]).astype(BF16)


def _latent_attention(qn, qr, kn, kr, vb):
    bsz = qn.shape[0]
    qb = QB_MLA
    state = pltpu.VMEM((2 * qb, LANES), F32)
    return pl.pallas_call(
        _mla_kernel,
        grid=(bsz, N_PAIRS, SEQ // qb),
        in_specs=[pl.BlockSpec((1, qb, LANES), lambda b, p, i: (b, i, p)),
                  pl.BlockSpec((1, qb, LANES), lambda b, p, i: (b, i, p // 2)),
                  pl.BlockSpec((1, SEQ, LANES), lambda b, p, i: (b, 0, p)),
                  pl.BlockSpec((1, SEQ, LANES), lambda b, p, i: (b, 0, 0)),
                  pl.BlockSpec((1, SEQ, LANES), lambda b, p, i: (b, 0, p))],
        out_specs=pl.BlockSpec((1, qb, LANES), lambda b, p, i: (b, i, p)),
        out_shape=jax.ShapeDtypeStruct((bsz, SEQ, D_B), BF16),
        scratch_shapes=[state, state, state],
        compiler_params=pltpu.CompilerParams(
            dimension_semantics=("parallel", "parallel", "parallel"), vmem_limit_bytes=VMEM_LIMIT),
        name="latent_attn",
    )(qn, qr, kn, kr, vb)


def _post_kernel(x_ref, oa_ref, ob_ref, g1_ref, sh2_ref, sc2_ref, g2_ref, goa_ref, gob_ref,
                 wout_ref, gn2_ref, wfi_ref, wfo_ref, gfin_ref, o_ref, a_scr):
    ya = _rms(oa_ref[0].astype(F32), goa_ref[...])
    yb = _rms(ob_ref[0].astype(F32), gob_ref[...])
    y = jnp.concatenate([ya, yb], axis=1).astype(BF16)
    x1 = x_ref[0] + g1_ref[0] * jnp.dot(y, wout_ref[...], preferred_element_type=F32)
    h2 = (_rms(x1, gn2_ref[...]) * (1.0 + sc2_ref[0]) + sh2_ref[0]).astype(BF16)
    for c in range(D_FF // FF_CHUNK):
        lo = c * FF_CHUNK
        g = jnp.dot(h2, wfi_ref[:, lo:lo + FF_CHUNK], preferred_element_type=F32)
        u = jnp.dot(h2, wfi_ref[:, D_FF + lo:D_FF + lo + FF_CHUNK], preferred_element_type=F32)
        a_scr[:, lo:lo + FF_CHUNK] = (jax.nn.silu(g) * u).astype(BF16)
    x2 = x1 + g2_ref[0] * jnp.dot(a_scr[...], wfo_ref[...], preferred_element_type=F32)
    o_ref[0] = _rms(x2, gfin_ref[...])


def _post(x, oa, ob, g1m, sh2, sc2, g2m, goa, gob, wout, gn2, wfi, wfo, gfin):
    bsz = x.shape[0]
    tm = TM_POST
    tok = lambda w: pl.BlockSpec((1, tm, w), lambda b, j: (b, j, 0))
    per_b = pl.BlockSpec((1, 1, D_MODEL), lambda b, j: (b, 0, 0))
    return pl.pallas_call(
        _post_kernel,
        grid=(bsz, SEQ // tm),
        in_specs=[tok(D_MODEL), tok(D_A), tok(D_B), per_b, per_b, per_b, per_b,
                  _const_spec((1, D_A)), _const_spec((1, D_B)), _const_spec(wout.shape),
                  _const_spec((1, D_MODEL)), _const_spec(wfi.shape), _const_spec(wfo.shape),
                  _const_spec((1, D_MODEL))],
        out_specs=tok(D_MODEL),
        out_shape=jax.ShapeDtypeStruct((bsz, SEQ, D_MODEL), F32),
        scratch_shapes=[pltpu.VMEM((tm, D_FF), BF16)],
        compiler_params=pltpu.CompilerParams(
            dimension_semantics=("parallel", "parallel"), vmem_limit_bytes=VMEM_LIMIT),
        name="out_proj_ffn",
    )(x, oa, ob, g1m, sh2, sc2, g2m, goa, gob, wout, gn2, wfi, wfo, gfin)


def _rotate_half_cols(w):
    lead = w.shape[:-1]
    half = B_ROPE_DIM // 2
    w = w.reshape(lead + (-1, 2, half))
    return jnp.stack([-w[..., 1, :], w[..., 0, :]], axis=-2).reshape(lead + (-1,))


def _t5_bucket(dist):
    max_exact = N_BUCKETS // 2
    d = np.maximum(dist, 1).astype(np.float64)
    large = max_exact + (np.log(d / max_exact) / np.log(MAX_DISTANCE / max_exact)
                         * (N_BUCKETS - max_exact)).astype(np.int64)
    large = np.minimum(large, N_BUCKETS - 1)
    return np.where(dist < max_exact, dist, large).astype(np.int32)


def _bias_tables(rel_bias):
    a = np.arange(BLK)[:, None]
    bk = np.arange(2 * BLK)[None, :]
    steps = BLK + a - bk
    valid = jnp.asarray((steps >= 0) & (steps <= BLK))
    tabs = []
    for dil in DILATIONS:
        bucket = _t5_bucket(np.clip(steps, 0, BLK) * dil)
        bias = jnp.transpose(rel_bias[bucket], (2, 0, 1)).astype(F32)
        tabs.append(jnp.where(valid[None], bias, NEG))
    t = jnp.stack(tabs, axis=1)
    t = t.reshape(N_PAIRS, 2, 3, BLK, 2 * BLK).transpose(0, 2, 1, 3, 4)
    return t.reshape(N_PAIRS, 3, 2 * BLK, 2 * BLK)


def _rope_tables():
    half = B_ROPE_DIM // 2
    inv = ROPE_THETA ** (-jnp.arange(half, dtype=F32) / half)
    ang = jnp.arange(SEQ, dtype=F32)[:, None] * inv[None, :]
    reps = LANES // half
    return jnp.tile(jnp.cos(ang), (1, reps)), jnp.tile(jnp.sin(ang), (1, reps))


def kernel(x, c, w_ada, b_ada, g_norm1, w_in, g_cq, w_uq, g_ckv, w_ukv, rel_bias, g_out_a, g_out_b,
           w_out, g_norm2, w_ffn_in, w_ffn_out, g_final):
    bsz = x.shape[0]
    assert x.shape == (bsz, SEQ, D_MODEL) and w_ada.shape[0] == 1
    l = 0

    mod = _modulation(c, w_ada[l], b_ada[l]).reshape(bsz, N_MOD, 1, D_MODEL)
    sh1, sc1, g1m, sh2, sc2, g2m = (mod[:, i] for i in range(N_MOD))

    wi = w_in[l]
    kr_cols = wi[:, 3 * D_A + Q_LORA + KV_LORA:]
    reps = LANES // B_ROPE_DIM
    win = jnp.concatenate([wi[:, :D_A] * (A_HEAD_DIM ** -0.5), wi[:, D_A:3 * D_A + Q_LORA + KV_LORA],
                           jnp.tile(kr_cols, (1, reps)), jnp.tile(_rotate_half_cols(kr_cols), (1, reps))],
                          axis=1).astype(BF16)
    wq = w_uq[l].reshape(Q_LORA, B_HEADS, B_NOPE_DIM + B_ROPE_DIM)
    wq_rope = wq[:, :, B_NOPE_DIM:].reshape(Q_LORA, -1)
    wuq = jnp.concatenate([wq[:, :, :B_NOPE_DIM].reshape(Q_LORA, -1), wq_rope,
                           _rotate_half_cols(wq_rope)], axis=1).astype(BF16)
    wkv = w_ukv[l].reshape(KV_LORA, B_HEADS, B_NOPE_DIM + B_V_DIM)
    wukv = jnp.concatenate([wkv[:, :, :B_NOPE_DIM].reshape(KV_LORA, -1),
                            wkv[:, :, B_NOPE_DIM:].reshape(KV_LORA, -1)], axis=1).astype(BF16)
    cos_t, sin_t = _rope_tables()

    qa, ka, va, qn, qr, kn, kr, vb = _projections(
        x, sh1, sc1, g_norm1[l].reshape(1, -1), cos_t, sin_t, win, g_cq[l].reshape(1, -1), wuq,
        g_ckv[l].reshape(1, -1), wukv)

    out_a = _dilated_attention(qa, ka, va, _bias_tables(rel_bias))
    out_b = _latent_attention(qn, qr, kn, kr, vb)

    return _post(x, out_a, out_b, g1m, sh2, sc2, g2m, g_out_a[l].reshape(1, -1),
                 g_out_b[l].reshape(1, -1), w_out[l].astype(BF16), g_norm2[l].reshape(1, -1),
                 w_ffn_in[l].astype(BF16), w_ffn_out[l].astype(BF16), g_final.reshape(1, -1))
```

```python
import functools

import jax
import jax.numpy as jnp
import numpy as np
from jax import lax
from jax.experimental import pallas as pl
from jax.experimental.pallas import tpu as pltpu

F32 = jnp.float32
BF16 = jnp.bfloat16

D_MODEL = 1024
SEQ = 2048
A_HEADS = 8
A_HEAD_DIM = 64
D_A = A_HEADS * A_HEAD_DIM
DILATIONS = (1, 4, 16)
BLK = 128
B_HEADS = 8
B_NOPE_DIM = 64
B_ROPE_DIM = 32
B_V_DIM = 64
D_B = B_HEADS * B_V_DIM
Q_LORA = 384
KV_LORA = 256
ROPE_THETA = 10000.0
N_BUCKETS = 32
MAX_DISTANCE = 2048
D_FF = 2816
N_MOD = 6
EPS = 1e-6
NEG = -1e30

LANES = 128
N_PAIRS = A_HEADS // 2
VMEM_LIMIT = 56 * 1024 * 1024

_C_QA, _C_KA, _C_VA = 0, D_A, 2 * D_A
_C_CQ = 3 * D_A
_C_CKV = _C_CQ + Q_LORA
_C_KR = _C_CKV + KV_LORA
_C_KROT = _C_KR + LANES
P_IN2 = _C_KROT + LANES

TM_PROJ = 512
TM_POST = 512
FF_CHUNK = 256
QB_MLA = 256


def _nt_dot(a, b):
    return lax.dot_general(a, b, (((1,), (1,)), ((), ())), preferred_element_type=F32)


def _aligned(i, m):
    return i if isinstance(i, int) else pl.multiple_of(i, m)


def _rms(x, g):
    return x * lax.rsqrt(jnp.mean(x * x, axis=-1, keepdims=True) + EPS) * g


def _mod_kernel(c_ref, w_ref, b_ref, o_ref):
    cond = jax.nn.silu(c_ref[...])
    o_ref[...] = jnp.dot(cond.astype(BF16), w_ref[...].astype(BF16),
                         preferred_element_type=F32) + b_ref[...]


def _modulation(c, w_ada, b_ada):
    bsz = c.shape[0]
    n = w_ada.shape[1]
    bn = 1536
    return pl.pallas_call(
        _mod_kernel,
        grid=(n // bn,),
        in_specs=[pl.BlockSpec((bsz, D_MODEL), lambda j: (0, 0)),
                  pl.BlockSpec((D_MODEL, bn), lambda j: (0, j)),
                  pl.BlockSpec((1, bn), lambda j: (0, j))],
        out_specs=pl.BlockSpec((bsz, bn), lambda j: (0, j)),
        out_shape=jax.ShapeDtypeStruct((bsz, n), F32),
        compiler_params=pltpu.CompilerParams(vmem_limit_bytes=VMEM_LIMIT),
        name="adaln_mod",
    )(c, w_ada, b_ada.reshape(1, n))


def _proj_kernel(x_ref, sh_ref, sc_ref, g1_ref, cos_ref, sin_ref, win_ref, gcq_ref, wuq_ref,
                 gckv_ref, wukv_ref,
                 qa_ref, ka_ref, va_ref, qn_ref, qr_ref, kn_ref, kr_ref, vb_ref):
    h = _rms(x_ref[0], g1_ref[...]) * (1.0 + sc_ref[0]) + sh_ref[0]
    hb = h.astype(BF16)

    def proj(lo, hi):
        return jnp.dot(hb, win_ref[:, lo:hi], preferred_element_type=F32)

    qa_ref[0] = proj(_C_QA, _C_KA).astype(BF16)
    ka_ref[0] = proj(_C_KA, _C_VA).astype(BF16)
    va_ref[0] = proj(_C_VA, _C_CQ).astype(BF16)

    cos = cos_ref[...]
    sin = sin_ref[...]
    kr_ref[0] = (proj(_C_KR, _C_KROT) * cos + proj(_C_KROT, P_IN2) * sin).astype(BF16)

    cq = _rms(proj(_C_CQ, _C_CKV), gcq_ref[...]).astype(BF16)
    q = jnp.dot(cq, wuq_ref[...], preferred_element_type=F32)
    qn_ref[0] = q[:, :D_B].astype(BF16)
    cos2 = jnp.concatenate([cos, cos], axis=1)
    sin2 = jnp.concatenate([sin, sin], axis=1)
    n_r = B_HEADS * B_ROPE_DIM
    qr_ref[0] = (q[:, D_B:D_B + n_r] * cos2 + q[:, D_B + n_r:] * sin2).astype(BF16)

    ckv = _rms(proj(_C_CKV, _C_KR), gckv_ref[...]).astype(BF16)
    kv = jnp.dot(ckv, wukv_ref[...], preferred_element_type=F32)
    kn_ref[0] = kv[:, :D_B].astype(BF16)
    vb_ref[0] = kv[:, D_B:].astype(BF16)


def _const_spec(shape):
    nd = len(shape)
    return pl.BlockSpec(shape, lambda *_: (0,) * nd, pipeline_mode=pl.Buffered(1))


def _projections(x, sh1, sc1, g1, cos_t, sin_t, win, gcq, wuq, gckv, wukv):
    bsz = x.shape[0]
    tm = TM_PROJ
    tok = lambda w: pl.BlockSpec((1, tm, w), lambda b, j: (b, j, 0))
    per_b = pl.BlockSpec((1, 1, D_MODEL), lambda b, j: (b, 0, 0))
    tab = pl.BlockSpec((tm, LANES), lambda b, j: (j, 0))
    widths = (D_A, D_A, D_A, D_B, B_HEADS * B_ROPE_DIM, D_B, LANES, D_B)
    return pl.pallas_call(
        _proj_kernel,
        grid=(bsz, SEQ // tm),
        in_specs=[tok(D_MODEL), per_b, per_b, _const_spec((1, D_MODEL)), tab, tab,
                  _const_spec(win.shape), _const_spec((1, Q_LORA)), _const_spec(wuq.shape),
                  _const_spec((1, KV_LORA)), _const_spec(wukv.shape)],
        out_specs=[tok(w) for w in widths],
        out_shape=[jax.ShapeDtypeStruct((bsz, SEQ, w), BF16) for w in widths],
        compiler_params=pltpu.CompilerParams(
            dimension_semantics=("parallel", "parallel"), vmem_limit_bytes=VMEM_LIMIT),
        name="in_proj",
    )(x, sh1, sc1, g1, cos_t, sin_t, win, gcq, wuq, gckv, wukv)


def _stack_rows(ref, c, row):
    idx = (c,) if c is not None else ()
    return jnp.concatenate([ref[idx + (0, pl.ds(row, BLK), slice(None))],
                            ref[idx + (1, pl.ds(row, BLK), slice(None))]], axis=0)


def _softmax_step(qst, kb, vb, tab, state):
    s = _nt_dot(qst, kb) + tab
    nk = s.shape[1]
    m_cur = jnp.max(s, axis=1, keepdims=True)
    if state is None:
        m_new = jnp.broadcast_to(m_cur, (2 * BLK, LANES))
    else:
        m0, l0, acc0 = state
        m_new = jnp.maximum(m0, m_cur)
    m_wide = m_new if nk == LANES else jnp.concatenate([m_new] * (nk // LANES), axis=1)
    p = jnp.exp(s - m_wide)
    l_new = jnp.sum(p, axis=1, keepdims=True)
    acc = jnp.dot(p.astype(BF16), vb, preferred_element_type=F32)
    if state is None:
        l_new = jnp.broadcast_to(l_new, (2 * BLK, LANES))
    else:
        alpha = jnp.exp(m0 - m_new)
        l_new = alpha * l0 + l_new
        acc = alpha * acc0 + acc
    return m_new, l_new, acc


def _dil_kernel(q_ref, k_ref, v_ref, tab_ref, o_ref,
                fa, fb, qs1, qs4, qs16, k4, k16, v4, v16, st4, st1):
    lane = lax.broadcasted_iota(jnp.int32, (BLK, LANES), 1)
    head_a = lane < A_HEAD_DIM
    n4 = SEQ // 4

    def split_heads(dst, row, x):
        zero = jnp.zeros_like(x)
        dst[0, pl.ds(row, BLK), :] = jnp.where(head_a, x, zero)
        dst[1, pl.ds(row, BLK), :] = jnp.where(head_a, zero, x)

    def to_residue_orders(src_ref):
        fa[...] = src_ref[0].astype(F32)
        for r in range(4):
            fb[r * n4:(r + 1) * n4, :] = fa[pl.ds(r, n4, stride=4), :]

    def residue16_block(j):
        r4, q4 = j % 4, j // 4
        return fb[pl.ds(r4 * n4 + q4, BLK, stride=4), :]

    to_residue_orders(q_ref)
    for j in range(SEQ // BLK):
        split_heads(qs1, j * BLK, q_ref[0, j * BLK:(j + 1) * BLK, :])
        split_heads(qs4, j * BLK, fb[j * BLK:(j + 1) * BLK, :].astype(BF16))
        split_heads(qs16, j * BLK, residue16_block(j).astype(BF16))
    to_residue_orders(k_ref)
    k4[...] = fb[...].astype(BF16)
    for j in range(SEQ // BLK):
        k16[j * BLK:(j + 1) * BLK, :] = residue16_block(j).astype(BF16)
    to_residue_orders(v_ref)
    v4[...] = fb[...].astype(BF16)
    for j in range(SEQ // BLK):
        v16[j * BLK:(j + 1) * BLK, :] = residue16_block(j).astype(BF16)

    def store_state(dst, start, stride, state):
        for c, x in enumerate(state):
            rows = pl.ds(start, BLK, stride=stride) if stride > 1 else pl.ds(start, BLK)
            dst[c, 0, rows, :] = x[:BLK]
            dst[c, 1, rows, :] = x[BLK:]

    def load_state(src, row):
        return tuple(_stack_rows(src, c, row) for c in range(3))


    def block16(j, start):
        row = _aligned(j * BLK, BLK)
        state = _softmax_step(_stack_rows(qs16, None, row), k16[pl.ds(row, BLK), :],
                              v16[pl.ds(row, BLK), :], tab_ref[0, 2, :, BLK:], None)
        store_state(st4, start, 4, state)

    def branch16(it, carry):
        for r4 in range(4):
            block16(4 * it + r4, r4 * n4 + it)
        return carry

    lax.fori_loop(0, SEQ // BLK // 4, branch16, 0)

    def block4(r, n, first):
        row = _aligned(r * n4 + n * BLK, BLK)
        krow = row if first else _aligned(row - BLK, BLK)
        nk = BLK if first else 2 * BLK
        tab = tab_ref[0, 1, :, BLK:] if first else tab_ref[0, 1]
        state = _softmax_step(_stack_rows(qs4, None, row), k4[pl.ds(krow, nk), :],
                              v4[pl.ds(krow, nk), :], tab, load_state(st4, row))
        store_state(st1, r + 4 * BLK * n, 4, state)

    for r in range(4):
        block4(r, 0, True)

    def branch4(n, carry):
        for r in range(4):
            block4(r, n, False)
        return carry

    lax.fori_loop(1, n4 // BLK, branch4, 0)

    def block1(n, first):
        row = _aligned(n * BLK, BLK)
        krow = row if first else _aligned(row - BLK, BLK)
        nk = BLK if first else 2 * BLK
        tab = tab_ref[0, 0, :, BLK:] if first else tab_ref[0, 0]
        _, l, acc = _softmax_step(_stack_rows(qs1, None, row), k_ref[0, pl.ds(krow, nk), :],
                                  v_ref[0, pl.ds(krow, nk), :], tab, load_state(st1, row))
        o = acc / l
        o_ref[0, pl.ds(row, BLK), :] = jnp.where(head_a, o[:BLK], o[BLK:]).astype(BF16)

    block1(0, True)
    per_iter = 5
    assert (SEQ // BLK - 1) % per_iter == 0

    def branch1(it, carry):
        for g in range(per_iter):
            block1(1 + per_iter * it + g, False)
        return carry

    lax.fori_loop(0, (SEQ // BLK - 1) // per_iter, branch1, 0)


def _dilated_attention(qa, ka, va, tables):
    bsz = qa.shape[0]
    pair = pl.BlockSpec((1, SEQ, LANES), lambda p, b: (b, 0, p))
    seq_f32 = pltpu.VMEM((SEQ, LANES), F32)
    seq_bf16 = pltpu.VMEM((SEQ, LANES), BF16)
    split_bf16 = pltpu.VMEM((2, SEQ, LANES), BF16)
    state = pltpu.VMEM((3, 2, SEQ, LANES), F32)
    return pl.pallas_call(
        _dil_kernel,
        grid=(N_PAIRS, bsz),
        in_specs=[pair, pair, pair,
                  pl.BlockSpec((1, 3, 2 * BLK, 2 * BLK), lambda p, b: (p, 0, 0, 0))],
        out_specs=pair,
        out_shape=jax.ShapeDtypeStruct((bsz, SEQ, D_A), BF16),
        scratch_shapes=[seq_f32, seq_f32, split_bf16, split_bf16, split_bf16,
                        seq_bf16, seq_bf16, seq_bf16, seq_bf16, state, state],
        compiler_params=pltpu.CompilerParams(
            dimension_semantics=("parallel", "parallel"), vmem_limit_bytes=VMEM_LIMIT),
        name="dilated_attn",
    )(qa, ka, va, tables)


def _mla_kernel(qn_ref, qr_ref, kn_ref, kr_ref, v_ref, o_ref, m_scr, l_scr, acc_scr):
    i = pl.program_id(1)
    qb = QB_MLA
    scale = (B_NOPE_DIM + B_ROPE_DIM) ** -0.5
    lane = lax.broadcasted_iota(jnp.int32, (qb, LANES), 1)
    head_a = lane < B_NOPE_DIM
    slot = lane // B_ROPE_DIM

    def stacked_q(p):
        qn = qn_ref[0, :, p * LANES:(p + 1) * LANES]
        qr = qr_ref[0, :, (p // 2) * LANES:(p // 2 + 1) * LANES]
        zero = jnp.zeros_like(qn)
        sa = 2 * (p % 2)
        q_a = jnp.concatenate([jnp.where(head_a, qn, zero), jnp.where(slot == sa, qr, zero)], axis=1)
        q_b = jnp.concatenate([jnp.where(head_a, zero, qn), jnp.where(slot == sa + 1, qr, zero)], axis=1)
        return jnp.concatenate([q_a, q_b], axis=0)

    qst = [stacked_q(p) for p in range(N_PAIRS)]
    m_scr[...] = jnp.full(m_scr.shape, NEG, F32)
    l_scr[...] = jnp.zeros(l_scr.shape, F32)
    acc_scr[...] = jnp.zeros(acc_scr.shape, F32)

    def step(p, j, masked):
        row = _aligned(j * qb, qb)
        cols = slice(p * LANES, (p + 1) * LANES)
        kc = jnp.concatenate([kn_ref[0, pl.ds(row, qb), cols], kr_ref[0, pl.ds(row, qb), :]], axis=1)
        s = _nt_dot(qst[p], kc) * scale
        if masked:
            qpos = lax.broadcasted_iota(jnp.int32, (2 * qb, qb), 0) % qb
            kpos = lax.broadcasted_iota(jnp.int32, (2 * qb, qb), 1)
            s = jnp.where(qpos >= kpos, s, NEG)
        m0 = m_scr[p]
        m_new = jnp.maximum(m0, jnp.max(s, axis=1, keepdims=True))
        alpha = jnp.exp(m0 - m_new)
        pr = jnp.exp(s - jnp.concatenate([m_new] * (qb // LANES), axis=1))
        l_scr[p] = alpha * l_scr[p] + jnp.sum(pr, axis=1, keepdims=True)
        acc_scr[p] = alpha * acc_scr[p] + jnp.dot(pr.astype(BF16), v_ref[0, pl.ds(row, qb), cols],
                                                  preferred_element_type=F32)
        m_scr[p] = m_new

    def full_chunks(j, carry):
        for p in range(N_PAIRS):
            step(p, j, False)
        return carry

    lax.fori_loop(0, i, full_chunks, 0)
    for p in range(N_PAIRS):
        step(p, i, True)
    for p in range(N_PAIRS):
        o = acc_scr[p] / l_scr[p]
        o_ref[0, :, p * LANES:(p + 1) * LANES] = jnp.where(head_a, o[:qb], o[qb:]).astype(BF16)


def _latent_attention(qn, qr, kn, kr, vb):
    bsz = qn.shape[0]
    qb = QB_MLA
    state = pltpu.VMEM((N_PAIRS, 2 * qb, LANES), F32)
    blk = lambda rows, w: pl.BlockSpec((1, rows, w), lambda b, i: (b, i if rows == qb else 0, 0))
    return pl.pallas_call(
        _mla_kernel,
        grid=(bsz, SEQ // qb),
        in_specs=[blk(qb, D_B), blk(qb, B_HEADS * B_ROPE_DIM), blk(SEQ, D_B), blk(SEQ, LANES),
                  blk(SEQ, D_B)],
        out_specs=blk(qb, D_B),
        out_shape=jax.ShapeDtypeStruct((bsz, SEQ, D_B), BF16),
        scratch_shapes=[state, state, state],
        compiler_params=pltpu.CompilerParams(
            dimension_semantics=("parallel", "parallel"), vmem_limit_bytes=VMEM_LIMIT),
        name="latent_attn",
    )(qn, qr, kn, kr, vb)


def _post_kernel(x_ref, oa_ref, ob_ref, g1_ref, sh2_ref, sc2_ref, g2_ref, goa_ref, gob_ref,
                 wout_ref, gn2_ref, wfi_ref, wfo_ref, gfin_ref, o_ref, a_scr):
    ya = _rms(oa_ref[0].astype(F32), goa_ref[...])
    yb = _rms(ob_ref[0].astype(F32), gob_ref[...])
    y = jnp.concatenate([ya, yb], axis=1).astype(BF16)
    x1 = x_ref[0] + g1_ref[0] * jnp.dot(y, wout_ref[...], preferred_element_type=F32)
    h2 = (_rms(x1, gn2_ref[...]) * (1.0 + sc2_ref[0]) + sh2_ref[0]).astype(BF16)
    for c in range(D_FF // FF_CHUNK):
        lo = c * FF_CHUNK
        g = jnp.dot(h2, wfi_ref[:, lo:lo + FF_CHUNK], preferred_element_type=F32)
        u = jnp.dot(h2, wfi_ref[:, D_FF + lo:D_FF + lo + FF_CHUNK], preferred_element_type=F32)
        a_scr[:, lo:lo + FF_CHUNK] = (jax.nn.silu(g) * u).astype(BF16)
    x2 = x1 + g2_ref[0] * jnp.dot(a_scr[...], wfo_ref[...], preferred_element_type=F32)
    o_ref[0] = _rms(x2, gfin_ref[...])


def _post(x, oa, ob, g1m, sh2, sc2, g2m, goa, gob, wout, gn2, wfi, wfo, gfin):
    bsz = x.shape[0]
    tm = TM_POST
    tok = lambda w: pl.BlockSpec((1, tm, w), lambda b, j: (b, j, 0))
    per_b = pl.BlockSpec((1, 1, D_MODEL), lambda b, j: (b, 0, 0))
    return pl.pallas_call(
        _post_kernel,
        grid=(bsz, SEQ // tm),
        in_specs=[tok(D_MODEL), tok(D_A), tok(D_B), per_b, per_b, per_b, per_b,
                  _const_spec((1, D_A)), _const_spec((1, D_B)), _const_spec(wout.shape),
                  _const_spec((1, D_MODEL)), _const_spec(wfi.shape), _const_spec(wfo.shape),
                  _const_spec((1, D_MODEL))],
        out_specs=tok(D_MODEL),
        out_shape=jax.ShapeDtypeStruct((bsz, SEQ, D_MODEL), F32),
        scratch_shapes=[pltpu.VMEM((tm, D_FF), BF16)],
        compiler_params=pltpu.CompilerParams(
            dimension_semantics=("parallel", "parallel"), vmem_limit_bytes=VMEM_LIMIT),
        name="out_proj_ffn",
    )(x, oa, ob, g1m, sh2, sc2, g2m, goa, gob, wout, gn2, wfi, wfo, gfin)


def _rotate_half_cols(w):
    lead = w.shape[:-1]
    half = B_ROPE_DIM // 2
    w = w.reshape(lead + (-1, 2, half))
    return jnp.stack([-w[..., 1, :], w[..., 0, :]], axis=-2).reshape(lead + (-1,))


def _t5_bucket(dist):
    max_exact = N_BUCKETS // 2
    d = np.maximum(dist, 1).astype(np.float64)
    large = max_exact + (np.log(d / max_exact) / np.log(MAX_DISTANCE / max_exact)
                         * (N_BUCKETS - max_exact)).astype(np.int64)
    large = np.minimum(large, N_BUCKETS - 1)
    return np.where(dist < max_exact, dist, large).astype(np.int32)


def _bias_kernel(bucket_ref, rb_ref, o_ref):
    p = pl.program_id(0)
    for br in range(len(DILATIONS)):
        bucket = bucket_ref[br]
        for hh in range(2):
            tab = jnp.full((BLK, 2 * BLK), NEG, F32)
            for k in range(N_BUCKETS):
                tab = jnp.where(bucket == k, rb_ref[k, 2 * p + hh], tab)
            o_ref[0, br, hh * BLK:(hh + 1) * BLK, :] = tab


def _bias_tables(rel_bias):
    a = np.arange(BLK)[:, None]
    bk = np.arange(2 * BLK)[None, :]
    steps = BLK + a - bk
    valid = (steps >= 0) & (steps <= BLK)
    bucket = np.stack([np.where(valid, _t5_bucket(np.clip(steps, 0, BLK) * dil), -1)
                       for dil in DILATIONS]).astype(np.int32)
    nbr = len(DILATIONS)
    return pl.pallas_call(
        _bias_kernel,
        grid=(N_PAIRS,),
        in_specs=[pl.BlockSpec((nbr, BLK, 2 * BLK), lambda p: (0, 0, 0)),
                  pl.BlockSpec(memory_space=pltpu.SMEM)],
        out_specs=pl.BlockSpec((1, nbr, 2 * BLK, 2 * BLK), lambda p: (p, 0, 0, 0)),
        out_shape=jax.ShapeDtypeStruct((N_PAIRS, nbr, 2 * BLK, 2 * BLK), F32),
        name="bias_tables",
    )(jnp.asarray(bucket), rel_bias)


def _rope_tables():
    half = B_ROPE_DIM // 2
    inv = ROPE_THETA ** (-jnp.arange(half, dtype=F32) / half)
    ang = jnp.arange(SEQ, dtype=F32)[:, None] * inv[None, :]
    reps = LANES // half
    return jnp.tile(jnp.cos(ang), (1, reps)), jnp.tile(jnp.sin(ang), (1, reps))


def kernel(x, c, w_ada, b_ada, g_norm1, w_in, g_cq, w_uq, g_ckv, w_ukv, rel_bias, g_out_a, g_out_b,
           w_out, g_norm2, w_ffn_in, w_ffn_out, g_final):
    bsz = x.shape[0]
    assert x.shape == (bsz, SEQ, D_MODEL) and w_ada.shape[0] == 1
    l = 0

    mod = _modulation(c, w_ada[l], b_ada[l]).reshape(bsz, N_MOD, 1, D_MODEL)
    sh1, sc1, g1m, sh2, sc2, g2m = (mod[:, i] for i in range(N_MOD))

    wi = w_in[l]
    kr_cols = wi[:, 3 * D_A + Q_LORA + KV_LORA:]
    reps = LANES // B_ROPE_DIM
    win = jnp.concatenate([wi[:, :D_A] * (A_HEAD_DIM ** -0.5), wi[:, D_A:3 * D_A + Q_LORA + KV_LORA],
                           jnp.tile(kr_cols, (1, reps)), jnp.tile(_rotate_half_cols(kr_cols), (1, reps))],
                          axis=1).astype(BF16)
    wq = w_uq[l].reshape(Q_LORA, B_HEADS, B_NOPE_DIM + B_ROPE_DIM)
    wq_rope = wq[:, :, B_NOPE_DIM:].reshape(Q_LORA, -1)
    wuq = jnp.concatenate([wq[:, :, :B_NOPE_DIM].reshape(Q_LORA, -1), wq_rope,
                           _rotate_half_cols(wq_rope)], axis=1).astype(BF16)
    wkv = w_ukv[l].reshape(KV_LORA, B_HEADS, B_NOPE_DIM + B_V_DIM)
    wukv = jnp.concatenate([wkv[:, :, :B_NOPE_DIM].reshape(KV_LORA, -1),
                            wkv[:, :, B_NOPE_DIM:].reshape(KV_LORA, -1)], axis=1).astype(BF16)
    cos_t, sin_t = _rope_tables()

    qa, ka, va, qn, qr, kn, kr, vb = _projections(
        x, sh1, sc1, g_norm1[l].reshape(1, -1), cos_t, sin_t, win, g_cq[l].reshape(1, -1), wuq,
        g_ckv[l].reshape(1, -1), wukv)

    out_a = _dilated_attention(qa, ka, va, _bias_tables(rel_bias))
    out_b = _latent_attention(qn, qr, kn, kr, vb)

    return _post(x, out_a, out_b, g1m, sh2, sc2, g2m, g_out_a[l].reshape(1, -1),
                 g_out_b[l].reshape(1, -1), w_out[l].astype(BF16), g_norm2[l].reshape(1, -1),
                 w_ffn_in[l].astype(BF16), w_ffn_out[l].astype(BF16), g_final.reshape(1, -1))
```

```python
import functools

import jax
import jax.numpy as jnp
import numpy as np
from jax import lax
from jax.experimental import pallas as pl
from jax.experimental.pallas import tpu as pltpu

F32 = jnp.float32
BF16 = jnp.bfloat16

D_MODEL = 1024
SEQ = 2048
A_HEADS = 8
A_HEAD_DIM = 64
D_A = A_HEADS * A_HEAD_DIM
DILATIONS = (1, 4, 16)
BLK = 128
B_HEADS = 8
B_NOPE_DIM = 64
B_ROPE_DIM = 32
B_V_DIM = 64
D_B = B_HEADS * B_V_DIM
Q_LORA = 384
KV_LORA = 256
ROPE_THETA = 10000.0
N_BUCKETS = 32
MAX_DISTANCE = 2048
D_FF = 2816
N_MOD = 6
EPS = 1e-6
NEG = -1e30
LOG2E = 1.4426950408889634

LANES = 128
N_PAIRS = A_HEADS // 2
VMEM_LIMIT = 56 * 1024 * 1024

_C_QA, _C_KA, _C_VA = 0, D_A, 2 * D_A
_C_CQ = 3 * D_A
_C_CKV = _C_CQ + Q_LORA
_C_KR = _C_CKV + KV_LORA
_C_KROT = _C_KR + LANES
P_IN2 = _C_KROT + LANES

TM_PROJ = 512
TM_POST = 512
FF_CHUNK = 256
QB_MLA = 256
KC_MLA = 512


def _nt_dot(a, b):
    return lax.dot_general(a, b, (((1,), (1,)), ((), ())), preferred_element_type=F32)


def _aligned(i, m):
    return i if isinstance(i, int) else pl.multiple_of(i, m)


def _rms(x, g):
    return x * lax.rsqrt(jnp.mean(x * x, axis=-1, keepdims=True) + EPS) * g


def _mod_kernel(c_ref, w_ref, b_ref, o_ref):
    cond = jax.nn.silu(c_ref[...])
    o_ref[...] = jnp.dot(cond.astype(BF16), w_ref[...].astype(BF16),
                         preferred_element_type=F32) + b_ref[...]


def _modulation(c, w_ada, b_ada):
    bsz = c.shape[0]
    n = w_ada.shape[1]
    bn = 1536
    return pl.pallas_call(
        _mod_kernel,
        grid=(n // bn,),
        in_specs=[pl.BlockSpec((bsz, D_MODEL), lambda j: (0, 0)),
                  pl.BlockSpec((D_MODEL, bn), lambda j: (0, j)),
                  pl.BlockSpec((1, bn), lambda j: (0, j))],
        out_specs=pl.BlockSpec((bsz, bn), lambda j: (0, j)),
        out_shape=jax.ShapeDtypeStruct((bsz, n), F32),
        compiler_params=pltpu.CompilerParams(vmem_limit_bytes=VMEM_LIMIT),
        name="adaln_mod",
    )(c, w_ada, b_ada.reshape(1, n))


def _proj_kernel(x_ref, sh_ref, sc_ref, g1_ref, cos_ref, sin_ref, win_ref, gcq_ref, wuq_ref,
                 gckv_ref, wukv_ref,
                 qa_ref, ka_ref, va_ref, qn_ref, qr_ref, kn_ref, kr_ref, vb_ref):
    h = _rms(x_ref[0], g1_ref[...]) * (1.0 + sc_ref[0]) + sh_ref[0]
    hb = h.astype(BF16)

    def proj(lo, hi):
        return jnp.dot(hb, win_ref[:, lo:hi], preferred_element_type=F32)

    qa_ref[0] = proj(_C_QA, _C_KA).astype(BF16)
    ka_ref[0] = proj(_C_KA, _C_VA).astype(BF16)
    va_ref[0] = proj(_C_VA, _C_CQ).astype(BF16)

    cos = cos_ref[...]
    sin = sin_ref[...]
    kr_ref[0] = (proj(_C_KR, _C_KROT) * cos + proj(_C_KROT, P_IN2) * sin).astype(BF16)

    cq = _rms(proj(_C_CQ, _C_CKV), gcq_ref[...]).astype(BF16)
    q = jnp.dot(cq, wuq_ref[...], preferred_element_type=F32)
    qn_ref[0] = q[:, :D_B].astype(BF16)
    cos2 = jnp.concatenate([cos, cos], axis=1)
    sin2 = jnp.concatenate([sin, sin], axis=1)
    n_r = B_HEADS * B_ROPE_DIM
    qr_ref[0] = (q[:, D_B:D_B + n_r] * cos2 + q[:, D_B + n_r:] * sin2).astype(BF16)

    ckv = _rms(proj(_C_CKV, _C_KR), gckv_ref[...]).astype(BF16)
    kv = jnp.dot(ckv, wukv_ref[...], preferred_element_type=F32)
    kn_ref[0] = kv[:, :D_B].astype(BF16)
    vb_ref[0] = kv[:, D_B:].astype(BF16)


def _const_spec(shape):
    nd = len(shape)
    return pl.BlockSpec(shape, lambda *_: (0,) * nd, pipeline_mode=pl.Buffered(1))


def _projections(x, sh1, sc1, g1, cos_t, sin_t, win, gcq, wuq, gckv, wukv):
    bsz = x.shape[0]
    tm = TM_PROJ
    tok = lambda w: pl.BlockSpec((1, tm, w), lambda b, j: (b, j, 0))
    per_b = pl.BlockSpec((1, 1, D_MODEL), lambda b, j: (b, 0, 0))
    tab = pl.BlockSpec((tm, LANES), lambda b, j: (j, 0))
    widths = (D_A, D_A, D_A, D_B, B_HEADS * B_ROPE_DIM, D_B, LANES, D_B)
    return pl.pallas_call(
        _proj_kernel,
        grid=(bsz, SEQ // tm),
        in_specs=[tok(D_MODEL), per_b, per_b, _const_spec((1, D_MODEL)), tab, tab,
                  _const_spec(win.shape), _const_spec((1, Q_LORA)), _const_spec(wuq.shape),
                  _const_spec((1, KV_LORA)), _const_spec(wukv.shape)],
        out_specs=[tok(w) for w in widths],
        out_shape=[jax.ShapeDtypeStruct((bsz, SEQ, w), BF16) for w in widths],
        compiler_params=pltpu.CompilerParams(
            dimension_semantics=("parallel", "parallel"), vmem_limit_bytes=VMEM_LIMIT),
        name="in_proj",
    )(x, sh1, sc1, g1, cos_t, sin_t, win, gcq, wuq, gckv, wukv)


def _stack_rows(ref, c, row):
    idx = (c,) if c is not None else ()
    return jnp.concatenate([ref[idx + (0, pl.ds(row, BLK), slice(None))],
                            ref[idx + (1, pl.ds(row, BLK), slice(None))]], axis=0)


def _softmax_step(qst, kb, vb, tab, state):
    s = _nt_dot(qst, kb) + tab
    nk = s.shape[1]
    m_cur = jnp.max(s, axis=1, keepdims=True)
    if state is None:
        m_new = jnp.broadcast_to(m_cur, (2 * BLK, LANES))
    else:
        m0, l0, acc0 = state
        m_new = jnp.maximum(m0, m_cur)
    m_wide = m_new if nk == LANES else jnp.concatenate([m_new] * (nk // LANES), axis=1)
    p = jnp.exp2(s - m_wide)
    l_new = jnp.sum(p, axis=1, keepdims=True)
    acc = jnp.dot(p.astype(BF16), vb, preferred_element_type=F32)
    if state is None:
        l_new = jnp.broadcast_to(l_new, (2 * BLK, LANES))
    else:
        alpha = jnp.exp2(m0 - m_new)
        l_new = alpha * l0 + l_new
        acc = alpha * acc0 + acc
    return m_new, l_new, acc


def _dil_kernel(q_ref, k_ref, v_ref, tab_ref, o_ref,
                fa, fb, qs1, qs4, qs16, k4, k16, v4, v16, st4, st1):
    lane = lax.broadcasted_iota(jnp.int32, (BLK, LANES), 1)
    head_a = lane < A_HEAD_DIM
    n4 = SEQ // 4

    def split_heads(dst, row, x):
        zero = jnp.zeros_like(x)
        dst[0, pl.ds(row, BLK), :] = jnp.where(head_a, x, zero)
        dst[1, pl.ds(row, BLK), :] = jnp.where(head_a, zero, x)

    def to_residue_orders(src_ref):
        fa[...] = src_ref[0].astype(F32)
        for r in range(4):
            fb[r * n4:(r + 1) * n4, :] = fa[pl.ds(r, n4, stride=4), :]

    def residue16_block(j):
        r4, q4 = j % 4, j // 4
        return fb[pl.ds(r4 * n4 + q4, BLK, stride=4), :]

    to_residue_orders(q_ref)
    for j in range(SEQ // BLK):
        split_heads(qs1, j * BLK, q_ref[0, j * BLK:(j + 1) * BLK, :])
        split_heads(qs4, j * BLK, fb[j * BLK:(j + 1) * BLK, :].astype(BF16))
        split_heads(qs16, j * BLK, residue16_block(j).astype(BF16))
    to_residue_orders(k_ref)
    k4[...] = fb[...].astype(BF16)
    for j in range(SEQ // BLK):
        k16[j * BLK:(j + 1) * BLK, :] = residue16_block(j).astype(BF16)
    to_residue_orders(v_ref)
    v4[...] = fb[...].astype(BF16)
    for j in range(SEQ // BLK):
        v16[j * BLK:(j + 1) * BLK, :] = residue16_block(j).astype(BF16)

    def store_state(dst, start, stride, state):
        for c, x in enumerate(state):
            rows = pl.ds(start, BLK, stride=stride) if stride > 1 else pl.ds(start, BLK)
            dst[c, 0, rows, :] = x[:BLK]
            dst[c, 1, rows, :] = x[BLK:]

    def load_state(src, row):
        return tuple(_stack_rows(src, c, row) for c in range(3))


    def block16(j, start):
        row = _aligned(j * BLK, BLK)
        state = _softmax_step(_stack_rows(qs16, None, row), k16[pl.ds(row, BLK), :],
                              v16[pl.ds(row, BLK), :], tab_ref[0, 2, :, BLK:], None)
        store_state(st4, start, 4, state)

    def branch16(it, carry):
        for r4 in range(4):
            block16(4 * it + r4, r4 * n4 + it)
        return carry

    for it in range(SEQ // BLK // 4):
        branch16(it, 0)

    def block4(r, n, first):
        row = _aligned(r * n4 + n * BLK, BLK)
        krow = row if first else _aligned(row - BLK, BLK)
        nk = BLK if first else 2 * BLK
        tab = tab_ref[0, 1, :, BLK:] if first else tab_ref[0, 1]
        state = _softmax_step(_stack_rows(qs4, None, row), k4[pl.ds(krow, nk), :],
                              v4[pl.ds(krow, nk), :], tab, load_state(st4, row))
        store_state(st1, r + 4 * BLK * n, 4, state)

    for r in range(4):
        block4(r, 0, True)

    def branch4(n, carry):
        for r in range(4):
            block4(r, n, False)
        return carry

    for n in range(1, n4 // BLK):
        branch4(n, 0)

    def block1(n, first):
        row = _aligned(n * BLK, BLK)
        krow = row if first else _aligned(row - BLK, BLK)
        nk = BLK if first else 2 * BLK
        tab = tab_ref[0, 0, :, BLK:] if first else tab_ref[0, 0]
        _, l, acc = _softmax_step(_stack_rows(qs1, None, row), k_ref[0, pl.ds(krow, nk), :],
                                  v_ref[0, pl.ds(krow, nk), :], tab, load_state(st1, row))
        o = acc / l
        o_ref[0, pl.ds(row, BLK), :] = jnp.where(head_a, o[:BLK], o[BLK:]).astype(BF16)

    block1(0, True)
    per_iter = 5
    assert (SEQ // BLK - 1) % per_iter == 0

    def branch1(it, carry):
        for g in range(per_iter):
            block1(1 + per_iter * it + g, False)
        return carry

    for it in range((SEQ // BLK - 1) // per_iter):
        branch1(it, 0)


def _dilated_attention(qa, ka, va, tables):
    bsz = qa.shape[0]
    pair = pl.BlockSpec((1, SEQ, LANES), lambda p, b: (b, 0, p))
    seq_f32 = pltpu.VMEM((SEQ, LANES), F32)
    seq_bf16 = pltpu.VMEM((SEQ, LANES), BF16)
    split_bf16 = pltpu.VMEM((2, SEQ, LANES), BF16)
    state = pltpu.VMEM((3, 2, SEQ, LANES), F32)
    return pl.pallas_call(
        _dil_kernel,
        grid=(N_PAIRS, bsz),
        in_specs=[pair, pair, pair,
                  pl.BlockSpec((1, 3, 2 * BLK, 2 * BLK), lambda p, b: (p, 0, 0, 0))],
        out_specs=pair,
        out_shape=jax.ShapeDtypeStruct((bsz, SEQ, D_A), BF16),
        scratch_shapes=[seq_f32, seq_f32, split_bf16, split_bf16, split_bf16,
                        seq_bf16, seq_bf16, seq_bf16, seq_bf16, state, state],
        compiler_params=pltpu.CompilerParams(
            dimension_semantics=("parallel", "parallel"), vmem_limit_bytes=VMEM_LIMIT),
        name="dilated_attn",
    )(qa, ka, va, tables)


def _mla_kernel(qn_ref, qr_ref, kn_ref, kr_ref, v_ref, o_ref):
    i = pl.program_id(1)
    qb = QB_MLA
    kc = KC_MLA
    lane = lax.broadcasted_iota(jnp.int32, (qb, LANES), 1)
    head_a = lane < B_NOPE_DIM
    slot = lane // B_ROPE_DIM

    def stacked_q(p):
        qn = qn_ref[0, :, p * LANES:(p + 1) * LANES]
        qr = qr_ref[0, :, (p // 2) * LANES:(p // 2 + 1) * LANES]
        zero = jnp.zeros_like(qn)
        sa = 2 * (p % 2)
        q_a = jnp.concatenate([jnp.where(head_a, qn, zero), jnp.where(slot == sa, qr, zero)], axis=1)
        q_b = jnp.concatenate([jnp.where(head_a, zero, qn), jnp.where(slot == sa + 1, qr, zero)], axis=1)
        return jnp.concatenate([q_a, q_b], axis=0)

    qst = [stacked_q(p) for p in range(N_PAIRS)]

    def step(p, c, masked, state):
        cols = slice(p * LANES, (p + 1) * LANES)
        kcat = jnp.concatenate([kn_ref[0, c * kc:(c + 1) * kc, cols], kr_ref[0, c * kc:(c + 1) * kc, :]],
                               axis=1)
        s = _nt_dot(qst[p], kcat)
        if masked:
            qpos = lax.broadcasted_iota(jnp.int32, (2 * qb, kc), 0) % qb + (i * qb - c * kc)
            kpos = lax.broadcasted_iota(jnp.int32, (2 * qb, kc), 1)
            s = jnp.where(qpos >= kpos, s, NEG)
        m_cur = jnp.max(s, axis=1, keepdims=True)
        m_new = jnp.broadcast_to(m_cur, (2 * qb, LANES)) if state is None else jnp.maximum(state[0], m_cur)
        pr = jnp.exp2(s - jnp.concatenate([m_new] * (kc // LANES), axis=1))
        l_new = jnp.sum(pr, axis=1, keepdims=True)
        acc = jnp.dot(pr.astype(BF16), v_ref[0, c * kc:(c + 1) * kc, cols], preferred_element_type=F32)
        if state is None:
            return m_new, jnp.broadcast_to(l_new, (2 * qb, LANES)), acc
        alpha = jnp.exp2(state[0] - m_new)
        return m_new, alpha * state[1] + l_new, alpha * state[2] + acc

    def run(n_full):
        states = [None] * N_PAIRS
        for c in range(n_full + 1):
            for p in range(N_PAIRS):
                states[p] = step(p, c, c == n_full, states[p])
        for p in range(N_PAIRS):
            _, l, acc = states[p]
            o = acc / l
            o_ref[0, :, p * LANES:(p + 1) * LANES] = jnp.where(head_a, o[:qb], o[qb:]).astype(BF16)

    for n_full in range(SEQ // kc):
        pl.when((i * qb) // kc == n_full)(functools.partial(run, n_full))


def _latent_attention(qn, qr, kn, kr, vb):
    bsz = qn.shape[0]
    qb = QB_MLA
    blk = lambda rows, w: pl.BlockSpec((1, rows, w), lambda b, i: (b, i if rows == qb else 0, 0))
    return pl.pallas_call(
        _mla_kernel,
        grid=(bsz, SEQ // qb),
        in_specs=[blk(qb, D_B), blk(qb, B_HEADS * B_ROPE_DIM), blk(SEQ, D_B), blk(SEQ, LANES),
                  blk(SEQ, D_B)],
        out_specs=blk(qb, D_B),
        out_shape=jax.ShapeDtypeStruct((bsz, SEQ, D_B), BF16),
        compiler_params=pltpu.CompilerParams(
            dimension_semantics=("parallel", "parallel"), vmem_limit_bytes=VMEM_LIMIT),
        name="latent_attn",
    )(qn, qr, kn, kr, vb)


def _post_kernel(x_ref, oa_ref, ob_ref, g1_ref, sh2_ref, sc2_ref, g2_ref, goa_ref, gob_ref,
                 wout_ref, gn2_ref, wfi_ref, wfo_ref, gfin_ref, o_ref, a_scr):
    ya = _rms(oa_ref[0].astype(F32), goa_ref[...])
    yb = _rms(ob_ref[0].astype(F32), gob_ref[...])
    y = jnp.concatenate([ya, yb], axis=1).astype(BF16)
    x1 = x_ref[0] + g1_ref[0] * jnp.dot(y, wout_ref[...], preferred_element_type=F32)
    h2 = (_rms(x1, gn2_ref[...]) * (1.0 + sc2_ref[0]) + sh2_ref[0]).astype(BF16)
    for c in range(D_FF // FF_CHUNK):
        lo = c * FF_CHUNK
        g = jnp.dot(h2, wfi_ref[:, lo:lo + FF_CHUNK], preferred_element_type=F32)
        u = jnp.dot(h2, wfi_ref[:, D_FF + lo:D_FF + lo + FF_CHUNK], preferred_element_type=F32)
        a_scr[:, lo:lo + FF_CHUNK] = (jax.nn.silu(g) * u).astype(BF16)
    x2 = x1 + g2_ref[0] * jnp.dot(a_scr[...], wfo_ref[...], preferred_element_type=F32)
    o_ref[0] = _rms(x2, gfin_ref[...])


def _post(x, oa, ob, g1m, sh2, sc2, g2m, goa, gob, wout, gn2, wfi, wfo, gfin):
    bsz = x.shape[0]
    tm = TM_POST
    tok = lambda w: pl.BlockSpec((1, tm, w), lambda b, j: (b, j, 0))
    per_b = pl.BlockSpec((1, 1, D_MODEL), lambda b, j: (b, 0, 0))
    return pl.pallas_call(
        _post_kernel,
        grid=(bsz, SEQ // tm),
        in_specs=[tok(D_MODEL), tok(D_A), tok(D_B), per_b, per_b, per_b, per_b,
                  _const_spec((1, D_A)), _const_spec((1, D_B)), _const_spec(wout.shape),
                  _const_spec((1, D_MODEL)), _const_spec(wfi.shape), _const_spec(wfo.shape),
                  _const_spec((1, D_MODEL))],
        out_specs=tok(D_MODEL),
        out_shape=jax.ShapeDtypeStruct((bsz, SEQ, D_MODEL), F32),
        scratch_shapes=[pltpu.VMEM((tm, D_FF), BF16)],
        compiler_params=pltpu.CompilerParams(
            dimension_semantics=("parallel", "parallel"), vmem_limit_bytes=VMEM_LIMIT),
        name="out_proj_ffn",
    )(x, oa, ob, g1m, sh2, sc2, g2m, goa, gob, wout, gn2, wfi, wfo, gfin)


def _rotate_half_cols(w):
    lead = w.shape[:-1]
    half = B_ROPE_DIM // 2
    w = w.reshape(lead + (-1, 2, half))
    return jnp.stack([-w[..., 1, :], w[..., 0, :]], axis=-2).reshape(lead + (-1,))


def _t5_bucket(dist):
    max_exact = N_BUCKETS // 2
    d = np.maximum(dist, 1).astype(np.float64)
    large = max_exact + (np.log(d / max_exact) / np.log(MAX_DISTANCE / max_exact)
                         * (N_BUCKETS - max_exact)).astype(np.int64)
    large = np.minimum(large, N_BUCKETS - 1)
    return np.where(dist < max_exact, dist, large).astype(np.int32)


def _bias_kernel(bucket_ref, rb_ref, o_ref):
    p = pl.program_id(0)
    for br in range(len(DILATIONS)):
        bucket = bucket_ref[br]
        for hh in range(2):
            tab = jnp.full((BLK, 2 * BLK), NEG, F32)
            for k in range(N_BUCKETS):
                tab = jnp.where(bucket == k, rb_ref[k, 2 * p + hh] * LOG2E, tab)
            o_ref[0, br, hh * BLK:(hh + 1) * BLK, :] = tab


def _bias_tables(rel_bias):
    a = np.arange(BLK)[:, None]
    bk = np.arange(2 * BLK)[None, :]
    steps = BLK + a - bk
    valid = (steps >= 0) & (steps <= BLK)
    bucket = np.stack([np.where(valid, _t5_bucket(np.clip(steps, 0, BLK) * dil), -1)
                       for dil in DILATIONS]).astype(np.int32)
    nbr = len(DILATIONS)
    return pl.pallas_call(
        _bias_kernel,
        grid=(N_PAIRS,),
        in_specs=[pl.BlockSpec((nbr, BLK, 2 * BLK), lambda p: (0, 0, 0)),
                  pl.BlockSpec(memory_space=pltpu.SMEM)],
        out_specs=pl.BlockSpec((1, nbr, 2 * BLK, 2 * BLK), lambda p: (p, 0, 0, 0)),
        out_shape=jax.ShapeDtypeStruct((N_PAIRS, nbr, 2 * BLK, 2 * BLK), F32),
        name="bias_tables",
    )(jnp.asarray(bucket), rel_bias)


def _rope_tables():
    half = B_ROPE_DIM // 2
    inv = ROPE_THETA ** (-jnp.arange(half, dtype=F32) / half)
    ang = jnp.arange(SEQ, dtype=F32)[:, None] * inv[None, :]
    reps = LANES // half
    return jnp.tile(jnp.cos(ang), (1, reps)), jnp.tile(jnp.sin(ang), (1, reps))


def kernel(x, c, w_ada, b_ada, g_norm1, w_in, g_cq, w_uq, g_ckv, w_ukv, rel_bias, g_out_a, g_out_b,
           w_out, g_norm2, w_ffn_in, w_ffn_out, g_final):
    bsz = x.shape[0]
    assert x.shape == (bsz, SEQ, D_MODEL) and w_ada.shape[0] == 1
    l = 0

    mod = _modulation(c, w_ada[l], b_ada[l]).reshape(bsz, N_MOD, 1, D_MODEL)
    sh1, sc1, g1m, sh2, sc2, g2m = (mod[:, i] for i in range(N_MOD))

    wi = w_in[l]
    kr_cols = wi[:, 3 * D_A + Q_LORA + KV_LORA:]
    reps = LANES // B_ROPE_DIM
    win = jnp.concatenate([wi[:, :D_A] * (A_HEAD_DIM ** -0.5 * LOG2E), wi[:, D_A:3 * D_A + Q_LORA + KV_LORA],
                           jnp.tile(kr_cols, (1, reps)), jnp.tile(_rotate_half_cols(kr_cols), (1, reps))],
                          axis=1).astype(BF16)
    wq = (w_uq[l] * ((B_NOPE_DIM + B_ROPE_DIM) ** -0.5 * LOG2E)).reshape(
        Q_LORA, B_HEADS, B_NOPE_DIM + B_ROPE_DIM)
    wq_rope = wq[:, :, B_NOPE_DIM:].reshape(Q_LORA, -1)
    wuq = jnp.concatenate([wq[:, :, :B_NOPE_DIM].reshape(Q_LORA, -1), wq_rope,
                           _rotate_half_cols(wq_rope)], axis=1).astype(BF16)
    wkv = w_ukv[l].reshape(KV_LORA, B_HEADS, B_NOPE_DIM + B_V_DIM)
    wukv = jnp.concatenate([wkv[:, :, :B_NOPE_DIM].reshape(KV_LORA, -1),
                            wkv[:, :, B_NOPE_DIM:].reshape(KV_LORA, -1)], axis=1).astype(BF16)
    cos_t, sin_t = _rope_tables()

    qa, ka, va, qn, qr, kn, kr, vb = _projections(
        x, sh1, sc1, g_norm1[l].reshape(1, -1), cos_t, sin_t, win, g_cq[l].reshape(1, -1), wuq,
        g_ckv[l].reshape(1, -1), wukv)

    out_a = _dilated_attention(qa, ka, va, _bias_tables(rel_bias))
    out_b = _latent_attention(qn, qr, kn, kr, vb)

    return _post(x, out_a, out_b, g1m, sh2, sc2, g2m, g_out_a[l].reshape(1, -1),
                 g_out_b[l].reshape(1, -1), w_out[l].astype(BF16), g_norm2[l].reshape(1, -1),
                 w_ffn_in[l].astype(BF16), w_ffn_out[l].astype(BF16), g_final.reshape(1, -1))
```

```python
import functools

import jax
import jax.numpy as jnp
import numpy as np
from jax import lax
from jax.experimental import pallas as pl
from jax.experimental.pallas import tpu as pltpu

F32 = jnp.float32
BF16 = jnp.bfloat16

D_MODEL = 1024
SEQ = 2048
A_HEADS = 8
A_HEAD_DIM = 64
D_A = A_HEADS * A_HEAD_DIM
DILATIONS = (1, 4, 16)
BLK = 128
B_HEADS = 8
B_NOPE_DIM = 64
B_ROPE_DIM = 32
B_V_DIM = 64
D_B = B_HEADS * B_V_DIM
Q_LORA = 384
KV_LORA = 256
ROPE_THETA = 10000.0
N_BUCKETS = 32
MAX_DISTANCE = 2048
D_FF = 2816
N_MOD = 6
EPS = 1e-6
NEG = -1e30
LOG2E = 1.4426950408889634

LANES = 128
N_PAIRS = A_HEADS // 2
VMEM_LIMIT = 56 * 1024 * 1024

_C_QA, _C_KA, _C_VA = 0, D_A, 2 * D_A
_C_CQ = 3 * D_A
_C_CKV = _C_CQ + Q_LORA
_C_KR = _C_CKV + KV_LORA
_C_KROT = _C_KR + LANES
P_IN2 = _C_KROT + LANES

TM_PROJ = 1024
SUB_PROJ = 512
TM_POST = 512
FF_CHUNK = 256
LOOKAHEAD = 2
QB_MLA = 256
KC_MLA = 512


def _nt_dot(a, b):
    return lax.dot_general(a, b, (((1,), (1,)), ((), ())), preferred_element_type=F32)


def _aligned(i, m):
    return i if isinstance(i, int) else pl.multiple_of(i, m)


def _rms(x, g):
    return x * lax.rsqrt(jnp.mean(x * x, axis=-1, keepdims=True) + EPS) * g


def _mod_kernel(c_ref, w_ref, b_ref, o_ref):
    cond = jax.nn.silu(c_ref[...])
    o_ref[...] = jnp.dot(cond.astype(BF16), w_ref[...].astype(BF16),
                         preferred_element_type=F32) + b_ref[...]


def _modulation(c, w_ada, b_ada):
    bsz = c.shape[0]
    n = w_ada.shape[1]
    bn = 1536
    return pl.pallas_call(
        _mod_kernel,
        grid=(n // bn,),
        in_specs=[pl.BlockSpec((bsz, D_MODEL), lambda j: (0, 0)),
                  pl.BlockSpec((D_MODEL, bn), lambda j: (0, j)),
                  pl.BlockSpec((1, bn), lambda j: (0, j))],
        out_specs=pl.BlockSpec((bsz, bn), lambda j: (0, j)),
        out_shape=jax.ShapeDtypeStruct((bsz, n), F32),
        compiler_params=pltpu.CompilerParams(vmem_limit_bytes=VMEM_LIMIT),
        name="adaln_mod",
    )(c, w_ada, b_ada.reshape(1, n))


def _proj_kernel(x_ref, sh_ref, sc_ref, g1_ref, cos_ref, sin_ref, win_ref, gcq_ref, wuq_ref,
                 gckv_ref, wukv_ref,
                 qa_ref, ka_ref, va_ref, qn_ref, qr_ref, kn_ref, kr_ref, vb_ref):
    for t in range(TM_PROJ // SUB_PROJ):
        rows = slice(t * SUB_PROJ, (t + 1) * SUB_PROJ)
        h = _rms(x_ref[0, rows, :], g1_ref[...]) * (1.0 + sc_ref[0]) + sh_ref[0]
        hb = h.astype(BF16)

        def proj(lo, hi):
            return jnp.dot(hb, win_ref[:, lo:hi], preferred_element_type=F32)

        qa_ref[0, rows, :] = proj(_C_QA, _C_KA).astype(BF16)
        ka_ref[0, rows, :] = proj(_C_KA, _C_VA).astype(BF16)
        va_ref[0, rows, :] = proj(_C_VA, _C_CQ).astype(BF16)

        cos = cos_ref[rows, :]
        sin = sin_ref[rows, :]
        kr_ref[0, rows, :] = (proj(_C_KR, _C_KROT) * cos + proj(_C_KROT, P_IN2) * sin).astype(BF16)

        cq = _rms(proj(_C_CQ, _C_CKV), gcq_ref[...]).astype(BF16)
        q = jnp.dot(cq, wuq_ref[...], preferred_element_type=F32)
        qn_ref[0, rows, :] = q[:, :D_B].astype(BF16)
        cos2 = jnp.concatenate([cos, cos], axis=1)
        sin2 = jnp.concatenate([sin, sin], axis=1)
        n_r = B_HEADS * B_ROPE_DIM
        qr_ref[0, rows, :] = (q[:, D_B:D_B + n_r] * cos2 + q[:, D_B + n_r:] * sin2).astype(BF16)

        ckv = _rms(proj(_C_CKV, _C_KR), gckv_ref[...]).astype(BF16)
        kv = jnp.dot(ckv, wukv_ref[...], preferred_element_type=F32)
        kn_ref[0, rows, :] = kv[:, :D_B].astype(BF16)
        vb_ref[0, rows, :] = kv[:, D_B:].astype(BF16)


def _const_spec(shape):
    nd = len(shape)
    return pl.BlockSpec(shape, lambda *_: (0,) * nd, pipeline_mode=pl.Buffered(1))


def _projections(x, sh1, sc1, g1, cos_t, sin_t, win, gcq, wuq, gckv, wukv):
    bsz = x.shape[0]
    tm = TM_PROJ
    tok = lambda w: pl.BlockSpec((1, tm, w), lambda b, j: (b, j, 0))
    per_b = pl.BlockSpec((1, 1, D_MODEL), lambda b, j: (b, 0, 0))
    tab = pl.BlockSpec((tm, LANES), lambda b, j: (j, 0))
    widths = (D_A, D_A, D_A, D_B, B_HEADS * B_ROPE_DIM, D_B, LANES, D_B)
    return pl.pallas_call(
        _proj_kernel,
        grid=(bsz, SEQ // tm),
        in_specs=[tok(D_MODEL), per_b, per_b, _const_spec((1, D_MODEL)), tab, tab,
                  _const_spec(win.shape), _const_spec((1, Q_LORA)), _const_spec(wuq.shape),
                  _const_spec((1, KV_LORA)), _const_spec(wukv.shape)],
        out_specs=[tok(w) for w in widths],
        out_shape=[jax.ShapeDtypeStruct((bsz, SEQ, w), BF16) for w in widths],
        compiler_params=pltpu.CompilerParams(
            dimension_semantics=("parallel", "parallel"), vmem_limit_bytes=VMEM_LIMIT),
        name="in_proj",
    )(x, sh1, sc1, g1, cos_t, sin_t, win, gcq, wuq, gckv, wukv)


def _stack_rows(ref, c, row):
    idx = (c,) if c is not None else ()
    return jnp.concatenate([ref[idx + (0, pl.ds(row, BLK), slice(None))],
                            ref[idx + (1, pl.ds(row, BLK), slice(None))]], axis=0)


def _softmax_update(s, vb, state):
    nk = s.shape[1]
    m_cur = jnp.max(s, axis=1, keepdims=True)
    if state is None:
        m_new = jnp.broadcast_to(m_cur, (2 * BLK, LANES))
    else:
        m0, l0, acc0 = state
        m_new = jnp.maximum(m0, m_cur)
    m_wide = m_new if nk == LANES else jnp.concatenate([m_new] * (nk // LANES), axis=1)
    p = jnp.exp2(s - m_wide)
    v_ones = jnp.concatenate([vb, jnp.ones((nk, LANES), BF16)], axis=1)
    pv = jnp.dot(p.astype(BF16), v_ones, preferred_element_type=F32)
    acc, l_new = pv[:, :LANES], pv[:, LANES:]
    if state is not None:
        alpha = jnp.exp2(m0 - m_new)
        l_new = alpha * l0 + l_new
        acc = alpha * acc0 + acc
    return m_new, l_new, acc


def _dil_kernel(q_ref, k_ref, v_ref, tab_ref, o_ref,
                fa, fb, qs1, qs4, qs16, k4, k16, v4, v16, st4, st1):
    lane = lax.broadcasted_iota(jnp.int32, (BLK, LANES), 1)
    head_a = lane < A_HEAD_DIM
    n4 = SEQ // 4

    def split_heads(dst, row, x):
        zero = jnp.zeros_like(x)
        dst[0, pl.ds(row, BLK), :] = jnp.where(head_a, x, zero)
        dst[1, pl.ds(row, BLK), :] = jnp.where(head_a, zero, x)

    def to_residue_orders(src_ref):
        fa[...] = src_ref[0].astype(F32)
        for r in range(4):
            fb[r * n4:(r + 1) * n4, :] = fa[pl.ds(r, n4, stride=4), :]

    def residue16_block(j):
        r4, q4 = j % 4, j // 4
        return fb[pl.ds(r4 * n4 + q4, BLK, stride=4), :]

    to_residue_orders(q_ref)
    for j in range(SEQ // BLK):
        split_heads(qs1, j * BLK, q_ref[0, j * BLK:(j + 1) * BLK, :])
        split_heads(qs4, j * BLK, fb[j * BLK:(j + 1) * BLK, :].astype(BF16))
        split_heads(qs16, j * BLK, residue16_block(j).astype(BF16))
    to_residue_orders(k_ref)
    k4[...] = fb[...].astype(BF16)
    for j in range(SEQ // BLK):
        k16[j * BLK:(j + 1) * BLK, :] = residue16_block(j).astype(BF16)
    to_residue_orders(v_ref)
    v4[...] = fb[...].astype(BF16)
    for j in range(SEQ // BLK):
        v16[j * BLK:(j + 1) * BLK, :] = residue16_block(j).astype(BF16)

    def store_state(dst, start, stride, state):
        for c, x in enumerate(state):
            rows = pl.ds(start, BLK, stride=stride) if stride > 1 else pl.ds(start, BLK)
            dst[c, 0, rows, :] = x[:BLK]
            dst[c, 1, rows, :] = x[BLK:]

    def load_state(src, row):
        return tuple(_stack_rows(src, c, row) for c in range(3))

    def finish(row, state):
        _, l, acc = state
        o = acc / l
        o_ref[0, row:row + BLK, :] = jnp.where(head_a, o[:BLK], o[BLK:]).astype(BF16)

    blocks = []
    for j in range(SEQ // BLK):
        sink = functools.partial(store_state, st4, (j % 4) * n4 + j // 4, 4)
        blocks.append((2, qs16, k16, v16, j * BLK, True, None, sink))
    for n in range(n4 // BLK):
        for r in range(4):
            sink = functools.partial(store_state, st1, r + 4 * BLK * n, 4)
            blocks.append((1, qs4, k4, v4, r * n4 + n * BLK, n == 0, st4, sink))
    for n in range(SEQ // BLK):
        blocks.append((0, qs1, k_ref.at[0], v_ref.at[0], n * BLK, n == 0, st1,
                       functools.partial(finish, n * BLK)))

    def key_rows(row, first):
        return slice(row, row + BLK) if first else slice(row - BLK, row + BLK)

    def logits(blk):
        br, q_src, k_src, _, row, first, _, _ = blk
        tab = tab_ref[0, br, :, BLK:] if first else tab_ref[0, br]
        return _nt_dot(_stack_rows(q_src, None, row), k_src[key_rows(row, first), :]) + tab

    def update(blk, s):
        _, _, _, v_src, row, first, st_src, sink = blk
        state = None if st_src is None else load_state(st_src, row)
        sink(_softmax_update(s, v_src[key_rows(row, first), :], state))

    pending = []
    for blk in blocks:
        pending.append((blk, logits(blk)))
        if len(pending) > LOOKAHEAD:
            update(*pending.pop(0))
    for item in pending:
        update(*item)


def _dilated_attention(qa, ka, va, tables):
    bsz = qa.shape[0]
    pair = pl.BlockSpec((1, SEQ, LANES), lambda p, b: (b, 0, p))
    seq_f32 = pltpu.VMEM((SEQ, LANES), F32)
    seq_bf16 = pltpu.VMEM((SEQ, LANES), BF16)
    split_bf16 = pltpu.VMEM((2, SEQ, LANES), BF16)
    state = pltpu.VMEM((3, 2, SEQ, LANES), F32)
    return pl.pallas_call(
        _dil_kernel,
        grid=(N_PAIRS, bsz),
        in_specs=[pair, pair, pair,
                  pl.BlockSpec((1, 3, 2 * BLK, 2 * BLK), lambda p, b: (p, 0, 0, 0))],
        out_specs=pair,
        out_shape=jax.ShapeDtypeStruct((bsz, SEQ, D_A), BF16),
        scratch_shapes=[seq_f32, seq_f32, split_bf16, split_bf16, split_bf16,
                        seq_bf16, seq_bf16, seq_bf16, seq_bf16, state, state],
        compiler_params=pltpu.CompilerParams(
            dimension_semantics=("parallel", "parallel"), vmem_limit_bytes=VMEM_LIMIT),
        name="dilated_attn",
    )(qa, ka, va, tables)


def _mla_kernel(qn_ref, qr_ref, kn_ref, kr_ref, v_ref, o_ref):
    i = pl.program_id(1)
    qb = QB_MLA
    kc = KC_MLA
    lane = lax.broadcasted_iota(jnp.int32, (qb, LANES), 1)
    head_a = lane < B_NOPE_DIM
    slot = lane // B_ROPE_DIM

    def stacked_q(p):
        qn = qn_ref[0, :, p * LANES:(p + 1) * LANES]
        qr = qr_ref[0, :, (p // 2) * LANES:(p // 2 + 1) * LANES]
        zero = jnp.zeros_like(qn)
        sa = 2 * (p % 2)
        q_a = jnp.concatenate([jnp.where(head_a, qn, zero), jnp.where(slot == sa, qr, zero)], axis=1)
        q_b = jnp.concatenate([jnp.where(head_a, zero, qn), jnp.where(slot == sa + 1, qr, zero)], axis=1)
        return jnp.concatenate([q_a, q_b], axis=0)

    qst = [stacked_q(p) for p in range(N_PAIRS)]

    def logits(p, c, masked):
        cols = slice(p * LANES, (p + 1) * LANES)
        kcat = jnp.concatenate([kn_ref[0, c * kc:(c + 1) * kc, cols], kr_ref[0, c * kc:(c + 1) * kc, :]],
                               axis=1)
        s = _nt_dot(qst[p], kcat)
        if masked:
            qpos = lax.broadcasted_iota(jnp.int32, (2 * qb, kc), 0) % qb + (i * qb - c * kc)
            kpos = lax.broadcasted_iota(jnp.int32, (2 * qb, kc), 1)
            s = jnp.where(qpos >= kpos, s, NEG)
        return s

    def update(p, c, s, state):
        m_cur = jnp.max(s, axis=1, keepdims=True)
        m_new = jnp.broadcast_to(m_cur, (2 * qb, LANES)) if state is None else jnp.maximum(state[0], m_cur)
        pr = jnp.exp2(s - jnp.concatenate([m_new] * (kc // LANES), axis=1))
        v_ones = jnp.concatenate([v_ref[0, c * kc:(c + 1) * kc, p * LANES:(p + 1) * LANES],
                                  jnp.ones((kc, LANES), BF16)], axis=1)
        pv = jnp.dot(pr.astype(BF16), v_ones, preferred_element_type=F32)
        acc, l_new = pv[:, :LANES], pv[:, LANES:]
        if state is None:
            return m_new, l_new, acc
        alpha = jnp.exp2(state[0] - m_new)
        return m_new, alpha * state[1] + l_new, alpha * state[2] + acc

    def run(n_full):
        states = [None] * N_PAIRS
        steps = [(p, c) for c in range(n_full + 1) for p in range(N_PAIRS)]
        pending = []
        for p, c in steps:
            pending.append((p, c, logits(p, c, c == n_full)))
            if len(pending) > LOOKAHEAD:
                p0, c0, s0 = pending.pop(0)
                states[p0] = update(p0, c0, s0, states[p0])
        for p0, c0, s0 in pending:
            states[p0] = update(p0, c0, s0, states[p0])
        for p in range(N_PAIRS):
            _, l, acc = states[p]
            o = acc / l
            o_ref[0, :, p * LANES:(p + 1) * LANES] = jnp.where(head_a, o[:qb], o[qb:]).astype(BF16)

    for n_full in range(SEQ // kc):
        pl.when((i * qb) // kc == n_full)(functools.partial(run, n_full))


def _latent_attention(qn, qr, kn, kr, vb):
    bsz = qn.shape[0]
    qb = QB_MLA
    blk = lambda rows, w: pl.BlockSpec((1, rows, w), lambda b, i: (b, i if rows == qb else 0, 0))
    return pl.pallas_call(
        _mla_kernel,
        grid=(bsz, SEQ // qb),
        in_specs=[blk(qb, D_B), blk(qb, B_HEADS * B_ROPE_DIM), blk(SEQ, D_B), blk(SEQ, LANES),
                  blk(SEQ, D_B)],
        out_specs=blk(qb, D_B),
        out_shape=jax.ShapeDtypeStruct((bsz, SEQ, D_B), BF16),
        compiler_params=pltpu.CompilerParams(
            dimension_semantics=("parallel", "parallel"), vmem_limit_bytes=VMEM_LIMIT),
        name="latent_attn",
    )(qn, qr, kn, kr, vb)


def _post_kernel(x_ref, oa_ref, ob_ref, g1_ref, sh2_ref, sc2_ref, g2_ref, goa_ref, gob_ref,
                 wout_ref, gn2_ref, wfi_ref, wfo_ref, gfin_ref, o_ref, a_scr):
    ya = _rms(oa_ref[0].astype(F32), goa_ref[...])
    yb = _rms(ob_ref[0].astype(F32), gob_ref[...])
    y = jnp.concatenate([ya, yb], axis=1).astype(BF16)
    x1 = x_ref[0] + g1_ref[0] * jnp.dot(y, wout_ref[...], preferred_element_type=F32)
    h2 = (_rms(x1, gn2_ref[...]) * (1.0 + sc2_ref[0]) + sh2_ref[0]).astype(BF16)
    for c in range(D_FF // FF_CHUNK):
        lo = c * FF_CHUNK
        g = jnp.dot(h2, wfi_ref[:, lo:lo + FF_CHUNK], preferred_element_type=F32)
        u = jnp.dot(h2, wfi_ref[:, D_FF + lo:D_FF + lo + FF_CHUNK], preferred_element_type=F32)
        a_scr[:, lo:lo + FF_CHUNK] = (jax.nn.silu(g) * u).astype(BF16)
    x2 = x1 + g2_ref[0] * jnp.dot(a_scr[...], wfo_ref[...], preferred_element_type=F32)
    o_ref[0] = _rms(x2, gfin_ref[...])


def _post(x, oa, ob, g1m, sh2, sc2, g2m, goa, gob, wout, gn2, wfi, wfo, gfin):
    bsz = x.shape[0]
    tm = TM_POST
    tok = lambda w: pl.BlockSpec((1, tm, w), lambda b, j: (b, j, 0))
    per_b = pl.BlockSpec((1, 1, D_MODEL), lambda b, j: (b, 0, 0))
    return pl.pallas_call(
        _post_kernel,
        grid=(bsz, SEQ // tm),
        in_specs=[tok(D_MODEL), tok(D_A), tok(D_B), per_b, per_b, per_b, per_b,
                  _const_spec((1, D_A)), _const_spec((1, D_B)), _const_spec(wout.shape),
                  _const_spec((1, D_MODEL)), _const_spec(wfi.shape), _const_spec(wfo.shape),
                  _const_spec((1, D_MODEL))],
        out_specs=tok(D_MODEL),
        out_shape=jax.ShapeDtypeStruct((bsz, SEQ, D_MODEL), F32),
        scratch_shapes=[pltpu.VMEM((tm, D_FF), BF16)],
        compiler_params=pltpu.CompilerParams(
            dimension_semantics=("parallel", "parallel"), vmem_limit_bytes=VMEM_LIMIT),
        name="out_proj_ffn",
    )(x, oa, ob, g1m, sh2, sc2, g2m, goa, gob, wout, gn2, wfi, wfo, gfin)


def _rotate_half_cols(w):
    lead = w.shape[:-1]
    half = B_ROPE_DIM // 2
    w = w.reshape(lead + (-1, 2, half))
    return jnp.stack([-w[..., 1, :], w[..., 0, :]], axis=-2).reshape(lead + (-1,))


def _t5_bucket(dist):
    max_exact = N_BUCKETS // 2
    d = np.maximum(dist, 1).astype(np.float64)
    large = max_exact + (np.log(d / max_exact) / np.log(MAX_DISTANCE / max_exact)
                         * (N_BUCKETS - max_exact)).astype(np.int64)
    large = np.minimum(large, N_BUCKETS - 1)
    return np.where(dist < max_exact, dist, large).astype(np.int32)


def _bias_kernel(bucket_ref, rb_ref, o_ref):
    p = pl.program_id(0)
    for br in range(len(DILATIONS)):
        bucket = bucket_ref[br]
        for hh in range(2):
            tab = jnp.full((BLK, 2 * BLK), NEG, F32)
            for k in range(N_BUCKETS):
                tab = jnp.where(bucket == k, rb_ref[k, 2 * p + hh] * LOG2E, tab)
            o_ref[0, br, hh * BLK:(hh + 1) * BLK, :] = tab


def _bias_tables(rel_bias):
    a = np.arange(BLK)[:, None]
    bk = np.arange(2 * BLK)[None, :]
    steps = BLK + a - bk
    valid = (steps >= 0) & (steps <= BLK)
    bucket = np.stack([np.where(valid, _t5_bucket(np.clip(steps, 0, BLK) * dil), -1)
                       for dil in DILATIONS]).astype(np.int32)
    nbr = len(DILATIONS)
    return pl.pallas_call(
        _bias_kernel,
        grid=(N_PAIRS,),
        in_specs=[pl.BlockSpec((nbr, BLK, 2 * BLK), lambda p: (0, 0, 0)),
                  pl.BlockSpec(memory_space=pltpu.SMEM)],
        out_specs=pl.BlockSpec((1, nbr, 2 * BLK, 2 * BLK), lambda p: (p, 0, 0, 0)),
        out_shape=jax.ShapeDtypeStruct((N_PAIRS, nbr, 2 * BLK, 2 * BLK), F32),
        name="bias_tables",
    )(jnp.asarray(bucket), rel_bias)


def _rope_tables():
    half = B_ROPE_DIM // 2
    inv = ROPE_THETA ** (-jnp.arange(half, dtype=F32) / half)
    ang = jnp.arange(SEQ, dtype=F32)[:, None] * inv[None, :]
    reps = LANES // half
    return jnp.tile(jnp.cos(ang), (1, reps)), jnp.tile(jnp.sin(ang), (1, reps))


def kernel(x, c, w_ada, b_ada, g_norm1, w_in, g_cq, w_uq, g_ckv, w_ukv, rel_bias, g_out_a, g_out_b,
           w_out, g_norm2, w_ffn_in, w_ffn_out, g_final):
    bsz = x.shape[0]
    assert x.shape == (bsz, SEQ, D_MODEL) and w_ada.shape[0] == 1
    l = 0

    mod = _modulation(c, w_ada[l], b_ada[l]).reshape(bsz, N_MOD, 1, D_MODEL)
    sh1, sc1, g1m, sh2, sc2, g2m = (mod[:, i] for i in range(N_MOD))

    wi = w_in[l]
    kr_cols = wi[:, 3 * D_A + Q_LORA + KV_LORA:]
    reps = LANES // B_ROPE_DIM
    win = jnp.concatenate([wi[:, :D_A] * (A_HEAD_DIM ** -0.5 * LOG2E), wi[:, D_A:3 * D_A + Q_LORA + KV_LORA],
                           jnp.tile(kr_cols, (1, reps)), jnp.tile(_rotate_half_cols(kr_cols), (1, reps))],
                          axis=1).astype(BF16)
    wq = (w_uq[l] * ((B_NOPE_DIM + B_ROPE_DIM) ** -0.5 * LOG2E)).reshape(
        Q_LORA, B_HEADS, B_NOPE_DIM + B_ROPE_DIM)
    wq_rope = wq[:, :, B_NOPE_DIM:].reshape(Q_LORA, -1)
    wuq = jnp.concatenate([wq[:, :, :B_NOPE_DIM].reshape(Q_LORA, -1), wq_rope,
                           _rotate_half_cols(wq_rope)], axis=1).astype(BF16)
    wkv = w_ukv[l].reshape(KV_LORA, B_HEADS, B_NOPE_DIM + B_V_DIM)
    wukv = jnp.concatenate([wkv[:, :, :B_NOPE_DIM].reshape(KV_LORA, -1),
                            wkv[:, :, B_NOPE_DIM:].reshape(KV_LORA, -1)], axis=1).astype(BF16)
    cos_t, sin_t = _rope_tables()

    qa, ka, va, qn, qr, kn, kr, vb = _projections(
        x, sh1, sc1, g_norm1[l].reshape(1, -1), cos_t, sin_t, win, g_cq[l].reshape(1, -1), wuq,
        g_ckv[l].reshape(1, -1), wukv)

    out_a = _dilated_attention(qa, ka, va, _bias_tables(rel_bias))
    out_b = _latent_attention(qn, qr, kn, kr, vb)

    return _post(x, out_a, out_b, g1m, sh2, sc2, g2m, g_out_a[l].reshape(1, -1),
                 g_out_b[l].reshape(1, -1), w_out[l].astype(BF16), g_norm2[l].reshape(1, -1),
                 w_ffn_in[l].astype(BF16), w_ffn_out[l].astype(BF16), g_final.reshape(1, -1))
```

```python
import functools

import jax
import jax.numpy as jnp
import numpy as np
from jax import lax
from jax.experimental import pallas as pl
from jax.experimental.pallas import tpu as pltpu

F32 = jnp.float32
BF16 = jnp.bfloat16

D_MODEL = 1024
SEQ = 2048
A_HEADS = 8
A_HEAD_DIM = 64
D_A = A_HEADS * A_HEAD_DIM
DILATIONS = (1, 4, 16)
BLK = 128
B_HEADS = 8
B_NOPE_DIM = 64
B_ROPE_DIM = 32
B_V_DIM = 64
D_B = B_HEADS * B_V_DIM
Q_LORA = 384
KV_LORA = 256
ROPE_THETA = 10000.0
N_BUCKETS = 32
MAX_DISTANCE = 2048
D_FF = 2816
N_MOD = 6
EPS = 1e-6
NEG = -1e30
LOG2E = 1.4426950408889634

LANES = 128
N_PAIRS = A_HEADS // 2
VMEM_LIMIT = 56 * 1024 * 1024

_C_QA, _C_KA, _C_VA = 0, D_A, 2 * D_A
_C_CQ = 3 * D_A
_C_CKV = _C_CQ + Q_LORA
_C_KR = _C_CKV + KV_LORA
P_IN2 = _C_KR + LANES

TM_PROJ = 1024
SUB_PROJ = 512
TM_POST = 1024
SUB_POST = 512
FF_CHUNK = 256
LOOKAHEAD = 2
QB_MLA = 256
KC_MLA = 512


def _nt_dot(a, b):
    return lax.dot_general(a, b, (((1,), (1,)), ((), ())), preferred_element_type=F32)


def _aligned(i, m):
    return i if isinstance(i, int) else pl.multiple_of(i, m)


def _rms(x, g):
    return x * lax.rsqrt(jnp.mean(x * x, axis=-1, keepdims=True) + EPS) * g


def _mod_kernel(c_ref, w_ref, b_ref, o_ref):
    cond = jax.nn.silu(c_ref[...])
    o_ref[...] = jnp.dot(cond.astype(BF16), w_ref[...].astype(BF16),
                         preferred_element_type=F32) + b_ref[...]


def _modulation(c, w_ada, b_ada):
    bsz = c.shape[0]
    n = w_ada.shape[1]
    bn = 1536
    return pl.pallas_call(
        _mod_kernel,
        grid=(n // bn,),
        in_specs=[pl.BlockSpec((bsz, D_MODEL), lambda j: (0, 0)),
                  pl.BlockSpec((D_MODEL, bn), lambda j: (0, j)),
                  pl.BlockSpec((1, bn), lambda j: (0, j))],
        out_specs=pl.BlockSpec((bsz, bn), lambda j: (0, j)),
        out_shape=jax.ShapeDtypeStruct((bsz, n), F32),
        compiler_params=pltpu.CompilerParams(vmem_limit_bytes=VMEM_LIMIT),
        name="adaln_mod",
    )(c, w_ada, b_ada.reshape(1, n))


def _proj_kernel(x_ref, sh_ref, sc_ref, g1_ref, cos_ref, sin_ref, win_ref, gcq_ref, wuq_ref,
                 gckv_ref, wukv_ref,
                 qa_ref, ka_ref, va_ref, qn_ref, qr_ref, kn_ref, kr_ref, vb_ref):
    for t in range(TM_PROJ // SUB_PROJ):
        rows = slice(t * SUB_PROJ, (t + 1) * SUB_PROJ)
        h = _rms(x_ref[0, rows, :], g1_ref[...]) * (1.0 + sc_ref[0]) + sh_ref[0]
        hb = h.astype(BF16)

        def proj(lo, hi):
            return jnp.dot(hb, win_ref[:, lo:hi], preferred_element_type=F32)

        qa_ref[0, rows, :] = proj(_C_QA, _C_KA).astype(BF16)
        ka_ref[0, rows, :] = proj(_C_KA, _C_VA).astype(BF16)
        va_ref[0, rows, :] = proj(_C_VA, _C_CQ).astype(BF16)

        cos = cos_ref[rows, :]
        sin = sin_ref[rows, :]
        kr = proj(_C_KR, P_IN2)
        nr = B_ROPE_DIM
        kr = kr[:, :nr] * cos[:, :nr] + kr[:, nr:2 * nr] * sin[:, :nr]
        kr_ref[0, rows, :] = jnp.concatenate([kr] * (LANES // nr), axis=1).astype(BF16)

        cq = _rms(proj(_C_CQ, _C_CKV), gcq_ref[...]).astype(BF16)
        q = jnp.dot(cq, wuq_ref[...], preferred_element_type=F32)
        qn_ref[0, rows, :] = q[:, :D_B].astype(BF16)
        cos2 = jnp.concatenate([cos, cos], axis=1)
        sin2 = jnp.concatenate([sin, sin], axis=1)
        n_r = B_HEADS * B_ROPE_DIM
        qr_ref[0, rows, :] = (q[:, D_B:D_B + n_r] * cos2 + q[:, D_B + n_r:] * sin2).astype(BF16)

        ckv = _rms(proj(_C_CKV, _C_KR), gckv_ref[...]).astype(BF16)
        kv = jnp.dot(ckv, wukv_ref[...], preferred_element_type=F32)
        kn_ref[0, rows, :] = kv[:, :D_B].astype(BF16)
        vb_ref[0, rows, :] = kv[:, D_B:].astype(BF16)


def _const_spec(shape):
    nd = len(shape)
    return pl.BlockSpec(shape, lambda *_: (0,) * nd, pipeline_mode=pl.Buffered(1))


def _projections(x, sh1, sc1, g1, cos_t, sin_t, win, gcq, wuq, gckv, wukv):
    bsz = x.shape[0]
    tm = TM_PROJ
    tok = lambda w: pl.BlockSpec((1, tm, w), lambda b, j: (b, j, 0))
    per_b = pl.BlockSpec((1, 1, D_MODEL), lambda b, j: (b, 0, 0))
    tab = pl.BlockSpec((tm, LANES), lambda b, j: (j, 0))
    widths = (D_A, D_A, D_A, D_B, B_HEADS * B_ROPE_DIM, D_B, LANES, D_B)
    return pl.pallas_call(
        _proj_kernel,
        grid=(bsz, SEQ // tm),
        in_specs=[tok(D_MODEL), per_b, per_b, _const_spec((1, D_MODEL)), tab, tab,
                  _const_spec(win.shape), _const_spec((1, Q_LORA)), _const_spec(wuq.shape),
                  _const_spec((1, KV_LORA)), _const_spec(wukv.shape)],
        out_specs=[tok(w) for w in widths],
        out_shape=[jax.ShapeDtypeStruct((bsz, SEQ, w), BF16) for w in widths],
        compiler_params=pltpu.CompilerParams(
            dimension_semantics=("parallel", "parallel"), vmem_limit_bytes=VMEM_LIMIT),
        name="in_proj",
    )(x, sh1, sc1, g1, cos_t, sin_t, win, gcq, wuq, gckv, wukv)


def _stack_rows(ref, c, row):
    idx = (c,) if c is not None else ()
    return jnp.concatenate([ref[idx + (0, pl.ds(row, BLK), slice(None))],
                            ref[idx + (1, pl.ds(row, BLK), slice(None))]], axis=0)


def _softmax_update(s, vb, state):
    nk = s.shape[1]
    m_cur = jnp.max(s, axis=1, keepdims=True)
    if state is None:
        m_new = jnp.broadcast_to(m_cur, (2 * BLK, LANES))
    else:
        m0, l0, acc0 = state
        m_new = jnp.maximum(m0, m_cur)
    m_wide = m_new if nk == LANES else jnp.concatenate([m_new] * (nk // LANES), axis=1)
    p = jnp.exp2(s - m_wide)
    v_ones = jnp.concatenate([vb, jnp.ones((nk, LANES), BF16)], axis=1)
    pv = jnp.dot(p.astype(BF16), v_ones, preferred_element_type=F32)
    acc, l_new = pv[:, :LANES], pv[:, LANES:]
    if state is not None:
        alpha = jnp.exp2(m0 - m_new)
        l_new = alpha * l0 + l_new
        acc = alpha * acc0 + acc
    return m_new, l_new, acc


def _dil_kernel(q_ref, k_ref, v_ref, tab_ref, o_ref,
                fa, fb, qs1, qs4, qs16, k4, k16, v4, v16, st4, st1):
    lane = lax.broadcasted_iota(jnp.int32, (BLK, LANES), 1)
    head_a = lane < A_HEAD_DIM
    n4 = SEQ // 4

    def split_heads(dst, row, x):
        zero = jnp.zeros_like(x)
        dst[0, pl.ds(row, BLK), :] = jnp.where(head_a, x, zero)
        dst[1, pl.ds(row, BLK), :] = jnp.where(head_a, zero, x)

    def to_residue_orders(src_ref):
        fa[...] = src_ref[0].astype(F32)
        for r in range(4):
            fb[r * n4:(r + 1) * n4, :] = fa[pl.ds(r, n4, stride=4), :]

    def residue16_block(j):
        r4, q4 = j % 4, j // 4
        return fb[pl.ds(r4 * n4 + q4, BLK, stride=4), :]

    to_residue_orders(q_ref)
    for j in range(SEQ // BLK):
        split_heads(qs1, j * BLK, q_ref[0, j * BLK:(j + 1) * BLK, :])
        split_heads(qs4, j * BLK, fb[j * BLK:(j + 1) * BLK, :].astype(BF16))
        split_heads(qs16, j * BLK, residue16_block(j).astype(BF16))
    to_residue_orders(k_ref)
    k4[...] = fb[...].astype(BF16)
    for j in range(SEQ // BLK):
        k16[j * BLK:(j + 1) * BLK, :] = residue16_block(j).astype(BF16)
    to_residue_orders(v_ref)
    v4[...] = fb[...].astype(BF16)
    for j in range(SEQ // BLK):
        v16[j * BLK:(j + 1) * BLK, :] = residue16_block(j).astype(BF16)

    def store_state(dst, start, stride, state):
        for c, x in enumerate(state):
            rows = pl.ds(start, BLK, stride=stride) if stride > 1 else pl.ds(start, BLK)
            dst[c, 0, rows, :] = x[:BLK]
            dst[c, 1, rows, :] = x[BLK:]

    def load_state(src, row):
        return tuple(_stack_rows(src, c, row) for c in range(3))

    def finish(row, state):
        _, l, acc = state
        o = acc / l
        o_ref[0, row:row + BLK, :] = jnp.where(head_a, o[:BLK], o[BLK:]).astype(BF16)

    blocks = []
    for j in range(SEQ // BLK):
        sink = functools.partial(store_state, st4, (j % 4) * n4 + j // 4, 4)
        blocks.append((2, qs16, k16, v16, j * BLK, True, None, sink))
    for n in range(n4 // BLK):
        for r in range(4):
            sink = functools.partial(store_state, st1, r + 4 * BLK * n, 4)
            blocks.append((1, qs4, k4, v4, r * n4 + n * BLK, n == 0, st4, sink))
    for n in range(SEQ // BLK):
        blocks.append((0, qs1, k_ref.at[0], v_ref.at[0], n * BLK, n == 0, st1,
                       functools.partial(finish, n * BLK)))

    def key_rows(row, first):
        return slice(row, row + BLK) if first else slice(row - BLK, row + BLK)

    def logits(blk):
        br, q_src, k_src, _, row, first, _, _ = blk
        tab = tab_ref[0, br, :, BLK:] if first else tab_ref[0, br]
        return _nt_dot(_stack_rows(q_src, None, row), k_src[key_rows(row, first), :]) + tab

    def update(blk, s):
        _, _, _, v_src, row, first, st_src, sink = blk
        state = None if st_src is None else load_state(st_src, row)
        sink(_softmax_update(s, v_src[key_rows(row, first), :], state))

    pending = []
    for blk in blocks:
        pending.append((blk, logits(blk)))
        if len(pending) > LOOKAHEAD:
            update(*pending.pop(0))
    for item in pending:
        update(*item)


def _dilated_attention(qa, ka, va, tables):
    bsz = qa.shape[0]
    pair = pl.BlockSpec((1, SEQ, LANES), lambda p, b: (b, 0, p))
    seq_f32 = pltpu.VMEM((SEQ, LANES), F32)
    seq_bf16 = pltpu.VMEM((SEQ, LANES), BF16)
    split_bf16 = pltpu.VMEM((2, SEQ, LANES), BF16)
    state = pltpu.VMEM((3, 2, SEQ, LANES), F32)
    return pl.pallas_call(
        _dil_kernel,
        grid=(N_PAIRS, bsz),
        in_specs=[pair, pair, pair,
                  pl.BlockSpec((1, 3, 2 * BLK, 2 * BLK), lambda p, b: (p, 0, 0, 0))],
        out_specs=pair,
        out_shape=jax.ShapeDtypeStruct((bsz, SEQ, D_A), BF16),
        scratch_shapes=[seq_f32, seq_f32, split_bf16, split_bf16, split_bf16,
                        seq_bf16, seq_bf16, seq_bf16, seq_bf16, state, state],
        compiler_params=pltpu.CompilerParams(
            dimension_semantics=("parallel", "parallel"), vmem_limit_bytes=VMEM_LIMIT),
        name="dilated_attn",
    )(qa, ka, va, tables)


def _mla_kernel(qn_ref, qr_ref, kn_ref, kr_ref, v_ref, o_ref):
    i = pl.program_id(1)
    qb = QB_MLA
    kc = KC_MLA
    lane = lax.broadcasted_iota(jnp.int32, (qb, LANES), 1)
    head_a = lane < B_NOPE_DIM
    slot = lane // B_ROPE_DIM

    def stacked_q(p):
        qn = qn_ref[0, :, p * LANES:(p + 1) * LANES]
        qr = qr_ref[0, :, (p // 2) * LANES:(p // 2 + 1) * LANES]
        zero = jnp.zeros_like(qn)
        sa = 2 * (p % 2)
        q_a = jnp.concatenate([jnp.where(head_a, qn, zero), jnp.where(slot == sa, qr, zero)], axis=1)
        q_b = jnp.concatenate([jnp.where(head_a, zero, qn), jnp.where(slot == sa + 1, qr, zero)], axis=1)
        return jnp.concatenate([q_a, q_b], axis=0)

    qst = [stacked_q(p) for p in range(N_PAIRS)]

    def logits(p, keys, diagonal):
        cols = slice(p * LANES, (p + 1) * LANES)
        s = _nt_dot(qst[p], jnp.concatenate([kn_ref[0, keys, cols], kr_ref[0, keys, :]], axis=1))
        if diagonal:
            qpos = lax.broadcasted_iota(jnp.int32, (2 * qb, qb), 0) % qb
            kpos = lax.broadcasted_iota(jnp.int32, (2 * qb, qb), 1)
            tail = jnp.where(qpos >= kpos, s[:, -qb:], NEG)
            s = tail if s.shape[1] == qb else jnp.concatenate([s[:, :-qb], tail], axis=1)
        return s

    def update(p, keys, s, state):
        nk = s.shape[1]
        m_cur = jnp.max(s, axis=1, keepdims=True)
        m_new = jnp.broadcast_to(m_cur, (2 * qb, LANES)) if state is None else jnp.maximum(state[0], m_cur)
        pr = jnp.exp2(s - jnp.concatenate([m_new] * (nk // LANES), axis=1))
        v_ones = jnp.concatenate([v_ref[0, keys, p * LANES:(p + 1) * LANES], jnp.ones((nk, LANES), BF16)],
                                 axis=1)
        pv = jnp.dot(pr.astype(BF16), v_ones, preferred_element_type=F32)
        acc, l_new = pv[:, :LANES], pv[:, LANES:]
        if state is None:
            return m_new, l_new, acc
        alpha = jnp.exp2(state[0] - m_new)
        return m_new, alpha * state[1] + l_new, alpha * state[2] + acc

    def run(qi):
        end = (qi + 1) * qb
        chunks = [slice(lo, min(lo + kc, end)) for lo in range(0, end, kc)]
        steps = [(p, keys) for keys in chunks for p in range(N_PAIRS)]
        states = [None] * N_PAIRS
        pending = []
        for p, keys in steps:
            pending.append((p, keys, logits(p, keys, keys.stop == end)))
            if len(pending) > LOOKAHEAD:
                p0, k0, s0 = pending.pop(0)
                states[p0] = update(p0, k0, s0, states[p0])
        for p0, k0, s0 in pending:
            states[p0] = update(p0, k0, s0, states[p0])
        for p in range(N_PAIRS):
            _, l, acc = states[p]
            o = acc / l
            o_ref[0, :, p * LANES:(p + 1) * LANES] = jnp.where(head_a, o[:qb], o[qb:]).astype(BF16)

    for qi in range(SEQ // qb):
        pl.when(i == qi)(functools.partial(run, qi))


def _latent_attention(qn, qr, kn, kr, vb):
    bsz = qn.shape[0]
    qb = QB_MLA
    blk = lambda rows, w: pl.BlockSpec((1, rows, w), lambda b, i: (b, i if rows == qb else 0, 0))
    return pl.pallas_call(
        _mla_kernel,
        grid=(bsz, SEQ // qb),
        in_specs=[blk(qb, D_B), blk(qb, B_HEADS * B_ROPE_DIM), blk(SEQ, D_B), blk(SEQ, LANES),
                  blk(SEQ, D_B)],
        out_specs=blk(qb, D_B),
        out_shape=jax.ShapeDtypeStruct((bsz, SEQ, D_B), BF16),
        compiler_params=pltpu.CompilerParams(
            dimension_semantics=("parallel", "parallel"), vmem_limit_bytes=VMEM_LIMIT),
        name="latent_attn",
    )(qn, qr, kn, kr, vb)


def _post_kernel(x_ref, oa_ref, ob_ref, g1_ref, sh2_ref, sc2_ref, g2_ref, goa_ref, gob_ref,
                 wout_ref, gn2_ref, wfi_ref, wfo_ref, gfin_ref, o_ref, a_scr):
    chains = [slice(t * SUB_POST, (t + 1) * SUB_POST) for t in range(TM_POST // SUB_POST)]

    def mixed(rows):
        ya = _rms(oa_ref[0, rows, :].astype(F32), goa_ref[...])
        yb = _rms(ob_ref[0, rows, :].astype(F32), gob_ref[...])
        y = jnp.concatenate([ya, yb], axis=1).astype(BF16)
        return jnp.dot(y, wout_ref[...], preferred_element_type=F32)

    def ffn(rows, x1):
        h2 = (_rms(x1, gn2_ref[...]) * (1.0 + sc2_ref[0]) + sh2_ref[0]).astype(BF16)
        for c in range(D_FF // FF_CHUNK):
            lo = c * FF_CHUNK
            g = jnp.dot(h2, wfi_ref[:, lo:lo + FF_CHUNK], preferred_element_type=F32)
            u = jnp.dot(h2, wfi_ref[:, D_FF + lo:D_FF + lo + FF_CHUNK], preferred_element_type=F32)
            a_scr[rows, lo:lo + FF_CHUNK] = (jax.nn.silu(g) * u).astype(BF16)
        return jnp.dot(a_scr[rows, :], wfo_ref[...], preferred_element_type=F32)

    mix = [mixed(rows) for rows in chains]
    x1 = [x_ref[0, rows, :] + g1_ref[0] * m for rows, m in zip(chains, mix)]
    f = [ffn(rows, x) for rows, x in zip(chains, x1)]
    for rows, x, y in zip(chains, x1, f):
        o_ref[0, rows, :] = _rms(x + g2_ref[0] * y, gfin_ref[...])


def _post(x, oa, ob, g1m, sh2, sc2, g2m, goa, gob, wout, gn2, wfi, wfo, gfin):
    bsz = x.shape[0]
    tm = TM_POST
    tok = lambda w: pl.BlockSpec((1, tm, w), lambda b, j: (b, j, 0))
    per_b = pl.BlockSpec((1, 1, D_MODEL), lambda b, j: (b, 0, 0))
    return pl.pallas_call(
        _post_kernel,
        grid=(bsz, SEQ // tm),
        in_specs=[tok(D_MODEL), tok(D_A), tok(D_B), per_b, per_b, per_b, per_b,
                  _const_spec((1, D_A)), _const_spec((1, D_B)), _const_spec(wout.shape),
                  _const_spec((1, D_MODEL)), _const_spec(wfi.shape), _const_spec(wfo.shape),
                  _const_spec((1, D_MODEL))],
        out_specs=tok(D_MODEL),
        out_shape=jax.ShapeDtypeStruct((bsz, SEQ, D_MODEL), F32),
        scratch_shapes=[pltpu.VMEM((tm, D_FF), BF16)],
        compiler_params=pltpu.CompilerParams(
            dimension_semantics=("parallel", "parallel"), vmem_limit_bytes=VMEM_LIMIT),
        name="out_proj_ffn",
    )(x, oa, ob, g1m, sh2, sc2, g2m, goa, gob, wout, gn2, wfi, wfo, gfin)


def _rotate_half_cols(w):
    lead = w.shape[:-1]
    half = B_ROPE_DIM // 2
    w = w.reshape(lead + (-1, 2, half))
    return jnp.stack([-w[..., 1, :], w[..., 0, :]], axis=-2).reshape(lead + (-1,))


def _t5_bucket(dist):
    max_exact = N_BUCKETS // 2
    d = np.maximum(dist, 1).astype(np.float64)
    large = max_exact + (np.log(d / max_exact) / np.log(MAX_DISTANCE / max_exact)
                         * (N_BUCKETS - max_exact)).astype(np.int64)
    large = np.minimum(large, N_BUCKETS - 1)
    return np.where(dist < max_exact, dist, large).astype(np.int32)


def _bias_kernel(bucket_ref, rb_ref, o_ref):
    p = pl.program_id(0)
    for br in range(len(DILATIONS)):
        bucket = bucket_ref[br]
        for hh in range(2):
            tab = jnp.full((BLK, 2 * BLK), NEG, F32)
            for k in range(N_BUCKETS):
                tab = jnp.where(bucket == k, rb_ref[k, 2 * p + hh] * LOG2E, tab)
            o_ref[0, br, hh * BLK:(hh + 1) * BLK, :] = tab


def _bias_tables(rel_bias):
    a = np.arange(BLK)[:, None]
    bk = np.arange(2 * BLK)[None, :]
    steps = BLK + a - bk
    valid = (steps >= 0) & (steps <= BLK)
    bucket = np.stack([np.where(valid, _t5_bucket(np.clip(steps, 0, BLK) * dil), -1)
                       for dil in DILATIONS]).astype(np.int32)
    nbr = len(DILATIONS)
    return pl.pallas_call(
        _bias_kernel,
        grid=(N_PAIRS,),
        in_specs=[pl.BlockSpec((nbr, BLK, 2 * BLK), lambda p: (0, 0, 0)),
                  pl.BlockSpec(memory_space=pltpu.SMEM)],
        out_specs=pl.BlockSpec((1, nbr, 2 * BLK, 2 * BLK), lambda p: (p, 0, 0, 0)),
        out_shape=jax.ShapeDtypeStruct((N_PAIRS, nbr, 2 * BLK, 2 * BLK), F32),
        name="bias_tables",
    )(jnp.asarray(bucket), rel_bias)


def _rope_tables():
    half = B_ROPE_DIM // 2
    inv = ROPE_THETA ** (-jnp.arange(half, dtype=F32) / half)
    ang = jnp.arange(SEQ, dtype=F32)[:, None] * inv[None, :]
    reps = LANES // half
    return jnp.tile(jnp.cos(ang), (1, reps)), jnp.tile(jnp.sin(ang), (1, reps))


def kernel(x, c, w_ada, b_ada, g_norm1, w_in, g_cq, w_uq, g_ckv, w_ukv, rel_bias, g_out_a, g_out_b,
           w_out, g_norm2, w_ffn_in, w_ffn_out, g_final):
    bsz = x.shape[0]
    assert x.shape == (bsz, SEQ, D_MODEL) and w_ada.shape[0] == 1
    l = 0

    mod = _modulation(c, w_ada[l], b_ada[l]).reshape(bsz, N_MOD, 1, D_MODEL)
    sh1, sc1, g1m, sh2, sc2, g2m = (mod[:, i] for i in range(N_MOD))

    wi = w_in[l]
    kr_cols = wi[:, 3 * D_A + Q_LORA + KV_LORA:]
    win = jnp.concatenate([wi[:, :D_A] * (A_HEAD_DIM ** -0.5 * LOG2E), wi[:, D_A:3 * D_A + Q_LORA + KV_LORA],
                           kr_cols, _rotate_half_cols(kr_cols),
                           jnp.zeros((D_MODEL, LANES - 2 * B_ROPE_DIM), F32)], axis=1).astype(BF16)
    wq = (w_uq[l] * ((B_NOPE_DIM + B_ROPE_DIM) ** -0.5 * LOG2E)).reshape(
        Q_LORA, B_HEADS, B_NOPE_DIM + B_ROPE_DIM)
    wq_rope = wq[:, :, B_NOPE_DIM:].reshape(Q_LORA, -1)
    wuq = jnp.concatenate([wq[:, :, :B_NOPE_DIM].reshape(Q_LORA, -1), wq_rope,
                           _rotate_half_cols(wq_rope)], axis=1).astype(BF16)
    wkv = w_ukv[l].reshape(KV_LORA, B_HEADS, B_NOPE_DIM + B_V_DIM)
    wukv = jnp.concatenate([wkv[:, :, :B_NOPE_DIM].reshape(KV_LORA, -1),
                            wkv[:, :, B_NOPE_DIM:].reshape(KV_LORA, -1)], axis=1).astype(BF16)
    cos_t, sin_t = _rope_tables()

    qa, ka, va, qn, qr, kn, kr, vb = _projections(
        x, sh1, sc1, g_norm1[l].reshape(1, -1), cos_t, sin_t, win, g_cq[l].reshape(1, -1), wuq,
        g_ckv[l].reshape(1, -1), wukv)

    out_a = _dilated_attention(qa, ka, va, _bias_tables(rel_bias))
    out_b = _latent_attention(qn, qr, kn, kr, vb)

    return _post(x, out_a, out_b, g1m, sh2, sc2, g2m, g_out_a[l].reshape(1, -1),
                 g_out_b[l].reshape(1, -1), w_out[l].astype(BF16), g_norm2[l].reshape(1, -1),
                 w_ffn_in[l].astype(BF16), w_ffn_out[l].astype(BF16), g_final.reshape(1, -1))
```

```python
import functools

import jax
import jax.numpy as jnp
import numpy as np
from jax import lax
from jax.experimental import pallas as pl
from jax.experimental.pallas import tpu as pltpu

F32 = jnp.float32
BF16 = jnp.bfloat16

D_MODEL = 1024
SEQ = 2048
A_HEADS = 8
A_HEAD_DIM = 64
D_A = A_HEADS * A_HEAD_DIM
DILATIONS = (1, 4, 16)
BLK = 128
B_HEADS = 8
B_NOPE_DIM = 64
B_ROPE_DIM = 32
B_V_DIM = 64
D_B = B_HEADS * B_V_DIM
Q_LORA = 384
KV_LORA = 256
ROPE_THETA = 10000.0
N_BUCKETS = 32
MAX_DISTANCE = 2048
D_FF = 2816
N_MOD = 6
EPS = 1e-6
NEG = -1e30
LOG2E = 1.4426950408889634

LANES = 128
N_PAIRS = A_HEADS // 2
VMEM_LIMIT = 56 * 1024 * 1024

_C_QA, _C_KA, _C_VA = 0, D_A, 2 * D_A
_C_CQ = 3 * D_A
_C_CKV = _C_CQ + Q_LORA
_C_KR = _C_CKV + KV_LORA
P_IN2 = _C_KR + LANES

TM_PROJ = 1024
SUB_PROJ = 512
TM_POST = 1024
SUB_POST = 512
FF_CHUNK = 256
LOOKAHEAD = 2
QB_MLA = 256
KC_MLA = 512


def _nt_dot(a, b):
    return lax.dot_general(a, b, (((1,), (1,)), ((), ())), preferred_element_type=F32)


def _aligned(i, m):
    return i if isinstance(i, int) else pl.multiple_of(i, m)


def _rms(x, g):
    return x * lax.rsqrt(jnp.mean(x * x, axis=-1, keepdims=True) + EPS) * g


def _mod_kernel(c_ref, w_ref, b_ref, o_ref):
    cond = jax.nn.silu(c_ref[...])
    o_ref[...] = jnp.dot(cond.astype(BF16), w_ref[...].astype(BF16),
                         preferred_element_type=F32) + b_ref[...]


def _modulation(c, w_ada, b_ada):
    bsz = c.shape[0]
    n = w_ada.shape[1]
    bn = 1536
    return pl.pallas_call(
        _mod_kernel,
        grid=(n // bn,),
        in_specs=[pl.BlockSpec((bsz, D_MODEL), lambda j: (0, 0)),
                  pl.BlockSpec((D_MODEL, bn), lambda j: (0, j)),
                  pl.BlockSpec((1, bn), lambda j: (0, j))],
        out_specs=pl.BlockSpec((bsz, bn), lambda j: (0, j)),
        out_shape=jax.ShapeDtypeStruct((bsz, n), F32),
        compiler_params=pltpu.CompilerParams(vmem_limit_bytes=VMEM_LIMIT),
        name="adaln_mod",
    )(c, w_ada, b_ada.reshape(1, n))


def _proj_kernel(x_ref, sh_ref, sc_ref, g1_ref, cos_ref, sin_ref, win_ref, gcq_ref, wuq_ref,
                 gckv_ref, wukv_ref,
                 qa_ref, ka_ref, va_ref, qn_ref, qr_ref, kn_ref, kr_ref, vb_ref,
                 qa4_ref, ka4_ref, va4_ref, qa16_ref, ka16_ref, va16_ref, fa_scr, fb_scr):
    n_chains = TM_PROJ // SUB_PROJ
    chain_rows = [slice(t * SUB_PROJ, (t + 1) * SUB_PROJ) for t in range(n_chains)]

    def proj(hb, lo, hi):
        return jnp.dot(hb, win_ref[:, lo:hi], preferred_element_type=F32)

    def emit_orders(t, res, nat_ref, d4_ref, d16_ref):
        nat_ref[0, chain_rows[t], :] = res.astype(BF16)
        fa, fb = fa_scr.at[t], fb_scr.at[t]
        n4, n16 = SUB_PROJ // 4, SUB_PROJ // 16
        for g in range(D_A // LANES):
            lanes = slice(g * LANES, (g + 1) * LANES)
            fa[g] = res[:, lanes]
            for r in range(4):
                x4 = fa[g, pl.ds(r, n4, stride=4), :]
                fb[g, r * n4:(r + 1) * n4, :] = x4
                d4_ref[0, r, t * n4:(t + 1) * n4, lanes] = x4.astype(BF16)
            for j in range(16):
                x16 = fb[g, pl.ds((j % 4) * n4 + j // 4, n16, stride=4), :]
                d16_ref[0, j, t * n16:(t + 1) * n16, lanes] = x16.astype(BF16)

    def group_a(t):
        h = _rms(x_ref[0, chain_rows[t], :], g1_ref[...]) * (1.0 + sc_ref[0]) + sh_ref[0]
        hb = h.astype(BF16)
        emit_orders(t, proj(hb, _C_QA, _C_KA), qa_ref, qa4_ref, qa16_ref)
        emit_orders(t, proj(hb, _C_KA, _C_VA), ka_ref, ka4_ref, ka16_ref)
        emit_orders(t, proj(hb, _C_VA, _C_CQ), va_ref, va4_ref, va16_ref)
        return hb

    def group_b(t, hb):
        rows = chain_rows[t]
        cos = cos_ref[rows, :]
        sin = sin_ref[rows, :]
        kr = proj(hb, _C_KR, P_IN2)
        nr = B_ROPE_DIM
        kr = kr[:, :nr] * cos[:, :nr] + kr[:, nr:2 * nr] * sin[:, :nr]
        kr_ref[0, rows, :] = jnp.concatenate([kr] * (LANES // nr), axis=1).astype(BF16)

        cq = _rms(proj(hb, _C_CQ, _C_CKV), gcq_ref[...]).astype(BF16)
        q = jnp.dot(cq, wuq_ref[...], preferred_element_type=F32)
        qn_ref[0, rows, :] = q[:, :D_B].astype(BF16)
        cos2 = jnp.concatenate([cos, cos], axis=1)
        sin2 = jnp.concatenate([sin, sin], axis=1)
        n_r = B_HEADS * B_ROPE_DIM
        qr_ref[0, rows, :] = (q[:, D_B:D_B + n_r] * cos2 + q[:, D_B + n_r:] * sin2).astype(BF16)

        ckv = _rms(proj(hb, _C_CKV, _C_KR), gckv_ref[...]).astype(BF16)
        kv = jnp.dot(ckv, wukv_ref[...], preferred_element_type=F32)
        kn_ref[0, rows, :] = kv[:, :D_B].astype(BF16)
        vb_ref[0, rows, :] = kv[:, D_B:].astype(BF16)

    hbs = [group_a(t) for t in range(n_chains)]
    for t in range(n_chains):
        group_b(t, hbs[t])


def _const_spec(shape):
    nd = len(shape)
    return pl.BlockSpec(shape, lambda *_: (0,) * nd, pipeline_mode=pl.Buffered(1))


def _projections(x, sh1, sc1, g1, cos_t, sin_t, win, gcq, wuq, gckv, wukv):
    bsz = x.shape[0]
    tm = TM_PROJ
    tok = lambda w: pl.BlockSpec((1, tm, w), lambda b, j: (b, j, 0))
    per_b = pl.BlockSpec((1, 1, D_MODEL), lambda b, j: (b, 0, 0))
    tab = pl.BlockSpec((tm, LANES), lambda b, j: (j, 0))
    widths = (D_A, D_A, D_A, D_B, B_HEADS * B_ROPE_DIM, D_B, LANES, D_B)
    res_spec = lambda d: pl.BlockSpec((1, d, tm // d, D_A), lambda b, j: (b, 0, j, 0))
    res_shape = lambda d: jax.ShapeDtypeStruct((bsz, d, SEQ // d, D_A), BF16)
    chain_f32 = pltpu.VMEM((tm // SUB_PROJ, D_A // LANES, SUB_PROJ, LANES), F32)
    return pl.pallas_call(
        _proj_kernel,
        grid=(bsz, SEQ // tm),
        in_specs=[tok(D_MODEL), per_b, per_b, _const_spec((1, D_MODEL)), tab, tab,
                  _const_spec(win.shape), _const_spec((1, Q_LORA)), _const_spec(wuq.shape),
                  _const_spec((1, KV_LORA)), _const_spec(wukv.shape)],
        out_specs=[tok(w) for w in widths] + [res_spec(4)] * 3 + [res_spec(16)] * 3,
        out_shape=[jax.ShapeDtypeStruct((bsz, SEQ, w), BF16) for w in widths]
        + [res_shape(4)] * 3 + [res_shape(16)] * 3,
        scratch_shapes=[chain_f32, chain_f32],
        compiler_params=pltpu.CompilerParams(
            dimension_semantics=("parallel", "parallel"), vmem_limit_bytes=VMEM_LIMIT),
        name="in_proj",
    )(x, sh1, sc1, g1, cos_t, sin_t, win, gcq, wuq, gckv, wukv)


def _stack_rows(ref, c, row):
    return jnp.concatenate([ref[c, 0, pl.ds(row, BLK), :], ref[c, 1, pl.ds(row, BLK), :]], axis=0)


def _softmax_update(s, vb, state):
    nk = s.shape[1]
    m_cur = jnp.max(s, axis=1, keepdims=True)
    if state is None:
        m_new = jnp.broadcast_to(m_cur, (2 * BLK, LANES))
    else:
        m0, l0, acc0 = state
        m_new = jnp.maximum(m0, m_cur)
    m_wide = m_new if nk == LANES else jnp.concatenate([m_new] * (nk // LANES), axis=1)
    p = jnp.exp2(s - m_wide)
    v_ones = jnp.concatenate([vb, jnp.ones((nk, LANES), BF16)], axis=1)
    pv = jnp.dot(p.astype(BF16), v_ones, preferred_element_type=F32)
    acc, l_new = pv[:, :LANES], pv[:, LANES:]
    if state is not None:
        alpha = jnp.exp2(m0 - m_new)
        l_new = alpha * l0 + l_new
        acc = alpha * acc0 + acc
    return m_new, l_new, acc


def _dil_kernel(q1_ref, k1_ref, v1_ref, q4_ref, k4_ref, v4_ref, q16_ref, k16_ref, v16_ref, tab_ref,
                o_ref, st4, st1):
    lane = lax.broadcasted_iota(jnp.int32, (BLK, LANES), 1)
    head_a = lane < A_HEAD_DIM
    n4 = SEQ // 4

    def store_state(dst, start, stride, state):
        for c, x in enumerate(state):
            rows = pl.ds(start, BLK, stride=stride) if stride > 1 else pl.ds(start, BLK)
            dst[c, 0, rows, :] = x[:BLK]
            dst[c, 1, rows, :] = x[BLK:]

    def load_state(src, row):
        return tuple(_stack_rows(src, c, row) for c in range(3))

    def finish(row, state):
        _, l, acc = state
        o = acc / l
        o_ref[0, row:row + BLK, :] = jnp.where(head_a, o[:BLK], o[BLK:]).astype(BF16)

    def key_rows(row, first):
        return slice(row, row + BLK) if first else slice(row - BLK, row + BLK)

    blocks = []
    for j in range(SEQ // BLK):
        sink = functools.partial(store_state, st4, (j % 4) * n4 + j // 4, 4)
        blocks.append((2, q16_ref.at[0, j], k16_ref.at[0, j], v16_ref.at[0, j], 0, True, None, 0, sink))
    for n in range(n4 // BLK):
        for r in range(4):
            sink = functools.partial(store_state, st1, r + 4 * BLK * n, 4)
            blocks.append((1, q4_ref.at[0, r], k4_ref.at[0, r], v4_ref.at[0, r], n * BLK, n == 0,
                           st4, r * n4 + n * BLK, sink))
    for n in range(SEQ // BLK):
        blocks.append((0, q1_ref.at[0], k1_ref.at[0], v1_ref.at[0], n * BLK, n == 0, st1, n * BLK,
                       functools.partial(finish, n * BLK)))

    def logits(blk):
        br, q_src, k_src, _, row, first, _, _, _ = blk
        q = q_src[row:row + BLK, :]
        zero = jnp.zeros_like(q)
        qst = jnp.concatenate([jnp.where(head_a, q, zero), jnp.where(head_a, zero, q)], axis=0)
        tab = tab_ref[0, br, :, BLK:] if first else tab_ref[0, br]
        return _nt_dot(qst, k_src[key_rows(row, first), :]) + tab

    def update(blk, s):
        _, _, _, v_src, row, first, st_src, st_row, sink = blk
        state = None if st_src is None else load_state(st_src, st_row)
        sink(_softmax_update(s, v_src[key_rows(row, first), :], state))

    pending = []
    for blk in blocks:
        pending.append((blk, logits(blk)))
        if len(pending) > LOOKAHEAD:
            update(*pending.pop(0))
    for item in pending:
        update(*item)


def _dilated_attention(qkv1, qkv4, qkv16, tables):
    bsz = qkv1[0].shape[0]
    pair = pl.BlockSpec((1, SEQ, LANES), lambda p, b: (b, 0, p))
    pair_res = lambda d: pl.BlockSpec((1, d, SEQ // d, LANES), lambda p, b: (b, 0, 0, p))
    state = pltpu.VMEM((3, 2, SEQ, LANES), F32)
    return pl.pallas_call(
        _dil_kernel,
        grid=(N_PAIRS, bsz),
        in_specs=[pair] * 3 + [pair_res(4)] * 3 + [pair_res(16)] * 3 + [
            pl.BlockSpec((1, 3, 2 * BLK, 2 * BLK), lambda p, b: (p, 0, 0, 0))],
        out_specs=pair,
        out_shape=jax.ShapeDtypeStruct((bsz, SEQ, D_A), BF16),
        scratch_shapes=[state, state],
        compiler_params=pltpu.CompilerParams(
            dimension_semantics=("parallel", "parallel"), vmem_limit_bytes=VMEM_LIMIT),
        name="dilated_attn",
    )(*qkv1, *qkv4, *qkv16, tables)


def _mla_kernel(qn_ref, qr_ref, kn_ref, kr_ref, v_ref, o_ref):
    i = pl.program_id(1)
    qb = QB_MLA
    kc = KC_MLA
    lane = lax.broadcasted_iota(jnp.int32, (qb, LANES), 1)
    head_a = lane < B_NOPE_DIM
    slot = lane // B_ROPE_DIM

    def stacked_q(p):
        qn = qn_ref[0, :, p * LANES:(p + 1) * LANES]
        qr = qr_ref[0, :, (p // 2) * LANES:(p // 2 + 1) * LANES]
        zero = jnp.zeros_like(qn)
        sa = 2 * (p % 2)
        q_a = jnp.concatenate([jnp.where(head_a, qn, zero), jnp.where(slot == sa, qr, zero)], axis=1)
        q_b = jnp.concatenate([jnp.where(head_a, zero, qn), jnp.where(slot == sa + 1, qr, zero)], axis=1)
        return jnp.concatenate([q_a, q_b], axis=0)

    qst = [stacked_q(p) for p in range(N_PAIRS)]

    def logits(p, keys, diagonal):
        cols = slice(p * LANES, (p + 1) * LANES)
        s = _nt_dot(qst[p], jnp.concatenate([kn_ref[0, keys, cols], kr_ref[0, keys, :]], axis=1))
        if diagonal:
            qpos = lax.broadcasted_iota(jnp.int32, (2 * qb, qb), 0) % qb
            kpos = lax.broadcasted_iota(jnp.int32, (2 * qb, qb), 1)
            tail = jnp.where(qpos >= kpos, s[:, -qb:], NEG)
            s = tail if s.shape[1] == qb else jnp.concatenate([s[:, :-qb], tail], axis=1)
        return s

    def update(p, keys, s, state):
        nk = s.shape[1]
        m_cur = jnp.max(s, axis=1, keepdims=True)
        m_new = jnp.broadcast_to(m_cur, (2 * qb, LANES)) if state is None else jnp.maximum(state[0], m_cur)
        pr = jnp.exp2(s - jnp.concatenate([m_new] * (nk // LANES), axis=1))
        v_ones = jnp.concatenate([v_ref[0, keys, p * LANES:(p + 1) * LANES], jnp.ones((nk, LANES), BF16)],
                                 axis=1)
        pv = jnp.dot(pr.astype(BF16), v_ones, preferred_element_type=F32)
        acc, l_new = pv[:, :LANES], pv[:, LANES:]
        if state is None:
            return m_new, l_new, acc
        alpha = jnp.exp2(state[0] - m_new)
        return m_new, alpha * state[1] + l_new, alpha * state[2] + acc

    def run(qi):
        end = (qi + 1) * qb
        chunks = [slice(lo, min(lo + kc, end)) for lo in range(0, end, kc)]
        steps = [(p, keys) for keys in chunks for p in range(N_PAIRS)]
        states = [None] * N_PAIRS
        pending = []
        for p, keys in steps:
            pending.append((p, keys, logits(p, keys, keys.stop == end)))
            if len(pending) > LOOKAHEAD:
                p0, k0, s0 = pending.pop(0)
                states[p0] = update(p0, k0, s0, states[p0])
        for p0, k0, s0 in pending:
            states[p0] = update(p0, k0, s0, states[p0])
        for p in range(N_PAIRS):
            _, l, acc = states[p]
            o = acc / l
            o_ref[0, :, p * LANES:(p + 1) * LANES] = jnp.where(head_a, o[:qb], o[qb:]).astype(BF16)

    for qi in range(SEQ // qb):
        pl.when(i == qi)(functools.partial(run, qi))


def _latent_attention(qn, qr, kn, kr, vb):
    bsz = qn.shape[0]
    qb = QB_MLA
    blk = lambda rows, w: pl.BlockSpec((1, rows, w), lambda b, i: (b, i if rows == qb else 0, 0))
    return pl.pallas_call(
        _mla_kernel,
        grid=(bsz, SEQ // qb),
        in_specs=[blk(qb, D_B), blk(qb, B_HEADS * B_ROPE_DIM), blk(SEQ, D_B), blk(SEQ, LANES),
                  blk(SEQ, D_B)],
        out_specs=blk(qb, D_B),
        out_shape=jax.ShapeDtypeStruct((bsz, SEQ, D_B), BF16),
        compiler_params=pltpu.CompilerParams(
            dimension_semantics=("parallel", "parallel"), vmem_limit_bytes=VMEM_LIMIT),
        name="latent_attn",
    )(qn, qr, kn, kr, vb)


def _post_kernel(x_ref, oa_ref, ob_ref, g1_ref, sh2_ref, sc2_ref, g2_ref, goa_ref, gob_ref,
                 wout_ref, gn2_ref, wfi_ref, wfo_ref, gfin_ref, o_ref, a_scr):
    chains = [slice(t * SUB_POST, (t + 1) * SUB_POST) for t in range(TM_POST // SUB_POST)]

    def mixed(rows):
        ya = _rms(oa_ref[0, rows, :].astype(F32), goa_ref[...])
        yb = _rms(ob_ref[0, rows, :].astype(F32), gob_ref[...])
        y = jnp.concatenate([ya, yb], axis=1).astype(BF16)
        return jnp.dot(y, wout_ref[...], preferred_element_type=F32)

    def ffn(rows, x1):
        h2 = (_rms(x1, gn2_ref[...]) * (1.0 + sc2_ref[0]) + sh2_ref[0]).astype(BF16)
        for c in range(D_FF // FF_CHUNK):
            lo = c * FF_CHUNK
            g = jnp.dot(h2, wfi_ref[:, lo:lo + FF_CHUNK], preferred_element_type=F32)
            u = jnp.dot(h2, wfi_ref[:, D_FF + lo:D_FF + lo + FF_CHUNK], preferred_element_type=F32)
            a_scr[rows, lo:lo + FF_CHUNK] = (jax.nn.silu(g) * u).astype(BF16)
        return jnp.dot(a_scr[rows, :], wfo_ref[...], preferred_element_type=F32)

    mix = [mixed(rows) for rows in chains]
    x1 = [x_ref[0, rows, :] + g1_ref[0] * m for rows, m in zip(chains, mix)]
    f = [ffn(rows, x) for rows, x in zip(chains, x1)]
    for rows, x, y in zip(chains, x1, f):
        o_ref[0, rows, :] = _rms(x + g2_ref[0] * y, gfin_ref[...])


def _post(x, oa, ob, g1m, sh2, sc2, g2m, goa, gob, wout, gn2, wfi, wfo, gfin):
    bsz = x.shape[0]
    tm = TM_POST
    tok = lambda w: pl.BlockSpec((1, tm, w), lambda b, j: (b, j, 0))
    per_b = pl.BlockSpec((1, 1, D_MODEL), lambda b, j: (b, 0, 0))
    return pl.pallas_call(
        _post_kernel,
        grid=(bsz, SEQ // tm),
        in_specs=[tok(D_MODEL), tok(D_A), tok(D_B), per_b, per_b, per_b, per_b,
                  _const_spec((1, D_A)), _const_spec((1, D_B)), _const_spec(wout.shape),
                  _const_spec((1, D_MODEL)), _const_spec(wfi.shape), _const_spec(wfo.shape),
                  _const_spec((1, D_MODEL))],
        out_specs=tok(D_MODEL),
        out_shape=jax.ShapeDtypeStruct((bsz, SEQ, D_MODEL), F32),
        scratch_shapes=[pltpu.VMEM((tm, D_FF), BF16)],
        compiler_params=pltpu.CompilerParams(
            dimension_semantics=("parallel", "parallel"), vmem_limit_bytes=VMEM_LIMIT),
        name="out_proj_ffn",
    )(x, oa, ob, g1m, sh2, sc2, g2m, goa, gob, wout, gn2, wfi, wfo, gfin)


def _rotate_half_cols(w):
    lead = w.shape[:-1]
    half = B_ROPE_DIM // 2
    w = w.reshape(lead + (-1, 2, half))
    return jnp.stack([-w[..., 1, :], w[..., 0, :]], axis=-2).reshape(lead + (-1,))


def _t5_bucket(dist):
    max_exact = N_BUCKETS // 2
    d = np.maximum(dist, 1).astype(np.float64)
    large = max_exact + (np.log(d / max_exact) / np.log(MAX_DISTANCE / max_exact)
                         * (N_BUCKETS - max_exact)).astype(np.int64)
    large = np.minimum(large, N_BUCKETS - 1)
    return np.where(dist < max_exact, dist, large).astype(np.int32)


def _bias_kernel(bucket_ref, rb_ref, o_ref):
    p = pl.program_id(0)
    for br in range(len(DILATIONS)):
        bucket = bucket_ref[br]
        for hh in range(2):
            tab = jnp.full((BLK, 2 * BLK), NEG, F32)
            for k in range(N_BUCKETS):
                tab = jnp.where(bucket == k, rb_ref[k, 2 * p + hh] * LOG2E, tab)
            o_ref[0, br, hh * BLK:(hh + 1) * BLK, :] = tab


def _bias_tables(rel_bias):
    a = np.arange(BLK)[:, None]
    bk = np.arange(2 * BLK)[None, :]
    steps = BLK + a - bk
    valid = (steps >= 0) & (steps <= BLK)
    bucket = np.stack([np.where(valid, _t5_bucket(np.clip(steps, 0, BLK) * dil), -1)
                       for dil in DILATIONS]).astype(np.int32)
    nbr = len(DILATIONS)
    return pl.pallas_call(
        _bias_kernel,
        grid=(N_PAIRS,),
        in_specs=[pl.BlockSpec((nbr, BLK, 2 * BLK), lambda p: (0, 0, 0)),
                  pl.BlockSpec(memory_space=pltpu.SMEM)],
        out_specs=pl.BlockSpec((1, nbr, 2 * BLK, 2 * BLK), lambda p: (p, 0, 0, 0)),
        out_shape=jax.ShapeDtypeStruct((N_PAIRS, nbr, 2 * BLK, 2 * BLK), F32),
        name="bias_tables",
    )(jnp.asarray(bucket), rel_bias)


def _rope_tables():
    half = B_ROPE_DIM // 2
    inv = ROPE_THETA ** (-jnp.arange(half, dtype=F32) / half)
    ang = jnp.arange(SEQ, dtype=F32)[:, None] * inv[None, :]
    reps = LANES // half
    return jnp.tile(jnp.cos(ang), (1, reps)), jnp.tile(jnp.sin(ang), (1, reps))


def kernel(x, c, w_ada, b_ada, g_norm1, w_in, g_cq, w_uq, g_ckv, w_ukv, rel_bias, g_out_a, g_out_b,
           w_out, g_norm2, w_ffn_in, w_ffn_out, g_final):
    bsz = x.shape[0]
    assert x.shape == (bsz, SEQ, D_MODEL) and w_ada.shape[0] == 1
    l = 0

    mod = _modulation(c, w_ada[l], b_ada[l]).reshape(bsz, N_MOD, 1, D_MODEL)
    sh1, sc1, g1m, sh2, sc2, g2m = (mod[:, i] for i in range(N_MOD))

    wi = w_in[l]
    kr_cols = wi[:, 3 * D_A + Q_LORA + KV_LORA:]
    win = jnp.concatenate([wi[:, :D_A] * (A_HEAD_DIM ** -0.5 * LOG2E), wi[:, D_A:3 * D_A + Q_LORA + KV_LORA],
                           kr_cols, _rotate_half_cols(kr_cols),
                           jnp.zeros((D_MODEL, LANES - 2 * B_ROPE_DIM), F32)], axis=1).astype(BF16)
    wq = (w_uq[l] * ((B_NOPE_DIM + B_ROPE_DIM) ** -0.5 * LOG2E)).reshape(
        Q_LORA, B_HEADS, B_NOPE_DIM + B_ROPE_DIM)
    wq_rope = wq[:, :, B_NOPE_DIM:].reshape(Q_LORA, -1)
    wuq = jnp.concatenate([wq[:, :, :B_NOPE_DIM].reshape(Q_LORA, -1), wq_rope,
                           _rotate_half_cols(wq_rope)], axis=1).astype(BF16)
    wkv = w_ukv[l].reshape(KV_LORA, B_HEADS, B_NOPE_DIM + B_V_DIM)
    wukv = jnp.concatenate([wkv[:, :, :B_NOPE_DIM].reshape(KV_LORA, -1),
                            wkv[:, :, B_NOPE_DIM:].reshape(KV_LORA, -1)], axis=1).astype(BF16)
    cos_t, sin_t = _rope_tables()

    proj = _projections(
        x, sh1, sc1, g_norm1[l].reshape(1, -1), cos_t, sin_t, win, g_cq[l].reshape(1, -1), wuq,
        g_ckv[l].reshape(1, -1), wukv)
    qn, qr, kn, kr, vb = proj[3:8]

    out_a = _dilated_attention(proj[0:3], proj[8:11], proj[11:14], _bias_tables(rel_bias))
    out_b = _latent_attention(qn, qr, kn, kr, vb)

    return _post(x, out_a, out_b, g1m, sh2, sc2, g2m, g_out_a[l].reshape(1, -1),
                 g_out_b[l].reshape(1, -1), w_out[l].astype(BF16), g_norm2[l].reshape(1, -1),
                 w_ffn_in[l].astype(BF16), w_ffn_out[l].astype(BF16), g_final.reshape(1, -1))
```

```python
import functools

import jax
import jax.numpy as jnp
import numpy as np
from jax import lax
from jax.experimental import pallas as pl
from jax.experimental.pallas import tpu as pltpu

F32 = jnp.float32
BF16 = jnp.bfloat16

D_MODEL = 1024
SEQ = 2048
A_HEADS = 8
A_HEAD_DIM = 64
D_A = A_HEADS * A_HEAD_DIM
DILATIONS = (1, 4, 16)
BLK = 128
B_HEADS = 8
B_NOPE_DIM = 64
B_ROPE_DIM = 32
B_V_DIM = 64
D_B = B_HEADS * B_V_DIM
Q_LORA = 384
KV_LORA = 256
ROPE_THETA = 10000.0
N_BUCKETS = 32
MAX_DISTANCE = 2048
D_FF = 2816
N_MOD = 6
EPS = 1e-6
NEG = -1e30
LOG2E = 1.4426950408889634

LANES = 128
N_PAIRS = A_HEADS // 2
VMEM_LIMIT = 56 * 1024 * 1024

_C_QA, _C_KA, _C_VA = 0, D_A, 2 * D_A
_C_CQ = 3 * D_A
_C_CKV = _C_CQ + Q_LORA
_C_KR = _C_CKV + KV_LORA
P_IN2 = _C_KR + LANES

TM_PROJ = 1024
SUB_PROJ = 512
TM_POST = 1024
SUB_POST = 512
FF_CHUNK = 256
LOOKAHEAD = 2
LOOKAHEAD_MLA = 1
QB_MLA = 256
KC_MLA = 512


def _nt_dot(a, b):
    return lax.dot_general(a, b, (((1,), (1,)), ((), ())), preferred_element_type=F32)


def _rms(x, g):
    return x * lax.rsqrt(jnp.mean(x * x, axis=-1, keepdims=True) + EPS) * g


def _mod_kernel(c_ref, w_ref, b_ref, o_ref):
    cond = jax.nn.silu(c_ref[...])
    o_ref[...] = jnp.dot(cond.astype(BF16), w_ref[...].astype(BF16),
                         preferred_element_type=F32) + b_ref[...]


def _modulation(c, w_ada, b_ada):
    bsz = c.shape[0]
    n = w_ada.shape[1]
    bn = 1536
    return pl.pallas_call(
        _mod_kernel,
        grid=(n // bn,),
        in_specs=[pl.BlockSpec((bsz, D_MODEL), lambda j: (0, 0)),
                  pl.BlockSpec((D_MODEL, bn), lambda j: (0, j)),
                  pl.BlockSpec((1, bn), lambda j: (0, j))],
        out_specs=pl.BlockSpec((bsz, bn), lambda j: (0, j)),
        out_shape=jax.ShapeDtypeStruct((bsz, n), F32),
        compiler_params=pltpu.CompilerParams(vmem_limit_bytes=VMEM_LIMIT),
        name="adaln_mod",
    )(c, w_ada, b_ada.reshape(1, n))


def _proj_kernel(x_ref, sh_ref, sc_ref, g1_ref, cos_ref, sin_ref, win_ref, gcq_ref, wuq_ref,
                 gckv_ref, wukv_ref,
                 qa_ref, ka_ref, va_ref, qn_ref, qr_ref, kn_ref, kr_ref, vb_ref,
                 qa4_ref, ka4_ref, va4_ref, qa16_ref, ka16_ref, va16_ref, fa_scr, fb_scr):
    for t in range(TM_PROJ // SUB_PROJ):
        rows = slice(t * SUB_PROJ, (t + 1) * SUB_PROJ)
        h = _rms(x_ref[0, rows, :], g1_ref[...]) * (1.0 + sc_ref[0]) + sh_ref[0]
        hb = h.astype(BF16)

        def proj(lo, hi):
            return jnp.dot(hb, win_ref[:, lo:hi], preferred_element_type=F32)

        def emit_orders(res, nat_ref, d4_ref, d16_ref):
            nat_ref[0, rows, :] = res.astype(BF16)
            fa, fb = fa_scr.at[t], fb_scr.at[t]
            n4, n16 = SUB_PROJ // 4, SUB_PROJ // 16
            for g in range(D_A // LANES):
                lanes = slice(g * LANES, (g + 1) * LANES)
                fa[g] = res[:, lanes]
                for r in range(4):
                    x4 = fa[g, pl.ds(r, n4, stride=4), :]
                    fb[g, r * n4:(r + 1) * n4, :] = x4
                    d4_ref[0, r, t * n4:(t + 1) * n4, lanes] = x4.astype(BF16)
                for j in range(16):
                    x16 = fb[g, pl.ds((j % 4) * n4 + j // 4, n16, stride=4), :]
                    d16_ref[0, j, t * n16:(t + 1) * n16, lanes] = x16.astype(BF16)

        emit_orders(proj(_C_QA, _C_KA), qa_ref, qa4_ref, qa16_ref)
        emit_orders(proj(_C_KA, _C_VA), ka_ref, ka4_ref, ka16_ref)
        emit_orders(proj(_C_VA, _C_CQ), va_ref, va4_ref, va16_ref)

        cos = cos_ref[rows, :]
        sin = sin_ref[rows, :]
        kr = proj(_C_KR, P_IN2)
        nr = B_ROPE_DIM
        kr = kr[:, :nr] * cos[:, :nr] + kr[:, nr:2 * nr] * sin[:, :nr]
        kr_ref[0, rows, :] = jnp.concatenate([kr] * (LANES // nr), axis=1).astype(BF16)

        cq = _rms(proj(_C_CQ, _C_CKV), gcq_ref[...]).astype(BF16)
        q = jnp.dot(cq, wuq_ref[...], preferred_element_type=F32)
        qn_ref[0, rows, :] = q[:, :D_B].astype(BF16)
        cos2 = jnp.concatenate([cos, cos], axis=1)
        sin2 = jnp.concatenate([sin, sin], axis=1)
        n_r = B_HEADS * B_ROPE_DIM
        qr_ref[0, rows, :] = (q[:, D_B:D_B + n_r] * cos2 + q[:, D_B + n_r:] * sin2).astype(BF16)

        ckv = _rms(proj(_C_CKV, _C_KR), gckv_ref[...]).astype(BF16)
        kv = jnp.dot(ckv, wukv_ref[...], preferred_element_type=F32)
        kn_ref[0, rows, :] = kv[:, :D_B].astype(BF16)
        vb_ref[0, rows, :] = kv[:, D_B:].astype(BF16)


def _const_spec(shape):
    nd = len(shape)
    return pl.BlockSpec(shape, lambda *_: (0,) * nd, pipeline_mode=pl.Buffered(1))


def _projections(x, sh1, sc1, g1, cos_t, sin_t, win, gcq, wuq, gckv, wukv):
    bsz = x.shape[0]
    tm = TM_PROJ
    tok = lambda w: pl.BlockSpec((1, tm, w), lambda b, j: (b, j, 0))
    per_b = pl.BlockSpec((1, 1, D_MODEL), lambda b, j: (b, 0, 0))
    tab = pl.BlockSpec((tm, LANES), lambda b, j: (j, 0))
    widths = (D_A, D_A, D_A, D_B, B_HEADS * B_ROPE_DIM, D_B, LANES, D_B)
    res_spec = lambda d: pl.BlockSpec((1, d, tm // d, D_A), lambda b, j: (b, 0, j, 0))
    res_shape = lambda d: jax.ShapeDtypeStruct((bsz, d, SEQ // d, D_A), BF16)
    chain_f32 = pltpu.VMEM((tm // SUB_PROJ, D_A // LANES, SUB_PROJ, LANES), F32)
    return pl.pallas_call(
        _proj_kernel,
        grid=(bsz, SEQ // tm),
        in_specs=[tok(D_MODEL), per_b, per_b, _const_spec((1, D_MODEL)), tab, tab,
                  _const_spec(win.shape), _const_spec((1, Q_LORA)), _const_spec(wuq.shape),
                  _const_spec((1, KV_LORA)), _const_spec(wukv.shape)],
        out_specs=[tok(w) for w in widths] + [res_spec(4)] * 3 + [res_spec(16)] * 3,
        out_shape=[jax.ShapeDtypeStruct((bsz, SEQ, w), BF16) for w in widths]
        + [res_shape(4)] * 3 + [res_shape(16)] * 3,
        scratch_shapes=[chain_f32, chain_f32],
        compiler_params=pltpu.CompilerParams(
            dimension_semantics=("parallel", "parallel"), vmem_limit_bytes=VMEM_LIMIT),
        name="in_proj",
    )(x, sh1, sc1, g1, cos_t, sin_t, win, gcq, wuq, gckv, wukv)


def _stack_rows(ref, c, row):
    return jnp.concatenate([ref[c, 0, pl.ds(row, BLK), :], ref[c, 1, pl.ds(row, BLK), :]], axis=0)


def _softmax_update(s, vb, state):
    nk = s.shape[1]
    m_cur = jnp.max(s, axis=1, keepdims=True)
    if state is None:
        m_new = jnp.broadcast_to(m_cur, (2 * BLK, LANES))
    else:
        m0, l0, acc0 = state
        m_new = jnp.maximum(m0, m_cur)
    m_wide = m_new if nk == LANES else jnp.concatenate([m_new] * (nk // LANES), axis=1)
    p = jnp.exp2(s - m_wide)
    v_ones = jnp.concatenate([vb, jnp.ones((nk, LANES), BF16)], axis=1)
    pv = jnp.dot(p.astype(BF16), v_ones, preferred_element_type=F32)
    acc, l_new = pv[:, :LANES], pv[:, LANES:]
    if state is not None:
        alpha = jnp.exp2(m0 - m_new)
        l_new = alpha * l0 + l_new
        acc = alpha * acc0 + acc
    return m_new, l_new, acc


def _dil_kernel(q1_ref, k1_ref, v1_ref, q4_ref, k4_ref, v4_ref, q16_ref, k16_ref, v16_ref, tab_ref,
                o_ref, st4, st1):
    lane = lax.broadcasted_iota(jnp.int32, (BLK, LANES), 1)
    head_a = lane < A_HEAD_DIM
    n4 = SEQ // 4

    def store_state(dst, start, stride, state):
        for c, x in enumerate(state):
            rows = pl.ds(start, BLK, stride=stride) if stride > 1 else pl.ds(start, BLK)
            dst[c, 0, rows, :] = x[:BLK]
            dst[c, 1, rows, :] = x[BLK:]

    def load_state(src, row):
        return tuple(_stack_rows(src, c, row) for c in range(3))

    def finish(row, state):
        _, l, acc = state
        o = acc / l
        o_ref[0, row:row + BLK, :] = jnp.where(head_a, o[:BLK], o[BLK:]).astype(BF16)

    def key_rows(row, first):
        return slice(row, row + BLK) if first else slice(row - BLK, row + BLK)

    blocks = []
    for j in range(SEQ // BLK):
        sink = functools.partial(store_state, st4, (j % 4) * n4 + j // 4, 4)
        blocks.append((2, q16_ref.at[0, j], k16_ref.at[0, j], v16_ref.at[0, j], 0, True, None, 0, sink))
    for n in range(n4 // BLK):
        for r in range(4):
            sink = functools.partial(store_state, st1, r + 4 * BLK * n, 4)
            blocks.append((1, q4_ref.at[0, r], k4_ref.at[0, r], v4_ref.at[0, r], n * BLK, n == 0,
                           st4, r * n4 + n * BLK, sink))
    for n in range(SEQ // BLK):
        blocks.append((0, q1_ref.at[0], k1_ref.at[0], v1_ref.at[0], n * BLK, n == 0, st1, n * BLK,
                       functools.partial(finish, n * BLK)))

    def logits(blk):
        br, q_src, k_src, _, row, first, _, _, _ = blk
        q = q_src[row:row + BLK, :]
        zero = jnp.zeros_like(q)
        qst = jnp.concatenate([jnp.where(head_a, q, zero), jnp.where(head_a, zero, q)], axis=0)
        tab = tab_ref[0, br, :, BLK:] if first else tab_ref[0, br]
        return _nt_dot(qst, k_src[key_rows(row, first), :]) + tab

    def update(blk, s):
        _, _, _, v_src, row, first, st_src, st_row, sink = blk
        state = None if st_src is None else load_state(st_src, st_row)
        sink(_softmax_update(s, v_src[key_rows(row, first), :], state))

    pending = []
    for blk in blocks:
        pending.append((blk, logits(blk)))
        if len(pending) > LOOKAHEAD:
            update(*pending.pop(0))
    for item in pending:
        update(*item)


def _dilated_attention(qkv1, qkv4, qkv16, tables):
    bsz = qkv1[0].shape[0]
    pair = pl.BlockSpec((1, SEQ, LANES), lambda p, b: (b, 0, p))
    pair_res = lambda d: pl.BlockSpec((1, d, SEQ // d, LANES), lambda p, b: (b, 0, 0, p))
    state = pltpu.VMEM((3, 2, SEQ, LANES), F32)
    return pl.pallas_call(
        _dil_kernel,
        grid=(N_PAIRS, bsz),
        in_specs=[pair] * 3 + [pair_res(4)] * 3 + [pair_res(16)] * 3 + [
            pl.BlockSpec((1, 3, 2 * BLK, 2 * BLK), lambda p, b: (p, 0, 0, 0))],
        out_specs=pair,
        out_shape=jax.ShapeDtypeStruct((bsz, SEQ, D_A), BF16),
        scratch_shapes=[state, state],
        compiler_params=pltpu.CompilerParams(
            dimension_semantics=("parallel", "parallel"), vmem_limit_bytes=VMEM_LIMIT),
        name="dilated_attn",
    )(*qkv1, *qkv4, *qkv16, tables)


def _mla_kernel(qn_ref, qr_ref, kn_ref, kr_ref, v_ref, o_ref):
    i = pl.program_id(1)
    qb = QB_MLA
    kc = KC_MLA
    lane = lax.broadcasted_iota(jnp.int32, (qb, LANES), 1)
    head_a = lane < B_NOPE_DIM
    slot = lane // B_ROPE_DIM

    def stacked_q(p):
        qn = qn_ref[0, :, p * LANES:(p + 1) * LANES]
        qr = qr_ref[0, :, (p // 2) * LANES:(p // 2 + 1) * LANES]
        zero = jnp.zeros_like(qn)
        sa = 2 * (p % 2)
        q_a = jnp.concatenate([jnp.where(head_a, qn, zero), jnp.where(slot == sa, qr, zero)], axis=1)
        q_b = jnp.concatenate([jnp.where(head_a, zero, qn), jnp.where(slot == sa + 1, qr, zero)], axis=1)
        return jnp.concatenate([q_a, q_b], axis=0)

    qst = [stacked_q(p) for p in range(N_PAIRS)]

    def logits(p, keys, diagonal):
        cols = slice(p * LANES, (p + 1) * LANES)
        s = _nt_dot(qst[p], jnp.concatenate([kn_ref[0, keys, cols], kr_ref[0, keys, :]], axis=1))
        if diagonal:
            qpos = lax.broadcasted_iota(jnp.int32, (2 * qb, qb), 0) % qb
            kpos = lax.broadcasted_iota(jnp.int32, (2 * qb, qb), 1)
            tail = jnp.where(qpos >= kpos, s[:, -qb:], NEG)
            s = tail if s.shape[1] == qb else jnp.concatenate([s[:, :-qb], tail], axis=1)
        return s

    def update(p, keys, s, state):
        nk = s.shape[1]
        m_cur = jnp.max(s, axis=1, keepdims=True)
        m_new = jnp.broadcast_to(m_cur, (2 * qb, LANES)) if state is None else jnp.maximum(state[0], m_cur)
        pr = jnp.exp2(s - jnp.concatenate([m_new] * (nk // LANES), axis=1))
        v_ones = jnp.concatenate([v_ref[0, keys, p * LANES:(p + 1) * LANES], jnp.ones((nk, LANES), BF16)],
                                 axis=1)
        pv = jnp.dot(pr.astype(BF16), v_ones, preferred_element_type=F32)
        acc, l_new = pv[:, :LANES], pv[:, LANES:]
        if state is None:
            return m_new, l_new, acc
        alpha = jnp.exp2(state[0] - m_new)
        return m_new, alpha * state[1] + l_new, alpha * state[2] + acc

    def run(qi):
        end = (qi + 1) * qb
        chunks = [slice(lo, min(lo + kc, end)) for lo in range(0, end, kc)]
        steps = [(p, keys) for keys in chunks for p in range(N_PAIRS)]
        states = [None] * N_PAIRS
        pending = []
        for p, keys in steps:
            pending.append((p, keys, logits(p, keys, keys.stop == end)))
            if len(pending) > LOOKAHEAD_MLA:
                p0, k0, s0 = pending.pop(0)
                states[p0] = update(p0, k0, s0, states[p0])
        for p0, k0, s0 in pending:
            states[p0] = update(p0, k0, s0, states[p0])
        for p in range(N_PAIRS):
            _, l, acc = states[p]
            o = acc / l
            o_ref[0, :, p * LANES:(p + 1) * LANES] = jnp.where(head_a, o[:qb], o[qb:]).astype(BF16)

    for qi in range(SEQ // qb):
        pl.when(i == qi)(functools.partial(run, qi))


def _latent_attention(qn, qr, kn, kr, vb):
    bsz = qn.shape[0]
    qb = QB_MLA
    blk = lambda rows, w: pl.BlockSpec((1, rows, w), lambda b, i: (b, i if rows == qb else 0, 0))
    return pl.pallas_call(
        _mla_kernel,
        grid=(bsz, SEQ // qb),
        in_specs=[blk(qb, D_B), blk(qb, B_HEADS * B_ROPE_DIM), blk(SEQ, D_B), blk(SEQ, LANES),
                  blk(SEQ, D_B)],
        out_specs=blk(qb, D_B),
        out_shape=jax.ShapeDtypeStruct((bsz, SEQ, D_B), BF16),
        compiler_params=pltpu.CompilerParams(
            dimension_semantics=("parallel", "parallel"), vmem_limit_bytes=VMEM_LIMIT),
        name="latent_attn",
    )(qn, qr, kn, kr, vb)


def _post_kernel(x_ref, oa_ref, ob_ref, g1_ref, sh2_ref, sc2_ref, g2_ref, goa_ref, gob_ref,
                 wout_ref, gn2_ref, wfi_ref, wfo_ref, gfin_ref, o_ref, a_scr):
    chains = [slice(t * SUB_POST, (t + 1) * SUB_POST) for t in range(TM_POST // SUB_POST)]

    def mixed(rows):
        ya = _rms(oa_ref[0, rows, :].astype(F32), goa_ref[...])
        yb = _rms(ob_ref[0, rows, :].astype(F32), gob_ref[...])
        y = jnp.concatenate([ya, yb], axis=1).astype(BF16)
        return jnp.dot(y, wout_ref[...], preferred_element_type=F32)

    def ffn(rows, x1):
        h2 = (_rms(x1, gn2_ref[...]) * (1.0 + sc2_ref[0]) + sh2_ref[0]).astype(BF16)
        for c in range(D_FF // FF_CHUNK):
            lo = c * FF_CHUNK
            g = jnp.dot(h2, wfi_ref[:, lo:lo + FF_CHUNK], preferred_element_type=F32)
            u = jnp.dot(h2, wfi_ref[:, D_FF + lo:D_FF + lo + FF_CHUNK], preferred_element_type=F32)
            a_scr[rows, lo:lo + FF_CHUNK] = (jax.nn.silu(g) * u).astype(BF16)
        return jnp.dot(a_scr[rows, :], wfo_ref[...], preferred_element_type=F32)

    mix = [mixed(rows) for rows in chains]
    x1 = [x_ref[0, rows, :] + g1_ref[0] * m for rows, m in zip(chains, mix)]
    f = [ffn(rows, x) for rows, x in zip(chains, x1)]
    for rows, x, y in zip(chains, x1, f):
        o_ref[0, rows, :] = _rms(x + g2_ref[0] * y, gfin_ref[...])


def _post(x, oa, ob, g1m, sh2, sc2, g2m, goa, gob, wout, gn2, wfi, wfo, gfin):
    bsz = x.shape[0]
    tm = TM_POST
    tok = lambda w: pl.BlockSpec((1, tm, w), lambda b, j: (b, j, 0))
    per_b = pl.BlockSpec((1, 1, D_MODEL), lambda b, j: (b, 0, 0))
    return pl.pallas_call(
        _post_kernel,
        grid=(bsz, SEQ // tm),
        in_specs=[tok(D_MODEL), tok(D_A), tok(D_B), per_b, per_b, per_b, per_b,
                  _const_spec((1, D_A)), _const_spec((1, D_B)), _const_spec(wout.shape),
                  _const_spec((1, D_MODEL)), _const_spec(wfi.shape), _const_spec(wfo.shape),
                  _const_spec((1, D_MODEL))],
        out_specs=tok(D_MODEL),
        out_shape=jax.ShapeDtypeStruct((bsz, SEQ, D_MODEL), F32),
        scratch_shapes=[pltpu.VMEM((tm, D_FF), BF16)],
        compiler_params=pltpu.CompilerParams(
            dimension_semantics=("parallel", "parallel"), vmem_limit_bytes=VMEM_LIMIT),
        name="out_proj_ffn",
    )(x, oa, ob, g1m, sh2, sc2, g2m, goa, gob, wout, gn2, wfi, wfo, gfin)


def _rotate_half_cols(w):
    lead = w.shape[:-1]
    half = B_ROPE_DIM // 2
    w = w.reshape(lead + (-1, 2, half))
    return jnp.stack([-w[..., 1, :], w[..., 0, :]], axis=-2).reshape(lead + (-1,))


def _t5_bucket(dist):
    max_exact = N_BUCKETS // 2
    d = np.maximum(dist, 1).astype(np.float64)
    large = max_exact + (np.log(d / max_exact) / np.log(MAX_DISTANCE / max_exact)
                         * (N_BUCKETS - max_exact)).astype(np.int64)
    large = np.minimum(large, N_BUCKETS - 1)
    return np.where(dist < max_exact, dist, large).astype(np.int32)


def _bias_kernel(bucket_ref, rb_ref, o_ref):
    p = pl.program_id(0)
    for br in range(len(DILATIONS)):
        bucket = bucket_ref[br]
        for hh in range(2):
            tab = jnp.full((BLK, 2 * BLK), NEG, F32)
            for k in range(N_BUCKETS):
                tab = jnp.where(bucket == k, rb_ref[k, 2 * p + hh] * LOG2E, tab)
            o_ref[0, br, hh * BLK:(hh + 1) * BLK, :] = tab


def _bias_tables(rel_bias):
    a = np.arange(BLK)[:, None]
    bk = np.arange(2 * BLK)[None, :]
    steps = BLK + a - bk
    valid = (steps >= 0) & (steps <= BLK)
    bucket = np.stack([np.where(valid, _t5_bucket(np.clip(steps, 0, BLK) * dil), -1)
                       for dil in DILATIONS]).astype(np.int32)
    nbr = len(DILATIONS)
    return pl.pallas_call(
        _bias_kernel,
        grid=(N_PAIRS,),
        in_specs=[pl.BlockSpec((nbr, BLK, 2 * BLK), lambda p: (0, 0, 0)),
                  pl.BlockSpec(memory_space=pltpu.SMEM)],
        out_specs=pl.BlockSpec((1, nbr, 2 * BLK, 2 * BLK), lambda p: (p, 0, 0, 0)),
        out_shape=jax.ShapeDtypeStruct((N_PAIRS, nbr, 2 * BLK, 2 * BLK), F32),
        name="bias_tables",
    )(jnp.asarray(bucket), rel_bias)


def _rope_tables():
    half = B_ROPE_DIM // 2
    inv = ROPE_THETA ** (-jnp.arange(half, dtype=F32) / half)
    ang = jnp.arange(SEQ, dtype=F32)[:, None] * inv[None, :]
    reps = LANES // half
    return jnp.tile(jnp.cos(ang), (1, reps)), jnp.tile(jnp.sin(ang), (1, reps))


def kernel(x, c, w_ada, b_ada, g_norm1, w_in, g_cq, w_uq, g_ckv, w_ukv, rel_bias, g_out_a, g_out_b,
           w_out, g_norm2, w_ffn_in, w_ffn_out, g_final):
    bsz = x.shape[0]
    assert x.shape == (bsz, SEQ, D_MODEL) and w_ada.shape[0] == 1
    l = 0

    mod = _modulation(c, w_ada[l], b_ada[l]).reshape(bsz, N_MOD, 1, D_MODEL)
    sh1, sc1, g1m, sh2, sc2, g2m = (mod[:, i] for i in range(N_MOD))

    wi = w_in[l]
    kr_cols = wi[:, 3 * D_A + Q_LORA + KV_LORA:]
    win = jnp.concatenate([wi[:, :D_A] * (A_HEAD_DIM ** -0.5 * LOG2E), wi[:, D_A:3 * D_A + Q_LORA + KV_LORA],
                           kr_cols, _rotate_half_cols(kr_cols),
                           jnp.zeros((D_MODEL, LANES - 2 * B_ROPE_DIM), F32)], axis=1).astype(BF16)
    wq = (w_uq[l] * ((B_NOPE_DIM + B_ROPE_DIM) ** -0.5 * LOG2E)).reshape(
        Q_LORA, B_HEADS, B_NOPE_DIM + B_ROPE_DIM)
    wq_rope = wq[:, :, B_NOPE_DIM:].reshape(Q_LORA, -1)
    wuq = jnp.concatenate([wq[:, :, :B_NOPE_DIM].reshape(Q_LORA, -1), wq_rope,
                           _rotate_half_cols(wq_rope)], axis=1).astype(BF16)
    wkv = w_ukv[l].reshape(KV_LORA, B_HEADS, B_NOPE_DIM + B_V_DIM)
    wukv = jnp.concatenate([wkv[:, :, :B_NOPE_DIM].reshape(KV_LORA, -1),
                            wkv[:, :, B_NOPE_DIM:].reshape(KV_LORA, -1)], axis=1).astype(BF16)
    cos_t, sin_t = _rope_tables()

    proj = _projections(
        x, sh1, sc1, g_norm1[l].reshape(1, -1), cos_t, sin_t, win, g_cq[l].reshape(1, -1), wuq,
        g_ckv[l].reshape(1, -1), wukv)
    qn, qr, kn, kr, vb = proj[3:8]

    out_a = _dilated_attention(proj[0:3], proj[8:11], proj[11:14], _bias_tables(rel_bias))
    out_b = _latent_attention(qn, qr, kn, kr, vb)

    return _post(x, out_a, out_b, g1m, sh2, sc2, g2m, g_out_a[l].reshape(1, -1),
                 g_out_b[l].reshape(1, -1), w_out[l].astype(BF16), g_norm2[l].reshape(1, -1),
                 w_ffn_in[l].astype(BF16), w_ffn_out[l].astype(BF16), g_final.reshape(1, -1))
```

```python
import functools

import jax
import jax.numpy as jnp
import numpy as np
from jax import lax
from jax.experimental import pallas as pl
from jax.experimental.pallas import tpu as pltpu

F32 = jnp.float32
BF16 = jnp.bfloat16

D_MODEL = 1024
SEQ = 2048
A_HEADS = 8
A_HEAD_DIM = 64
D_A = A_HEADS * A_HEAD_DIM
DILATIONS = (1, 4, 16)
BLK = 128
B_HEADS = 8
B_NOPE_DIM = 64
B_ROPE_DIM = 32
B_V_DIM = 64
D_B = B_HEADS * B_V_DIM
Q_LORA = 384
KV_LORA = 256
ROPE_THETA = 10000.0
N_BUCKETS = 32
MAX_DISTANCE = 2048
D_FF = 2816
N_MOD = 6
EPS = 1e-6
NEG = -1e30
LOG2E = 1.4426950408889634

LANES = 128
N_PAIRS = A_HEADS // 2
VMEM_LIMIT = 56 * 1024 * 1024

_C_QA, _C_KA, _C_VA = 0, D_A, 2 * D_A
_C_CQ = 3 * D_A
_C_CKV = _C_CQ + Q_LORA
_C_KR = _C_CKV + KV_LORA
P_IN2 = _C_KR + LANES

TM_PROJ = 1024
SUB_PROJ = 512
TM_POST = 1024
SUB_POST = 512
FF_CHUNK = 256
LOOKAHEAD = 2
LOOKAHEAD_MLA = 1
QB_MLA = 256
QBS_PER_STEP = 2
KC_MLA = 512


def _nt_dot(a, b):
    return lax.dot_general(a, b, (((1,), (1,)), ((), ())), preferred_element_type=F32)


def _rms(x, g):
    return x * lax.rsqrt(jnp.mean(x * x, axis=-1, keepdims=True) + EPS) * g


def _mod_kernel(c_ref, w_ref, b_ref, o_ref):
    cond = jax.nn.silu(c_ref[...])
    o_ref[...] = jnp.dot(cond.astype(BF16), w_ref[...].astype(BF16),
                         preferred_element_type=F32) + b_ref[...]


def _modulation(c, w_ada, b_ada):
    bsz = c.shape[0]
    n = w_ada.shape[1]
    bn = 1536
    return pl.pallas_call(
        _mod_kernel,
        grid=(n // bn,),
        in_specs=[pl.BlockSpec((bsz, D_MODEL), lambda j: (0, 0)),
                  pl.BlockSpec((D_MODEL, bn), lambda j: (0, j)),
                  pl.BlockSpec((1, bn), lambda j: (0, j))],
        out_specs=pl.BlockSpec((bsz, bn), lambda j: (0, j)),
        out_shape=jax.ShapeDtypeStruct((bsz, n), F32),
        compiler_params=pltpu.CompilerParams(vmem_limit_bytes=VMEM_LIMIT),
        name="adaln_mod",
    )(c, w_ada, b_ada.reshape(1, n))


def _proj_kernel(x_ref, sh_ref, sc_ref, g1_ref, cos_ref, sin_ref, win_ref, gcq_ref, wuq_ref,
                 gckv_ref, wukv_ref,
                 qa_ref, ka_ref, va_ref, qn_ref, qr_ref, kn_ref, kr_ref, vb_ref,
                 qa4_ref, ka4_ref, va4_ref, qa16_ref, ka16_ref, va16_ref, fa_scr, fb_scr):
    for t in range(TM_PROJ // SUB_PROJ):
        rows = slice(t * SUB_PROJ, (t + 1) * SUB_PROJ)
        h = _rms(x_ref[0, rows, :], g1_ref[...]) * (1.0 + sc_ref[0]) + sh_ref[0]
        hb = h.astype(BF16)

        def proj(lo, hi):
            return jnp.dot(hb, win_ref[:, lo:hi], preferred_element_type=F32)

        def emit_orders(res, nat_ref, d4_ref, d16_ref):
            nat_ref[0, rows, :] = res.astype(BF16)
            fa, fb = fa_scr.at[t], fb_scr.at[t]
            n4, n16 = SUB_PROJ // 4, SUB_PROJ // 16
            for g in range(D_A // LANES):
                lanes = slice(g * LANES, (g + 1) * LANES)
                fa[g] = res[:, lanes]
                for r in range(4):
                    x4 = fa[g, pl.ds(r, n4, stride=4), :]
                    fb[g, r * n4:(r + 1) * n4, :] = x4
                    d4_ref[0, r, t * n4:(t + 1) * n4, lanes] = x4.astype(BF16)
                for j in range(16):
                    x16 = fb[g, pl.ds((j % 4) * n4 + j // 4, n16, stride=4), :]
                    d16_ref[0, j, t * n16:(t + 1) * n16, lanes] = x16.astype(BF16)

        emit_orders(proj(_C_QA, _C_KA), qa_ref, qa4_ref, qa16_ref)
        emit_orders(proj(_C_KA, _C_VA), ka_ref, ka4_ref, ka16_ref)
        emit_orders(proj(_C_VA, _C_CQ), va_ref, va4_ref, va16_ref)

        cos = cos_ref[rows, :]
        sin = sin_ref[rows, :]
        kr = proj(_C_KR, P_IN2)
        nr = B_ROPE_DIM
        kr = kr[:, :nr] * cos[:, :nr] + kr[:, nr:2 * nr] * sin[:, :nr]
        kr_ref[0, rows, :] = jnp.concatenate([kr] * (LANES // nr), axis=1).astype(BF16)

        cq = _rms(proj(_C_CQ, _C_CKV), gcq_ref[...]).astype(BF16)
        q = jnp.dot(cq, wuq_ref[...], preferred_element_type=F32)
        qn_ref[0, rows, :] = q[:, :D_B].astype(BF16)
        cos2 = jnp.concatenate([cos, cos], axis=1)
        sin2 = jnp.concatenate([sin, sin], axis=1)
        n_r = B_HEADS * B_ROPE_DIM
        qr_ref[0, rows, :] = (q[:, D_B:D_B + n_r] * cos2 + q[:, D_B + n_r:] * sin2).astype(BF16)

        ckv = _rms(proj(_C_CKV, _C_KR), gckv_ref[...]).astype(BF16)
        kv = jnp.dot(ckv, wukv_ref[...], preferred_element_type=F32)
        kn_ref[0, rows, :] = kv[:, :D_B].astype(BF16)
        vb_ref[0, rows, :] = kv[:, D_B:].astype(BF16)


def _const_spec(shape):
    nd = len(shape)
    return pl.BlockSpec(shape, lambda *_: (0,) * nd, pipeline_mode=pl.Buffered(1))


def _projections(x, sh1, sc1, g1, cos_t, sin_t, win, gcq, wuq, gckv, wukv):
    bsz = x.shape[0]
    tm = TM_PROJ
    tok = lambda w: pl.BlockSpec((1, tm, w), lambda b, j: (b, j, 0))
    per_b = pl.BlockSpec((1, 1, D_MODEL), lambda b, j: (b, 0, 0))
    tab = pl.BlockSpec((tm, LANES), lambda b, j: (j, 0))
    widths = (D_A, D_A, D_A, D_B, B_HEADS * B_ROPE_DIM, D_B, LANES, D_B)
    res_spec = lambda d: pl.BlockSpec((1, d, tm // d, D_A), lambda b, j: (b, 0, j, 0))
    res_shape = lambda d: jax.ShapeDtypeStruct((bsz, d, SEQ // d, D_A), BF16)
    chain_f32 = pltpu.VMEM((tm // SUB_PROJ, D_A // LANES, SUB_PROJ, LANES), F32)
    return pl.pallas_call(
        _proj_kernel,
        grid=(bsz, SEQ // tm),
        in_specs=[tok(D_MODEL), per_b, per_b, _const_spec((1, D_MODEL)), tab, tab,
                  _const_spec(win.shape), _const_spec((1, Q_LORA)), _const_spec(wuq.shape),
                  _const_spec((1, KV_LORA)), _const_spec(wukv.shape)],
        out_specs=[tok(w) for w in widths] + [res_spec(4)] * 3 + [res_spec(16)] * 3,
        out_shape=[jax.ShapeDtypeStruct((bsz, SEQ, w), BF16) for w in widths]
        + [res_shape(4)] * 3 + [res_shape(16)] * 3,
        scratch_shapes=[chain_f32, chain_f32],
        compiler_params=pltpu.CompilerParams(
            dimension_semantics=("parallel", "parallel"), vmem_limit_bytes=VMEM_LIMIT),
        name="in_proj",
    )(x, sh1, sc1, g1, cos_t, sin_t, win, gcq, wuq, gckv, wukv)


def _stack_rows(ref, c, row):
    return jnp.concatenate([ref[c, 0, pl.ds(row, BLK), :], ref[c, 1, pl.ds(row, BLK), :]], axis=0)


def _softmax_update(s, vb, state):
    nk = s.shape[1]
    m_cur = jnp.max(s, axis=1, keepdims=True)
    if state is None:
        m_new = jnp.broadcast_to(m_cur, (2 * BLK, LANES))
    else:
        m0, l0, acc0 = state
        m_new = jnp.maximum(m0, m_cur)
    m_wide = m_new if nk == LANES else jnp.concatenate([m_new] * (nk // LANES), axis=1)
    p = jnp.exp2(s - m_wide)
    v_ones = jnp.concatenate([vb, jnp.ones((nk, LANES), BF16)], axis=1)
    pv = jnp.dot(p.astype(BF16), v_ones, preferred_element_type=F32)
    acc, l_new = pv[:, :LANES], pv[:, LANES:]
    if state is not None:
        alpha = jnp.exp2(m0 - m_new)
        l_new = alpha * l0 + l_new
        acc = alpha * acc0 + acc
    return m_new, l_new, acc


def _dil_kernel(q1_ref, k1_ref, v1_ref, q4_ref, k4_ref, v4_ref, q16_ref, k16_ref, v16_ref, tab_ref,
                o_ref, st4, st1):
    lane = lax.broadcasted_iota(jnp.int32, (BLK, LANES), 1)
    head_a = lane < A_HEAD_DIM
    n4 = SEQ // 4

    def store_state(dst, start, stride, state):
        for c, x in enumerate(state):
            rows = pl.ds(start, BLK, stride=stride) if stride > 1 else pl.ds(start, BLK)
            dst[c, 0, rows, :] = x[:BLK]
            dst[c, 1, rows, :] = x[BLK:]

    def load_state(src, row):
        return tuple(_stack_rows(src, c, row) for c in range(3))

    def finish(row, state):
        _, l, acc = state
        o = acc / l
        o_ref[0, row:row + BLK, :] = jnp.where(head_a, o[:BLK], o[BLK:]).astype(BF16)

    def key_rows(row, first):
        return slice(row, row + BLK) if first else slice(row - BLK, row + BLK)

    blocks = []
    for j in range(SEQ // BLK):
        sink = functools.partial(store_state, st4, (j % 4) * n4 + j // 4, 4)
        blocks.append((2, q16_ref.at[0, j], k16_ref.at[0, j], v16_ref.at[0, j], 0, True, None, 0, sink))
    for n in range(n4 // BLK):
        for r in range(4):
            sink = functools.partial(store_state, st1, r + 4 * BLK * n, 4)
            blocks.append((1, q4_ref.at[0, r], k4_ref.at[0, r], v4_ref.at[0, r], n * BLK, n == 0,
                           st4, r * n4 + n * BLK, sink))
    for n in range(SEQ // BLK):
        blocks.append((0, q1_ref.at[0], k1_ref.at[0], v1_ref.at[0], n * BLK, n == 0, st1, n * BLK,
                       functools.partial(finish, n * BLK)))

    def logits(blk):
        br, q_src, k_src, _, row, first, _, _, _ = blk
        q = q_src[row:row + BLK, :]
        zero = jnp.zeros_like(q)
        qst = jnp.concatenate([jnp.where(head_a, q, zero), jnp.where(head_a, zero, q)], axis=0)
        tab = tab_ref[0, br, :, BLK:] if first else tab_ref[0, br]
        return _nt_dot(qst, k_src[key_rows(row, first), :]) + tab

    def update(blk, s):
        _, _, _, v_src, row, first, st_src, st_row, sink = blk
        state = None if st_src is None else load_state(st_src, st_row)
        sink(_softmax_update(s, v_src[key_rows(row, first), :], state))

    pending = []
    for blk in blocks:
        pending.append((blk, logits(blk)))
        if len(pending) > LOOKAHEAD:
            update(*pending.pop(0))
    for item in pending:
        update(*item)


def _dilated_attention(qkv1, qkv4, qkv16, tables):
    bsz = qkv1[0].shape[0]
    pair = pl.BlockSpec((1, SEQ, LANES), lambda p, b: (b, 0, p))
    pair_res = lambda d: pl.BlockSpec((1, d, SEQ // d, LANES), lambda p, b: (b, 0, 0, p))
    state = pltpu.VMEM((3, 2, SEQ, LANES), F32)
    return pl.pallas_call(
        _dil_kernel,
        grid=(N_PAIRS, bsz),
        in_specs=[pair] * 3 + [pair_res(4)] * 3 + [pair_res(16)] * 3 + [
            pl.BlockSpec((1, 3, 2 * BLK, 2 * BLK), lambda p, b: (p, 0, 0, 0))],
        out_specs=pair,
        out_shape=jax.ShapeDtypeStruct((bsz, SEQ, D_A), BF16),
        scratch_shapes=[state, state],
        compiler_params=pltpu.CompilerParams(
            dimension_semantics=("parallel", "parallel"), vmem_limit_bytes=VMEM_LIMIT),
        name="dilated_attn",
    )(*qkv1, *qkv4, *qkv16, tables)


def _mla_kernel(qn_ref, qr_ref, kn_ref, kr_ref, v_ref, o_ref):
    g = pl.program_id(1)
    qb = QB_MLA
    kc = KC_MLA
    lane = lax.broadcasted_iota(jnp.int32, (qb, LANES), 1)
    head_a = lane < B_NOPE_DIM
    slot = lane // B_ROPE_DIM

    def stacked_q(h, p):
        rows = slice(h * qb, (h + 1) * qb)
        qn = qn_ref[0, rows, p * LANES:(p + 1) * LANES]
        qr = qr_ref[0, rows, (p // 2) * LANES:(p // 2 + 1) * LANES]
        zero = jnp.zeros_like(qn)
        sa = 2 * (p % 2)
        q_a = jnp.concatenate([jnp.where(head_a, qn, zero), jnp.where(slot == sa, qr, zero)], axis=1)
        q_b = jnp.concatenate([jnp.where(head_a, zero, qn), jnp.where(slot == sa + 1, qr, zero)], axis=1)
        return jnp.concatenate([q_a, q_b], axis=0)

    qst = {(h, p): stacked_q(h, p) for h in range(QBS_PER_STEP) for p in range(N_PAIRS)}

    def logits(chain, keys, diagonal):
        cols = slice(chain[1] * LANES, (chain[1] + 1) * LANES)
        s = _nt_dot(qst[chain], jnp.concatenate([kn_ref[0, keys, cols], kr_ref[0, keys, :]], axis=1))
        if diagonal:
            qpos = lax.broadcasted_iota(jnp.int32, (2 * qb, qb), 0) % qb
            kpos = lax.broadcasted_iota(jnp.int32, (2 * qb, qb), 1)
            tail = jnp.where(qpos >= kpos, s[:, -qb:], NEG)
            s = tail if s.shape[1] == qb else jnp.concatenate([s[:, :-qb], tail], axis=1)
        return s

    def update(chain, keys, s, state):
        nk = s.shape[1]
        p = chain[1]
        m_cur = jnp.max(s, axis=1, keepdims=True)
        m_new = jnp.broadcast_to(m_cur, (2 * qb, LANES)) if state is None else jnp.maximum(state[0], m_cur)
        pr = jnp.exp2(s - jnp.concatenate([m_new] * (nk // LANES), axis=1))
        v_ones = jnp.concatenate([v_ref[0, keys, p * LANES:(p + 1) * LANES], jnp.ones((nk, LANES), BF16)],
                                 axis=1)
        pv = jnp.dot(pr.astype(BF16), v_ones, preferred_element_type=F32)
        acc, l_new = pv[:, :LANES], pv[:, LANES:]
        if state is None:
            return m_new, l_new, acc
        alpha = jnp.exp2(state[0] - m_new)
        return m_new, alpha * state[1] + l_new, alpha * state[2] + acc

    def run(step):
        ends = [(step * QBS_PER_STEP + h + 1) * qb for h in range(QBS_PER_STEP)]
        steps = []
        for lo in range(0, max(ends), kc):
            for h, end in enumerate(ends):
                if lo < end:
                    steps += [((h, p), slice(lo, min(lo + kc, end)), end) for p in range(N_PAIRS)]
        states = {chain: None for chain in qst}
        pending = []
        for chain, keys, end in steps:
            pending.append((chain, keys, logits(chain, keys, keys.stop == end)))
            if len(pending) > LOOKAHEAD_MLA:
                c0, k0, s0 = pending.pop(0)
                states[c0] = update(c0, k0, s0, states[c0])
        for c0, k0, s0 in pending:
            states[c0] = update(c0, k0, s0, states[c0])
        for (h, p), (_, l, acc) in states.items():
            o = acc / l
            o_ref[0, h * qb:(h + 1) * qb, p * LANES:(p + 1) * LANES] = jnp.where(
                head_a, o[:qb], o[qb:]).astype(BF16)

    for step in range(SEQ // (qb * QBS_PER_STEP)):
        pl.when(g == step)(functools.partial(run, step))


def _latent_attention(qn, qr, kn, kr, vb):
    bsz = qn.shape[0]
    rows = QB_MLA * QBS_PER_STEP
    blk = lambda r, w: pl.BlockSpec((1, r, w), lambda b, i: (b, i if r == rows else 0, 0))
    return pl.pallas_call(
        _mla_kernel,
        grid=(bsz, SEQ // rows),
        in_specs=[blk(rows, D_B), blk(rows, B_HEADS * B_ROPE_DIM), blk(SEQ, D_B), blk(SEQ, LANES),
                  blk(SEQ, D_B)],
        out_specs=blk(rows, D_B),
        out_shape=jax.ShapeDtypeStruct((bsz, SEQ, D_B), BF16),
        compiler_params=pltpu.CompilerParams(
            dimension_semantics=("parallel", "parallel"), vmem_limit_bytes=VMEM_LIMIT),
        name="latent_attn",
    )(qn, qr, kn, kr, vb)


def _post_kernel(x_ref, oa_ref, ob_ref, g1_ref, sh2_ref, sc2_ref, g2_ref, goa_ref, gob_ref,
                 wout_ref, gn2_ref, wfi_ref, wfo_ref, gfin_ref, o_ref, a_scr):
    chains = [slice(t * SUB_POST, (t + 1) * SUB_POST) for t in range(TM_POST // SUB_POST)]

    def mixed(rows):
        ya = _rms(oa_ref[0, rows, :].astype(F32), goa_ref[...])
        yb = _rms(ob_ref[0, rows, :].astype(F32), gob_ref[...])
        y = jnp.concatenate([ya, yb], axis=1).astype(BF16)
        return jnp.dot(y, wout_ref[...], preferred_element_type=F32)

    def ffn(rows, x1):
        h2 = (_rms(x1, gn2_ref[...]) * (1.0 + sc2_ref[0]) + sh2_ref[0]).astype(BF16)
        for c in range(D_FF // FF_CHUNK):
            lo = c * FF_CHUNK
            g = jnp.dot(h2, wfi_ref[:, lo:lo + FF_CHUNK], preferred_element_type=F32)
            u = jnp.dot(h2, wfi_ref[:, D_FF + lo:D_FF + lo + FF_CHUNK], preferred_element_type=F32)
            a_scr[rows, lo:lo + FF_CHUNK] = (jax.nn.silu(g) * u).astype(BF16)
        return jnp.dot(a_scr[rows, :], wfo_ref[...], preferred_element_type=F32)

    mix = [mixed(rows) for rows in chains]
    x1 = [x_ref[0, rows, :] + g1_ref[0] * m for rows, m in zip(chains, mix)]
    f = [ffn(rows, x) for rows, x in zip(chains, x1)]
    for rows, x, y in zip(chains, x1, f):
        o_ref[0, rows, :] = _rms(x + g2_ref[0] * y, gfin_ref[...])


def _post(x, oa, ob, g1m, sh2, sc2, g2m, goa, gob, wout, gn2, wfi, wfo, gfin):
    bsz = x.shape[0]
    tm = TM_POST
    tok = lambda w: pl.BlockSpec((1, tm, w), lambda b, j: (b, j, 0))
    per_b = pl.BlockSpec((1, 1, D_MODEL), lambda b, j: (b, 0, 0))
    return pl.pallas_call(
        _post_kernel,
        grid=(bsz, SEQ // tm),
        in_specs=[tok(D_MODEL), tok(D_A), tok(D_B), per_b, per_b, per_b, per_b,
                  _const_spec((1, D_A)), _const_spec((1, D_B)), _const_spec(wout.shape),
                  _const_spec((1, D_MODEL)), _const_spec(wfi.shape), _const_spec(wfo.shape),
                  _const_spec((1, D_MODEL))],
        out_specs=tok(D_MODEL),
        out_shape=jax.ShapeDtypeStruct((bsz, SEQ, D_MODEL), F32),
        scratch_shapes=[pltpu.VMEM((tm, D_FF), BF16)],
        compiler_params=pltpu.CompilerParams(
            dimension_semantics=("parallel", "parallel"), vmem_limit_bytes=VMEM_LIMIT),
        name="out_proj_ffn",
    )(x, oa, ob, g1m, sh2, sc2, g2m, goa, gob, wout, gn2, wfi, wfo, gfin)


def _rotate_half_cols(w):
    lead = w.shape[:-1]
    half = B_ROPE_DIM // 2
    w = w.reshape(lead + (-1, 2, half))
    return jnp.stack([-w[..., 1, :], w[..., 0, :]], axis=-2).reshape(lead + (-1,))


def _t5_bucket(dist):
    max_exact = N_BUCKETS // 2
    d = np.maximum(dist, 1).astype(np.float64)
    large = max_exact + (np.log(d / max_exact) / np.log(MAX_DISTANCE / max_exact)
                         * (N_BUCKETS - max_exact)).astype(np.int64)
    large = np.minimum(large, N_BUCKETS - 1)
    return np.where(dist < max_exact, dist, large).astype(np.int32)


def _bias_kernel(bucket_ref, rb_ref, o_ref):
    p = pl.program_id(0)
    for br in range(len(DILATIONS)):
        bucket = bucket_ref[br]
        for hh in range(2):
            tab = jnp.full((BLK, 2 * BLK), NEG, F32)
            for k in range(N_BUCKETS):
                tab = jnp.where(bucket == k, rb_ref[k, 2 * p + hh] * LOG2E, tab)
            o_ref[0, br, hh * BLK:(hh + 1) * BLK, :] = tab


def _bias_tables(rel_bias):
    a = np.arange(BLK)[:, None]
    bk = np.arange(2 * BLK)[None, :]
    steps = BLK + a - bk
    valid = (steps >= 0) & (steps <= BLK)
    bucket = np.stack([np.where(valid, _t5_bucket(np.clip(steps, 0, BLK) * dil), -1)
                       for dil in DILATIONS]).astype(np.int32)
    nbr = len(DILATIONS)
    return pl.pallas_call(
        _bias_kernel,
        grid=(N_PAIRS,),
        in_specs=[pl.BlockSpec((nbr, BLK, 2 * BLK), lambda p: (0, 0, 0)),
                  pl.BlockSpec(memory_space=pltpu.SMEM)],
        out_specs=pl.BlockSpec((1, nbr, 2 * BLK, 2 * BLK), lambda p: (p, 0, 0, 0)),
        out_shape=jax.ShapeDtypeStruct((N_PAIRS, nbr, 2 * BLK, 2 * BLK), F32),
        name="bias_tables",
    )(jnp.asarray(bucket), rel_bias)


def _rope_tables():
    half = B_ROPE_DIM // 2
    inv = ROPE_THETA ** (-jnp.arange(half, dtype=F32) / half)
    ang = jnp.arange(SEQ, dtype=F32)[:, None] * inv[None, :]
    reps = LANES // half
    return jnp.tile(jnp.cos(ang), (1, reps)), jnp.tile(jnp.sin(ang), (1, reps))


def kernel(x, c, w_ada, b_ada, g_norm1, w_in, g_cq, w_uq, g_ckv, w_ukv, rel_bias, g_out_a, g_out_b,
           w_out, g_norm2, w_ffn_in, w_ffn_out, g_final):
    bsz = x.shape[0]
    assert x.shape == (bsz, SEQ, D_MODEL) and w_ada.shape[0] == 1
    l = 0

    mod = _modulation(c, w_ada[l], b_ada[l]).reshape(bsz, N_MOD, 1, D_MODEL)
    sh1, sc1, g1m, sh2, sc2, g2m = (mod[:, i] for i in range(N_MOD))

    wi = w_in[l]
    kr_cols = wi[:, 3 * D_A + Q_LORA + KV_LORA:]
    win = jnp.concatenate([wi[:, :D_A] * (A_HEAD_DIM ** -0.5 * LOG2E), wi[:, D_A:3 * D_A + Q_LORA + KV_LORA],
                           kr_cols, _rotate_half_cols(kr_cols),
                           jnp.zeros((D_MODEL, LANES - 2 * B_ROPE_DIM), F32)], axis=1).astype(BF16)
    wq = (w_uq[l] * ((B_NOPE_DIM + B_ROPE_DIM) ** -0.5 * LOG2E)).reshape(
        Q_LORA, B_HEADS, B_NOPE_DIM + B_ROPE_DIM)
    wq_rope = wq[:, :, B_NOPE_DIM:].reshape(Q_LORA, -1)
    wuq = jnp.concatenate([wq[:, :, :B_NOPE_DIM].reshape(Q_LORA, -1), wq_rope,
                           _rotate_half_cols(wq_rope)], axis=1).astype(BF16)
    wkv = w_ukv[l].reshape(KV_LORA, B_HEADS, B_NOPE_DIM + B_V_DIM)
    wukv = jnp.concatenate([wkv[:, :, :B_NOPE_DIM].reshape(KV_LORA, -1),
                            wkv[:, :, B_NOPE_DIM:].reshape(KV_LORA, -1)], axis=1).astype(BF16)
    cos_t, sin_t = _rope_tables()

    proj = _projections(
        x, sh1, sc1, g_norm1[l].reshape(1, -1), cos_t, sin_t, win, g_cq[l].reshape(1, -1), wuq,
        g_ckv[l].reshape(1, -1), wukv)
    qn, qr, kn, kr, vb = proj[3:8]

    out_a = _dilated_attention(proj[0:3], proj[8:11], proj[11:14], _bias_tables(rel_bias))
    out_b = _latent_attention(qn, qr, kn, kr, vb)

    return _post(x, out_a, out_b, g1m, sh2, sc2, g2m, g_out_a[l].reshape(1, -1),
                 g_out_b[l].reshape(1, -1), w_out[l].astype(BF16), g_norm2[l].reshape(1, -1),
                 w_ffn_in[l].astype(BF16), w_ffn_out[l].astype(BF16), g_final.reshape(1, -1))
```

```python
import functools

import jax
import jax.numpy as jnp
import numpy as np
from jax import lax
from jax.experimental import pallas as pl
from jax.experimental.pallas import tpu as pltpu

F32 = jnp.float32
BF16 = jnp.bfloat16

D_MODEL = 1024
SEQ = 2048
A_HEADS = 8
A_HEAD_DIM = 64
D_A = A_HEADS * A_HEAD_DIM
DILATIONS = (1, 4, 16)
BLK = 128
B_HEADS = 8
B_NOPE_DIM = 64
B_ROPE_DIM = 32
B_V_DIM = 64
D_B = B_HEADS * B_V_DIM
Q_LORA = 384
KV_LORA = 256
ROPE_THETA = 10000.0
N_BUCKETS = 32
MAX_DISTANCE = 2048
D_FF = 2816
N_MOD = 6
EPS = 1e-6
NEG = -1e30
LOG2E = 1.4426950408889634

LANES = 128
N_PAIRS = A_HEADS // 2
VMEM_LIMIT = 56 * 1024 * 1024

_C_QA, _C_KA, _C_VA = 0, D_A, 2 * D_A
_C_CQ = 3 * D_A
_C_CKV = _C_CQ + Q_LORA
_C_KR = _C_CKV + KV_LORA
P_IN2 = _C_KR + LANES

TM_PROJ = 1024
SUB_PROJ = 512
TM_POST = 1024
SUB_POST = 512
FF_CHUNK = 256
LOOKAHEAD = 2
LOOKAHEAD_MLA = 1
QB_MLA = 256
QBS_PER_STEP = 8
KC_MLA = 512


def _nt_dot(a, b):
    return lax.dot_general(a, b, (((1,), (1,)), ((), ())), preferred_element_type=F32)


def _rms(x, g):
    return x * lax.rsqrt(jnp.mean(x * x, axis=-1, keepdims=True) + EPS) * g


def _mod_kernel(c_ref, w_ref, b_ref, o_ref):
    cond = jax.nn.silu(c_ref[...])
    o_ref[...] = jnp.dot(cond.astype(BF16), w_ref[...].astype(BF16),
                         preferred_element_type=F32) + b_ref[...]


def _modulation(c, w_ada, b_ada):
    bsz = c.shape[0]
    n = w_ada.shape[1]
    bn = 1536
    return pl.pallas_call(
        _mod_kernel,
        grid=(n // bn,),
        in_specs=[pl.BlockSpec((bsz, D_MODEL), lambda j: (0, 0)),
                  pl.BlockSpec((D_MODEL, bn), lambda j: (0, j)),
                  pl.BlockSpec((1, bn), lambda j: (0, j))],
        out_specs=pl.BlockSpec((bsz, bn), lambda j: (0, j)),
        out_shape=jax.ShapeDtypeStruct((bsz, n), F32),
        compiler_params=pltpu.CompilerParams(vmem_limit_bytes=VMEM_LIMIT),
        name="adaln_mod",
    )(c, w_ada, b_ada.reshape(1, n))


def _proj_kernel(x_ref, sh_ref, sc_ref, g1_ref, cos_ref, sin_ref, win_ref, gcq_ref, wuq_ref,
                 gckv_ref, wukv_ref,
                 qa_ref, ka_ref, va_ref, qn_ref, qr_ref, kn_ref, kr_ref, vb_ref,
                 qa4_ref, ka4_ref, va4_ref, qa16_ref, ka16_ref, va16_ref, fa_scr, fb_scr):
    for t in range(TM_PROJ // SUB_PROJ):
        rows = slice(t * SUB_PROJ, (t + 1) * SUB_PROJ)
        h = _rms(x_ref[0, rows, :], g1_ref[...]) * (1.0 + sc_ref[0]) + sh_ref[0]
        hb = h.astype(BF16)

        def proj(lo, hi):
            return jnp.dot(hb, win_ref[:, lo:hi], preferred_element_type=F32)

        def emit_orders(res, nat_ref, d4_ref, d16_ref):
            nat_ref[0, rows, :] = res.astype(BF16)
            fa, fb = fa_scr.at[t], fb_scr.at[t]
            n4, n16 = SUB_PROJ // 4, SUB_PROJ // 16
            for g in range(D_A // LANES):
                lanes = slice(g * LANES, (g + 1) * LANES)
                fa[g] = res[:, lanes]
                for r in range(4):
                    x4 = fa[g, pl.ds(r, n4, stride=4), :]
                    fb[g, r * n4:(r + 1) * n4, :] = x4
                    d4_ref[0, r, t * n4:(t + 1) * n4, lanes] = x4.astype(BF16)
                for j in range(16):
                    x16 = fb[g, pl.ds((j % 4) * n4 + j // 4, n16, stride=4), :]
                    d16_ref[0, j, t * n16:(t + 1) * n16, lanes] = x16.astype(BF16)

        emit_orders(proj(_C_QA, _C_KA), qa_ref, qa4_ref, qa16_ref)
        emit_orders(proj(_C_KA, _C_VA), ka_ref, ka4_ref, ka16_ref)
        emit_orders(proj(_C_VA, _C_CQ), va_ref, va4_ref, va16_ref)

        cos = cos_ref[rows, :]
        sin = sin_ref[rows, :]
        kr = proj(_C_KR, P_IN2)
        nr = B_ROPE_DIM
        kr = kr[:, :nr] * cos[:, :nr] + kr[:, nr:2 * nr] * sin[:, :nr]
        kr_ref[0, rows, :] = jnp.concatenate([kr] * (LANES // nr), axis=1).astype(BF16)

        cq = _rms(proj(_C_CQ, _C_CKV), gcq_ref[...]).astype(BF16)
        q = jnp.dot(cq, wuq_ref[...], preferred_element_type=F32)
        qn_ref[0, rows, :] = q[:, :D_B].astype(BF16)
        cos2 = jnp.concatenate([cos, cos], axis=1)
        sin2 = jnp.concatenate([sin, sin], axis=1)
        n_r = B_HEADS * B_ROPE_DIM
        qr_ref[0, rows, :] = (q[:, D_B:D_B + n_r] * cos2 + q[:, D_B + n_r:] * sin2).astype(BF16)

        ckv = _rms(proj(_C_CKV, _C_KR), gckv_ref[...]).astype(BF16)
        kv = jnp.dot(ckv, wukv_ref[...], preferred_element_type=F32)
        kn_ref[0, rows, :] = kv[:, :D_B].astype(BF16)
        vb_ref[0, rows, :] = kv[:, D_B:].astype(BF16)


def _const_spec(shape):
    nd = len(shape)
    return pl.BlockSpec(shape, lambda *_: (0,) * nd, pipeline_mode=pl.Buffered(1))


def _projections(x, sh1, sc1, g1, cos_t, sin_t, win, gcq, wuq, gckv, wukv):
    bsz = x.shape[0]
    tm = TM_PROJ
    tok = lambda w: pl.BlockSpec((1, tm, w), lambda b, j: (b, j, 0))
    per_b = pl.BlockSpec((1, 1, D_MODEL), lambda b, j: (b, 0, 0))
    tab = pl.BlockSpec((tm, LANES), lambda b, j: (j, 0))
    widths = (D_A, D_A, D_A, D_B, B_HEADS * B_ROPE_DIM, D_B, LANES, D_B)
    res_spec = lambda d: pl.BlockSpec((1, d, tm // d, D_A), lambda b, j: (b, 0, j, 0))
    res_shape = lambda d: jax.ShapeDtypeStruct((bsz, d, SEQ // d, D_A), BF16)
    chain_f32 = pltpu.VMEM((tm // SUB_PROJ, D_A // LANES, SUB_PROJ, LANES), F32)
    return pl.pallas_call(
        _proj_kernel,
        grid=(bsz, SEQ // tm),
        in_specs=[tok(D_MODEL), per_b, per_b, _const_spec((1, D_MODEL)), tab, tab,
                  _const_spec(win.shape), _const_spec((1, Q_LORA)), _const_spec(wuq.shape),
                  _const_spec((1, KV_LORA)), _const_spec(wukv.shape)],
        out_specs=[tok(w) for w in widths] + [res_spec(4)] * 3 + [res_spec(16)] * 3,
        out_shape=[jax.ShapeDtypeStruct((bsz, SEQ, w), BF16) for w in widths]
        + [res_shape(4)] * 3 + [res_shape(16)] * 3,
        scratch_shapes=[chain_f32, chain_f32],
        compiler_params=pltpu.CompilerParams(
            dimension_semantics=("parallel", "parallel"), vmem_limit_bytes=VMEM_LIMIT),
        name="in_proj",
    )(x, sh1, sc1, g1, cos_t, sin_t, win, gcq, wuq, gckv, wukv)


def _stack_rows(ref, c, row):
    return jnp.concatenate([ref[c, 0, pl.ds(row, BLK), :], ref[c, 1, pl.ds(row, BLK), :]], axis=0)


def _softmax_update(s, vb, state):
    nk = s.shape[1]
    m_cur = jnp.max(s, axis=1, keepdims=True)
    if state is None:
        m_new = jnp.broadcast_to(m_cur, (2 * BLK, LANES))
    else:
        m0, l0, acc0 = state
        m_new = jnp.maximum(m0, m_cur)
    m_wide = m_new if nk == LANES else jnp.concatenate([m_new] * (nk // LANES), axis=1)
    p = jnp.exp2(s - m_wide)
    v_ones = jnp.concatenate([vb, jnp.ones((nk, LANES), BF16)], axis=1)
    pv = jnp.dot(p.astype(BF16), v_ones, preferred_element_type=F32)
    acc, l_new = pv[:, :LANES], pv[:, LANES:]
    if state is not None:
        alpha = jnp.exp2(m0 - m_new)
        l_new = alpha * l0 + l_new
        acc = alpha * acc0 + acc
    return m_new, l_new, acc


def _dil_kernel(q1_ref, k1_ref, v1_ref, q4_ref, k4_ref, v4_ref, q16_ref, k16_ref, v16_ref, tab_ref,
                o_ref, st4, st1):
    lane = lax.broadcasted_iota(jnp.int32, (BLK, LANES), 1)
    head_a = lane < A_HEAD_DIM
    n4 = SEQ // 4

    def store_state(dst, start, stride, state):
        for c, x in enumerate(state):
            rows = pl.ds(start, BLK, stride=stride) if stride > 1 else pl.ds(start, BLK)
            dst[c, 0, rows, :] = x[:BLK]
            dst[c, 1, rows, :] = x[BLK:]

    def load_state(src, row):
        return tuple(_stack_rows(src, c, row) for c in range(3))

    def finish(row, state):
        _, l, acc = state
        o = acc / l
        o_ref[0, row:row + BLK, :] = jnp.where(head_a, o[:BLK], o[BLK:]).astype(BF16)

    def key_rows(row, first):
        return slice(row, row + BLK) if first else slice(row - BLK, row + BLK)

    blocks = []
    for j in range(SEQ // BLK):
        sink = functools.partial(store_state, st4, (j % 4) * n4 + j // 4, 4)
        blocks.append((2, q16_ref.at[0, j], k16_ref.at[0, j], v16_ref.at[0, j], 0, True, None, 0, sink))
    for n in range(n4 // BLK):
        for r in range(4):
            sink = functools.partial(store_state, st1, r + 4 * BLK * n, 4)
            blocks.append((1, q4_ref.at[0, r], k4_ref.at[0, r], v4_ref.at[0, r], n * BLK, n == 0,
                           st4, r * n4 + n * BLK, sink))
    for n in range(SEQ // BLK):
        blocks.append((0, q1_ref.at[0], k1_ref.at[0], v1_ref.at[0], n * BLK, n == 0, st1, n * BLK,
                       functools.partial(finish, n * BLK)))

    def logits(blk):
        br, q_src, k_src, _, row, first, _, _, _ = blk
        q = q_src[row:row + BLK, :]
        zero = jnp.zeros_like(q)
        qst = jnp.concatenate([jnp.where(head_a, q, zero), jnp.where(head_a, zero, q)], axis=0)
        tab = tab_ref[0, br, :, BLK:] if first else tab_ref[0, br]
        return _nt_dot(qst, k_src[key_rows(row, first), :]) + tab

    def update(blk, s):
        _, _, _, v_src, row, first, st_src, st_row, sink = blk
        state = None if st_src is None else load_state(st_src, st_row)
        sink(_softmax_update(s, v_src[key_rows(row, first), :], state))

    pending = []
    for blk in blocks:
        pending.append((blk, logits(blk)))
        if len(pending) > LOOKAHEAD:
            update(*pending.pop(0))
    for item in pending:
        update(*item)


def _dilated_attention(qkv1, qkv4, qkv16, tables):
    bsz = qkv1[0].shape[0]
    pair = pl.BlockSpec((1, SEQ, LANES), lambda p, b: (b, 0, p))
    pair_res = lambda d: pl.BlockSpec((1, d, SEQ // d, LANES), lambda p, b: (b, 0, 0, p))
    state = pltpu.VMEM((3, 2, SEQ, LANES), F32)
    return pl.pallas_call(
        _dil_kernel,
        grid=(N_PAIRS, bsz),
        in_specs=[pair] * 3 + [pair_res(4)] * 3 + [pair_res(16)] * 3 + [
            pl.BlockSpec((1, 3, 2 * BLK, 2 * BLK), lambda p, b: (p, 0, 0, 0))],
        out_specs=pair,
        out_shape=jax.ShapeDtypeStruct((bsz, SEQ, D_A), BF16),
        scratch_shapes=[state, state],
        compiler_params=pltpu.CompilerParams(
            dimension_semantics=("parallel", "parallel"), vmem_limit_bytes=VMEM_LIMIT),
        name="dilated_attn",
    )(*qkv1, *qkv4, *qkv16, tables)


def _mla_kernel(qn_ref, qr_ref, kn_ref, kr_ref, v_ref, o_ref):
    g = pl.program_id(1)
    qb = QB_MLA
    kc = KC_MLA
    lane = lax.broadcasted_iota(jnp.int32, (qb, LANES), 1)
    head_a = lane < B_NOPE_DIM
    slot = lane // B_ROPE_DIM

    def stacked_q(h, p):
        rows = slice(h * qb, (h + 1) * qb)
        qn = qn_ref[0, rows, p * LANES:(p + 1) * LANES]
        qr = qr_ref[0, rows, (p // 2) * LANES:(p // 2 + 1) * LANES]
        zero = jnp.zeros_like(qn)
        sa = 2 * (p % 2)
        q_a = jnp.concatenate([jnp.where(head_a, qn, zero), jnp.where(slot == sa, qr, zero)], axis=1)
        q_b = jnp.concatenate([jnp.where(head_a, zero, qn), jnp.where(slot == sa + 1, qr, zero)], axis=1)
        return jnp.concatenate([q_a, q_b], axis=0)

    qst = {(h, p): stacked_q(h, p) for h in range(QBS_PER_STEP) for p in range(N_PAIRS)}

    def logits(chain, keys, diagonal):
        cols = slice(chain[1] * LANES, (chain[1] + 1) * LANES)
        s = _nt_dot(qst[chain], jnp.concatenate([kn_ref[0, keys, cols], kr_ref[0, keys, :]], axis=1))
        if diagonal:
            qpos = lax.broadcasted_iota(jnp.int32, (2 * qb, qb), 0) % qb
            kpos = lax.broadcasted_iota(jnp.int32, (2 * qb, qb), 1)
            tail = jnp.where(qpos >= kpos, s[:, -qb:], NEG)
            s = tail if s.shape[1] == qb else jnp.concatenate([s[:, :-qb], tail], axis=1)
        return s

    def update(chain, keys, s, state):
        nk = s.shape[1]
        p = chain[1]
        m_cur = jnp.max(s, axis=1, keepdims=True)
        m_new = jnp.broadcast_to(m_cur, (2 * qb, LANES)) if state is None else jnp.maximum(state[0], m_cur)
        pr = jnp.exp2(s - jnp.concatenate([m_new] * (nk // LANES), axis=1))
        v_ones = jnp.concatenate([v_ref[0, keys, p * LANES:(p + 1) * LANES], jnp.ones((nk, LANES), BF16)],
                                 axis=1)
        pv = jnp.dot(pr.astype(BF16), v_ones, preferred_element_type=F32)
        acc, l_new = pv[:, :LANES], pv[:, LANES:]
        if state is None:
            return m_new, l_new, acc
        alpha = jnp.exp2(state[0] - m_new)
        return m_new, alpha * state[1] + l_new, alpha * state[2] + acc

    def run(step):
        ends = [(step * QBS_PER_STEP + h + 1) * qb for h in range(QBS_PER_STEP)]
        steps = []
        for lo in range(0, max(ends), kc):
            for h, end in enumerate(ends):
                if lo < end:
                    steps += [((h, p), slice(lo, min(lo + kc, end)), end) for p in range(N_PAIRS)]
        states = {chain: None for chain in qst}
        pending = []
        for chain, keys, end in steps:
            pending.append((chain, keys, logits(chain, keys, keys.stop == end)))
            if len(pending) > LOOKAHEAD_MLA:
                c0, k0, s0 = pending.pop(0)
                states[c0] = update(c0, k0, s0, states[c0])
        for c0, k0, s0 in pending:
            states[c0] = update(c0, k0, s0, states[c0])
        for (h, p), (_, l, acc) in states.items():
            o = acc / l
            o_ref[0, h * qb:(h + 1) * qb, p * LANES:(p + 1) * LANES] = jnp.where(
                head_a, o[:qb], o[qb:]).astype(BF16)

    for step in range(SEQ // (qb * QBS_PER_STEP)):
        pl.when(g == step)(functools.partial(run, step))


def _latent_attention(qn, qr, kn, kr, vb):
    bsz = qn.shape[0]
    rows = QB_MLA * QBS_PER_STEP
    blk = lambda r, w: pl.BlockSpec((1, r, w), lambda b, i: (b, i if r == rows else 0, 0))
    return pl.pallas_call(
        _mla_kernel,
        grid=(bsz, SEQ // rows),
        in_specs=[blk(rows, D_B), blk(rows, B_HEADS * B_ROPE_DIM), blk(SEQ, D_B), blk(SEQ, LANES),
                  blk(SEQ, D_B)],
        out_specs=blk(rows, D_B),
        out_shape=jax.ShapeDtypeStruct((bsz, SEQ, D_B), BF16),
        compiler_params=pltpu.CompilerParams(
            dimension_semantics=("parallel", "parallel"), vmem_limit_bytes=VMEM_LIMIT),
        name="latent_attn",
    )(qn, qr, kn, kr, vb)


def _post_kernel(x_ref, oa_ref, ob_ref, g1_ref, sh2_ref, sc2_ref, g2_ref, goa_ref, gob_ref,
                 wout_ref, gn2_ref, wfi_ref, wfo_ref, gfin_ref, o_ref, a_scr):
    chains = [slice(t * SUB_POST, (t + 1) * SUB_POST) for t in range(TM_POST // SUB_POST)]

    def mixed(rows):
        ya = _rms(oa_ref[0, rows, :].astype(F32), goa_ref[...])
        yb = _rms(ob_ref[0, rows, :].astype(F32), gob_ref[...])
        y = jnp.concatenate([ya, yb], axis=1).astype(BF16)
        return jnp.dot(y, wout_ref[...], preferred_element_type=F32)

    def ffn(rows, x1):
        h2 = (_rms(x1, gn2_ref[...]) * (1.0 + sc2_ref[0]) + sh2_ref[0]).astype(BF16)
        for c in range(D_FF // FF_CHUNK):
            lo = c * FF_CHUNK
            g = jnp.dot(h2, wfi_ref[:, lo:lo + FF_CHUNK], preferred_element_type=F32)
            u = jnp.dot(h2, wfi_ref[:, D_FF + lo:D_FF + lo + FF_CHUNK], preferred_element_type=F32)
            a_scr[rows, lo:lo + FF_CHUNK] = (jax.nn.silu(g) * u).astype(BF16)
        return jnp.dot(a_scr[rows, :], wfo_ref[...], preferred_element_type=F32)

    mix = [mixed(rows) for rows in chains]
    x1 = [x_ref[0, rows, :] + g1_ref[0] * m for rows, m in zip(chains, mix)]
    f = [ffn(rows, x) for rows, x in zip(chains, x1)]
    for rows, x, y in zip(chains, x1, f):
        o_ref[0, rows, :] = _rms(x + g2_ref[0] * y, gfin_ref[...])


def _post(x, oa, ob, g1m, sh2, sc2, g2m, goa, gob, wout, gn2, wfi, wfo, gfin):
    bsz = x.shape[0]
    tm = TM_POST
    tok = lambda w: pl.BlockSpec((1, tm, w), lambda b, j: (b, j, 0))
    per_b = pl.BlockSpec((1, 1, D_MODEL), lambda b, j: (b, 0, 0))
    return pl.pallas_call(
        _post_kernel,
        grid=(bsz, SEQ // tm),
        in_specs=[tok(D_MODEL), tok(D_A), tok(D_B), per_b, per_b, per_b, per_b,
                  _const_spec((1, D_A)), _const_spec((1, D_B)), _const_spec(wout.shape),
                  _const_spec((1, D_MODEL)), _const_spec(wfi.shape), _const_spec(wfo.shape),
                  _const_spec((1, D_MODEL))],
        out_specs=tok(D_MODEL),
        out_shape=jax.ShapeDtypeStruct((bsz, SEQ, D_MODEL), F32),
        scratch_shapes=[pltpu.VMEM((tm, D_FF), BF16)],
        compiler_params=pltpu.CompilerParams(
            dimension_semantics=("parallel", "parallel"), vmem_limit_bytes=VMEM_LIMIT),
        name="out_proj_ffn",
    )(x, oa, ob, g1m, sh2, sc2, g2m, goa, gob, wout, gn2, wfi, wfo, gfin)


def _rotate_half_cols(w):
    lead = w.shape[:-1]
    half = B_ROPE_DIM // 2
    w = w.reshape(lead + (-1, 2, half))
    return jnp.stack([-w[..., 1, :], w[..., 0, :]], axis=-2).reshape(lead + (-1,))


def _t5_bucket(dist):
    max_exact = N_BUCKETS // 2
    d = np.maximum(dist, 1).astype(np.float64)
    large = max_exact + (np.log(d / max_exact) / np.log(MAX_DISTANCE / max_exact)
                         * (N_BUCKETS - max_exact)).astype(np.int64)
    large = np.minimum(large, N_BUCKETS - 1)
    return np.where(dist < max_exact, dist, large).astype(np.int32)


def _bias_kernel(bucket_ref, rb_ref, o_ref):
    p = pl.program_id(0)
    for br in range(len(DILATIONS)):
        bucket = bucket_ref[br]
        for hh in range(2):
            tab = jnp.full((BLK, 2 * BLK), NEG, F32)
            for k in range(N_BUCKETS):
                tab = jnp.where(bucket == k, rb_ref[k, 2 * p + hh] * LOG2E, tab)
            o_ref[0, br, hh * BLK:(hh + 1) * BLK, :] = tab


def _bias_tables(rel_bias):
    a = np.arange(BLK)[:, None]
    bk = np.arange(2 * BLK)[None, :]
    steps = BLK + a - bk
    valid = (steps >= 0) & (steps <= BLK)
    bucket = np.stack([np.where(valid, _t5_bucket(np.clip(steps, 0, BLK) * dil), -1)
                       for dil in DILATIONS]).astype(np.int32)
    nbr = len(DILATIONS)
    return pl.pallas_call(
        _bias_kernel,
        grid=(N_PAIRS,),
        in_specs=[pl.BlockSpec((nbr, BLK, 2 * BLK), lambda p: (0, 0, 0)),
                  pl.BlockSpec(memory_space=pltpu.SMEM)],
        out_specs=pl.BlockSpec((1, nbr, 2 * BLK, 2 * BLK), lambda p: (p, 0, 0, 0)),
        out_shape=jax.ShapeDtypeStruct((N_PAIRS, nbr, 2 * BLK, 2 * BLK), F32),
        name="bias_tables",
    )(jnp.asarray(bucket), rel_bias)


def _rope_tables():
    half = B_ROPE_DIM // 2
    inv = ROPE_THETA ** (-jnp.arange(half, dtype=F32) / half)
    ang = jnp.arange(SEQ, dtype=F32)[:, None] * inv[None, :]
    reps = LANES // half
    return jnp.tile(jnp.cos(ang), (1, reps)), jnp.tile(jnp.sin(ang), (1, reps))


def kernel(x, c, w_ada, b_ada, g_norm1, w_in, g_cq, w_uq, g_ckv, w_ukv, rel_bias, g_out_a, g_out_b,
           w_out, g_norm2, w_ffn_in, w_ffn_out, g_final):
    bsz = x.shape[0]
    assert x.shape == (bsz, SEQ, D_MODEL) and w_ada.shape[0] == 1
    l = 0

    mod = _modulation(c, w_ada[l], b_ada[l]).reshape(bsz, N_MOD, 1, D_MODEL)
    sh1, sc1, g1m, sh2, sc2, g2m = (mod[:, i] for i in range(N_MOD))

    wi = w_in[l]
    kr_cols = wi[:, 3 * D_A + Q_LORA + KV_LORA:]
    win = jnp.concatenate([wi[:, :D_A] * (A_HEAD_DIM ** -0.5 * LOG2E), wi[:, D_A:3 * D_A + Q_LORA + KV_LORA],
                           kr_cols, _rotate_half_cols(kr_cols),
                           jnp.zeros((D_MODEL, LANES - 2 * B_ROPE_DIM), F32)], axis=1).astype(BF16)
    wq = (w_uq[l] * ((B_NOPE_DIM + B_ROPE_DIM) ** -0.5 * LOG2E)).reshape(
        Q_LORA, B_HEADS, B_NOPE_DIM + B_ROPE_DIM)
    wq_rope = wq[:, :, B_NOPE_DIM:].reshape(Q_LORA, -1)
    wuq = jnp.concatenate([wq[:, :, :B_NOPE_DIM].reshape(Q_LORA, -1), wq_rope,
                           _rotate_half_cols(wq_rope)], axis=1).astype(BF16)
    wkv = w_ukv[l].reshape(KV_LORA, B_HEADS, B_NOPE_DIM + B_V_DIM)
    wukv = jnp.concatenate([wkv[:, :, :B_NOPE_DIM].reshape(KV_LORA, -1),
                            wkv[:, :, B_NOPE_DIM:].reshape(KV_LORA, -1)], axis=1).astype(BF16)
    cos_t, sin_t = _rope_tables()

    proj = _projections(
        x, sh1, sc1, g_norm1[l].reshape(1, -1), cos_t, sin_t, win, g_cq[l].reshape(1, -1), wuq,
        g_ckv[l].reshape(1, -1), wukv)
    qn, qr, kn, kr, vb = proj[3:8]

    out_a = _dilated_attention(proj[0:3], proj[8:11], proj[11:14], _bias_tables(rel_bias))
    out_b = _latent_attention(qn, qr, kn, kr, vb)

    return _post(x, out_a, out_b, g1m, sh2, sc2, g2m, g_out_a[l].reshape(1, -1),
                 g_out_b[l].reshape(1, -1), w_out[l].astype(BF16), g_norm2[l].reshape(1, -1),
                 w_ffn_in[l].astype(BF16), w_ffn_out[l].astype(BF16), g_final.reshape(1, -1))
```

```python
import functools

import jax
import jax.numpy as jnp
import numpy as np
from jax import lax
from jax.experimental import pallas as pl
from jax.experimental.pallas import tpu as pltpu

F32 = jnp.float32
BF16 = jnp.bfloat16

D_MODEL = 1024
SEQ = 2048
A_HEADS = 8
A_HEAD_DIM = 64
D_A = A_HEADS * A_HEAD_DIM
DILATIONS = (1, 4, 16)
BLK = 128
B_HEADS = 8
B_NOPE_DIM = 64
B_ROPE_DIM = 32
B_V_DIM = 64
D_B = B_HEADS * B_V_DIM
Q_LORA = 384
KV_LORA = 256
ROPE_THETA = 10000.0
N_BUCKETS = 32
MAX_DISTANCE = 2048
D_FF = 2816
N_MOD = 6
EPS = 1e-6
NEG = -1e30
LOG2E = 1.4426950408889634

LANES = 128
N_PAIRS = A_HEADS // 2
VMEM_LIMIT = 56 * 1024 * 1024

_C_QA, _C_KA, _C_VA = 0, D_A, 2 * D_A
_C_CQ = 3 * D_A
_C_CKV = _C_CQ + Q_LORA
_C_KR = _C_CKV + KV_LORA
P_IN2 = _C_KR + LANES

TM_PROJ = 1024
SUB_PROJ = 512
TM_POST = 1024
SUB_POST = 512
FF_CHUNK = 256
LOOKAHEAD = 2
PAIRS_PER_STEP = 2
LOOKAHEAD_MLA = 1
QB_MLA = 256
KC_MLA = 512


def _nt_dot(a, b):
    return lax.dot_general(a, b, (((1,), (1,)), ((), ())), preferred_element_type=F32)


def _rms(x, g):
    return x * lax.rsqrt(jnp.mean(x * x, axis=-1, keepdims=True) + EPS) * g


def _mod_kernel(c_ref, w_ref, b_ref, o_ref):
    cond = jax.nn.silu(c_ref[...])
    o_ref[...] = jnp.dot(cond.astype(BF16), w_ref[...].astype(BF16),
                         preferred_element_type=F32) + b_ref[...]


def _modulation(c, w_ada, b_ada):
    bsz = c.shape[0]
    n = w_ada.shape[1]
    bn = 1536
    return pl.pallas_call(
        _mod_kernel,
        grid=(n // bn,),
        in_specs=[pl.BlockSpec((bsz, D_MODEL), lambda j: (0, 0)),
                  pl.BlockSpec((D_MODEL, bn), lambda j: (0, j)),
                  pl.BlockSpec((1, bn), lambda j: (0, j))],
        out_specs=pl.BlockSpec((bsz, bn), lambda j: (0, j)),
        out_shape=jax.ShapeDtypeStruct((bsz, n), F32),
        compiler_params=pltpu.CompilerParams(vmem_limit_bytes=VMEM_LIMIT),
        name="adaln_mod",
    )(c, w_ada, b_ada.reshape(1, n))


def _proj_kernel(x_ref, sh_ref, sc_ref, g1_ref, cos_ref, sin_ref, win_ref, gcq_ref, wuq_ref,
                 gckv_ref, wukv_ref,
                 qa_ref, ka_ref, va_ref, qn_ref, qr_ref, kn_ref, kr_ref, vb_ref,
                 qa4_ref, ka4_ref, va4_ref, qa16_ref, ka16_ref, va16_ref, fa_scr, fb_scr):
    for t in range(TM_PROJ // SUB_PROJ):
        rows = slice(t * SUB_PROJ, (t + 1) * SUB_PROJ)
        h = _rms(x_ref[0, rows, :], g1_ref[...]) * (1.0 + sc_ref[0]) + sh_ref[0]
        hb = h.astype(BF16)

        def proj(lo, hi):
            return jnp.dot(hb, win_ref[:, lo:hi], preferred_element_type=F32)

        def emit_orders(res, nat_ref, d4_ref, d16_ref):
            nat_ref[0, rows, :] = res.astype(BF16)
            fa, fb = fa_scr.at[t], fb_scr.at[t]
            n4, n16 = SUB_PROJ // 4, SUB_PROJ // 16
            for g in range(D_A // LANES):
                lanes = slice(g * LANES, (g + 1) * LANES)
                fa[g] = res[:, lanes]
                for r in range(4):
                    x4 = fa[g, pl.ds(r, n4, stride=4), :]
                    fb[g, r * n4:(r + 1) * n4, :] = x4
                    d4_ref[0, r, t * n4:(t + 1) * n4, lanes] = x4.astype(BF16)
                for j in range(16):
                    x16 = fb[g, pl.ds((j % 4) * n4 + j // 4, n16, stride=4), :]
                    d16_ref[0, j, t * n16:(t + 1) * n16, lanes] = x16.astype(BF16)

        emit_orders(proj(_C_QA, _C_KA), qa_ref, qa4_ref, qa16_ref)
        emit_orders(proj(_C_KA, _C_VA), ka_ref, ka4_ref, ka16_ref)
        emit_orders(proj(_C_VA, _C_CQ), va_ref, va4_ref, va16_ref)

        cos = cos_ref[rows, :]
        sin = sin_ref[rows, :]
        kr = proj(_C_KR, P_IN2)
        nr = B_ROPE_DIM
        kr = kr[:, :nr] * cos[:, :nr] + kr[:, nr:2 * nr] * sin[:, :nr]
        kr_ref[0, rows, :] = jnp.concatenate([kr] * (LANES // nr), axis=1).astype(BF16)

        cq = _rms(proj(_C_CQ, _C_CKV), gcq_ref[...]).astype(BF16)
        q = jnp.dot(cq, wuq_ref[...], preferred_element_type=F32)
        qn_ref[0, rows, :] = q[:, :D_B].astype(BF16)
        cos2 = jnp.concatenate([cos, cos], axis=1)
        sin2 = jnp.concatenate([sin, sin], axis=1)
        n_r = B_HEADS * B_ROPE_DIM
        qr_ref[0, rows, :] = (q[:, D_B:D_B + n_r] * cos2 + q[:, D_B + n_r:] * sin2).astype(BF16)

        ckv = _rms(proj(_C_CKV, _C_KR), gckv_ref[...]).astype(BF16)
        kv = jnp.dot(ckv, wukv_ref[...], preferred_element_type=F32)
        kn_ref[0, rows, :] = kv[:, :D_B].astype(BF16)
        vb_ref[0, rows, :] = kv[:, D_B:].astype(BF16)


def _const_spec(shape):
    nd = len(shape)
    return pl.BlockSpec(shape, lambda *_: (0,) * nd, pipeline_mode=pl.Buffered(1))


def _projections(x, sh1, sc1, g1, cos_t, sin_t, win, gcq, wuq, gckv, wukv):
    bsz = x.shape[0]
    tm = TM_PROJ
    tok = lambda w: pl.BlockSpec((1, tm, w), lambda b, j: (b, j, 0))
    per_b = pl.BlockSpec((1, 1, D_MODEL), lambda b, j: (b, 0, 0))
    tab = pl.BlockSpec((tm, LANES), lambda b, j: (j, 0))
    widths = (D_A, D_A, D_A, D_B, B_HEADS * B_ROPE_DIM, D_B, LANES, D_B)
    res_spec = lambda d: pl.BlockSpec((1, d, tm // d, D_A), lambda b, j: (b, 0, j, 0))
    res_shape = lambda d: jax.ShapeDtypeStruct((bsz, d, SEQ // d, D_A), BF16)
    chain_f32 = pltpu.VMEM((tm // SUB_PROJ, D_A // LANES, SUB_PROJ, LANES), F32)
    return pl.pallas_call(
        _proj_kernel,
        grid=(bsz, SEQ // tm),
        in_specs=[tok(D_MODEL), per_b, per_b, _const_spec((1, D_MODEL)), tab, tab,
                  _const_spec(win.shape), _const_spec((1, Q_LORA)), _const_spec(wuq.shape),
                  _const_spec((1, KV_LORA)), _const_spec(wukv.shape)],
        out_specs=[tok(w) for w in widths] + [res_spec(4)] * 3 + [res_spec(16)] * 3,
        out_shape=[jax.ShapeDtypeStruct((bsz, SEQ, w), BF16) for w in widths]
        + [res_shape(4)] * 3 + [res_shape(16)] * 3,
        scratch_shapes=[chain_f32, chain_f32],
        compiler_params=pltpu.CompilerParams(
            dimension_semantics=("parallel", "parallel"), vmem_limit_bytes=VMEM_LIMIT),
        name="in_proj",
    )(x, sh1, sc1, g1, cos_t, sin_t, win, gcq, wuq, gckv, wukv)


def _stack_rows(ref, c, row):
    return jnp.concatenate([ref[c, 0, pl.ds(row, BLK), :], ref[c, 1, pl.ds(row, BLK), :]], axis=0)


def _softmax_update(s, vb, state):
    nk = s.shape[1]
    m_cur = jnp.max(s, axis=1, keepdims=True)
    if state is None:
        m_new = jnp.broadcast_to(m_cur, (2 * BLK, LANES))
    else:
        m0, l0, acc0 = state
        m_new = jnp.maximum(m0, m_cur)
    m_wide = m_new if nk == LANES else jnp.concatenate([m_new] * (nk // LANES), axis=1)
    p = jnp.exp2(s - m_wide)
    v_ones = jnp.concatenate([vb, jnp.ones((nk, LANES), BF16)], axis=1)
    pv = jnp.dot(p.astype(BF16), v_ones, preferred_element_type=F32)
    acc, l_new = pv[:, :LANES], pv[:, LANES:]
    if state is not None:
        alpha = jnp.exp2(m0 - m_new)
        l_new = alpha * l0 + l_new
        acc = alpha * acc0 + acc
    return m_new, l_new, acc


def _dil_kernel(q1_ref, k1_ref, v1_ref, q4_ref, k4_ref, v4_ref, q16_ref, k16_ref, v16_ref, tab_ref,
                o_ref, st4, st1):
    lane = lax.broadcasted_iota(jnp.int32, (BLK, LANES), 1)
    head_a = lane < A_HEAD_DIM
    n4 = SEQ // 4

    def store_state(dst, start, state):
        for c, x in enumerate(state):
            dst[c, 0, pl.ds(start, BLK, stride=4), :] = x[:BLK]
            dst[c, 1, pl.ds(start, BLK, stride=4), :] = x[BLK:]

    def load_state(src, row):
        return tuple(_stack_rows(src, c, row) for c in range(3))

    def finish(rows, lanes, state):
        _, l, acc = state
        o = acc / l
        o_ref[0, rows, lanes] = jnp.where(head_a, o[:BLK], o[BLK:]).astype(BF16)

    def key_rows(row, first):
        return slice(row, row + BLK) if first else slice(row - BLK, row + BLK)

    def pair_blocks(pp):
        lanes = slice(pp * LANES, (pp + 1) * LANES)
        s4, s1 = st4.at[pp], st1.at[pp]
        out = []
        for j in range(SEQ // BLK):
            sink = functools.partial(store_state, s4, (j % 4) * n4 + j // 4)
            out.append((pp, 2, q16_ref.at[0, j], k16_ref.at[0, j], v16_ref.at[0, j], 0, True, None, 0, sink))
        for n in range(n4 // BLK):
            for r in range(4):
                sink = functools.partial(store_state, s1, r + 4 * BLK * n)
                out.append((pp, 1, q4_ref.at[0, r], k4_ref.at[0, r], v4_ref.at[0, r], n * BLK, n == 0,
                            s4, r * n4 + n * BLK, sink))
        for n in range(SEQ // BLK):
            sink = functools.partial(finish, slice(n * BLK, (n + 1) * BLK), lanes)
            out.append((pp, 0, q1_ref.at[0], k1_ref.at[0], v1_ref.at[0], n * BLK, n == 0, s1, n * BLK, sink))
        return out

    def logits(blk):
        pp, br, q_src, k_src, _, row, first, _, _, _ = blk
        lanes = slice(pp * LANES, (pp + 1) * LANES)
        q = q_src[row:row + BLK, lanes]
        zero = jnp.zeros_like(q)
        qst = jnp.concatenate([jnp.where(head_a, q, zero), jnp.where(head_a, zero, q)], axis=0)
        tab = tab_ref[pp, br, :, BLK:] if first else tab_ref[pp, br]
        return _nt_dot(qst, k_src[key_rows(row, first), lanes]) + tab

    def update(blk, s):
        pp, _, _, _, v_src, row, first, st_src, st_row, sink = blk
        lanes = slice(pp * LANES, (pp + 1) * LANES)
        state = None if st_src is None else load_state(st_src, st_row)
        sink(_softmax_update(s, v_src[key_rows(row, first), lanes], state))

    per_branch = SEQ // BLK
    blocks = sorted((blk for pp in range(PAIRS_PER_STEP) for blk in enumerate(pair_blocks(pp))),
                    key=lambda ib: (ib[0] + per_branch * ib[1][0], ib[1][0]))
    pending = []
    for _, blk in blocks:
        pending.append((blk, logits(blk)))
        if len(pending) > LOOKAHEAD:
            update(*pending.pop(0))
    for item in pending:
        update(*item)


def _dilated_attention(qkv1, qkv4, qkv16, tables):
    bsz = qkv1[0].shape[0]
    w = PAIRS_PER_STEP * LANES
    pairs = pl.BlockSpec((1, SEQ, w), lambda p, b: (b, 0, p))
    pairs_res = lambda d: pl.BlockSpec((1, d, SEQ // d, w), lambda p, b: (b, 0, 0, p))
    state = pltpu.VMEM((PAIRS_PER_STEP, 3, 2, SEQ, LANES), F32)
    return pl.pallas_call(
        _dil_kernel,
        grid=(N_PAIRS // PAIRS_PER_STEP, bsz),
        in_specs=[pairs] * 3 + [pairs_res(4)] * 3 + [pairs_res(16)] * 3 + [
            pl.BlockSpec((PAIRS_PER_STEP, 3, 2 * BLK, 2 * BLK), lambda p, b: (p, 0, 0, 0))],
        out_specs=pairs,
        out_shape=jax.ShapeDtypeStruct((bsz, SEQ, D_A), BF16),
        scratch_shapes=[state, state],
        compiler_params=pltpu.CompilerParams(
            dimension_semantics=("parallel", "parallel"), vmem_limit_bytes=VMEM_LIMIT),
        name="dilated_attn",
    )(*qkv1, *qkv4, *qkv16, tables)


def _mla_kernel(qn_ref, qr_ref, kn_ref, kr_ref, v_ref, o_ref):
    qb = QB_MLA
    kc = KC_MLA
    n_qb = SEQ // qb
    lane = lax.broadcasted_iota(jnp.int32, (qb, LANES), 1)
    head_a = lane < B_NOPE_DIM
    slot = lane // B_ROPE_DIM

    def stacked_q(h, p):
        rows = slice(h * qb, (h + 1) * qb)
        qn = qn_ref[0, rows, p * LANES:(p + 1) * LANES]
        qr = qr_ref[0, rows, (p // 2) * LANES:(p // 2 + 1) * LANES]
        zero = jnp.zeros_like(qn)
        sa = 2 * (p % 2)
        q_a = jnp.concatenate([jnp.where(head_a, qn, zero), jnp.where(slot == sa, qr, zero)], axis=1)
        q_b = jnp.concatenate([jnp.where(head_a, zero, qn), jnp.where(slot == sa + 1, qr, zero)], axis=1)
        return jnp.concatenate([q_a, q_b], axis=0)

    qst = {(h, p): stacked_q(h, p) for h in range(n_qb) for p in range(N_PAIRS)}

    def logits(chain, keys, diagonal):
        cols = slice(chain[1] * LANES, (chain[1] + 1) * LANES)
        s = _nt_dot(qst[chain], jnp.concatenate([kn_ref[0, keys, cols], kr_ref[0, keys, :]], axis=1))
        if diagonal:
            qpos = lax.broadcasted_iota(jnp.int32, (2 * qb, qb), 0) % qb
            kpos = lax.broadcasted_iota(jnp.int32, (2 * qb, qb), 1)
            tail = jnp.where(qpos >= kpos, s[:, -qb:], NEG)
            s = tail if s.shape[1] == qb else jnp.concatenate([s[:, :-qb], tail], axis=1)
        return s

    def update(chain, keys, s, state):
        nk = s.shape[1]
        p = chain[1]
        m_cur = jnp.max(s, axis=1, keepdims=True)
        m_new = jnp.broadcast_to(m_cur, (2 * qb, LANES)) if state is None else jnp.maximum(state[0], m_cur)
        pr = jnp.exp2(s - jnp.concatenate([m_new] * (nk // LANES), axis=1))
        v_ones = jnp.concatenate([v_ref[0, keys, p * LANES:(p + 1) * LANES], jnp.ones((nk, LANES), BF16)],
                                 axis=1)
        pv = jnp.dot(pr.astype(BF16), v_ones, preferred_element_type=F32)
        acc, l_new = pv[:, :LANES], pv[:, LANES:]
        if state is None:
            return m_new, l_new, acc
        alpha = jnp.exp2(state[0] - m_new)
        return m_new, alpha * state[1] + l_new, alpha * state[2] + acc

    ends = [(h + 1) * qb for h in range(n_qb)]
    steps = []
    for lo in range(0, SEQ, kc):
        for h, end in enumerate(ends):
            if lo < end:
                steps += [((h, p), slice(lo, min(lo + kc, end)), end) for p in range(N_PAIRS)]
    states = {chain: None for chain in qst}
    pending = []
    for chain, keys, end in steps:
        pending.append((chain, keys, logits(chain, keys, keys.stop == end)))
        if len(pending) > LOOKAHEAD_MLA:
            c0, k0, s0 = pending.pop(0)
            states[c0] = update(c0, k0, s0, states[c0])
    for c0, k0, s0 in pending:
        states[c0] = update(c0, k0, s0, states[c0])
    for (h, p), (_, l, acc) in states.items():
        o = acc / l
        o_ref[0, h * qb:(h + 1) * qb, p * LANES:(p + 1) * LANES] = jnp.where(
            head_a, o[:qb], o[qb:]).astype(BF16)


def _latent_attention(qn, qr, kn, kr, vb):
    bsz = qn.shape[0]
    blk = lambda w: pl.BlockSpec((1, SEQ, w), lambda b: (b, 0, 0))
    return pl.pallas_call(
        _mla_kernel,
        grid=(bsz,),
        in_specs=[blk(D_B), blk(B_HEADS * B_ROPE_DIM), blk(D_B), blk(LANES), blk(D_B)],
        out_specs=blk(D_B),
        out_shape=jax.ShapeDtypeStruct((bsz, SEQ, D_B), BF16),
        compiler_params=pltpu.CompilerParams(
            dimension_semantics=("parallel",), vmem_limit_bytes=VMEM_LIMIT),
        name="latent_attn",
    )(qn, qr, kn, kr, vb)


def _post_kernel(x_ref, oa_ref, ob_ref, g1_ref, sh2_ref, sc2_ref, g2_ref, goa_ref, gob_ref,
                 wout_ref, gn2_ref, wfi_ref, wfo_ref, gfin_ref, o_ref, a_scr):
    chains = [slice(t * SUB_POST, (t + 1) * SUB_POST) for t in range(TM_POST // SUB_POST)]

    def mixed(rows):
        ya = _rms(oa_ref[0, rows, :].astype(F32), goa_ref[...])
        yb = _rms(ob_ref[0, rows, :].astype(F32), gob_ref[...])
        y = jnp.concatenate([ya, yb], axis=1).astype(BF16)
        return jnp.dot(y, wout_ref[...], preferred_element_type=F32)

    def ffn(rows, x1):
        h2 = (_rms(x1, gn2_ref[...]) * (1.0 + sc2_ref[0]) + sh2_ref[0]).astype(BF16)
        for c in range(D_FF // FF_CHUNK):
            lo = c * FF_CHUNK
            g = jnp.dot(h2, wfi_ref[:, lo:lo + FF_CHUNK], preferred_element_type=F32)
            u = jnp.dot(h2, wfi_ref[:, D_FF + lo:D_FF + lo + FF_CHUNK], preferred_element_type=F32)
            a_scr[rows, lo:lo + FF_CHUNK] = (jax.nn.silu(g) * u).astype(BF16)
        return jnp.dot(a_scr[rows, :], wfo_ref[...], preferred_element_type=F32)

    mix = [mixed(rows) for rows in chains]
    x1 = [x_ref[0, rows, :] + g1_ref[0] * m for rows, m in zip(chains, mix)]
    f = [ffn(rows, x) for rows, x in zip(chains, x1)]
    for rows, x, y in zip(chains, x1, f):
        o_ref[0, rows, :] = _rms(x + g2_ref[0] * y, gfin_ref[...])


def _post(x, oa, ob, g1m, sh2, sc2, g2m, goa, gob, wout, gn2, wfi, wfo, gfin):
    bsz = x.shape[0]
    tm = TM_POST
    tok = lambda w: pl.BlockSpec((1, tm, w), lambda b, j: (b, j, 0))
    per_b = pl.BlockSpec((1, 1, D_MODEL), lambda b, j: (b, 0, 0))
    return pl.pallas_call(
        _post_kernel,
        grid=(bsz, SEQ // tm),
        in_specs=[tok(D_MODEL), tok(D_A), tok(D_B), per_b, per_b, per_b, per_b,
                  _const_spec((1, D_A)), _const_spec((1, D_B)), _const_spec(wout.shape),
                  _const_spec((1, D_MODEL)), _const_spec(wfi.shape), _const_spec(wfo.shape),
                  _const_spec((1, D_MODEL))],
        out_specs=tok(D_MODEL),
        out_shape=jax.ShapeDtypeStruct((bsz, SEQ, D_MODEL), F32),
        scratch_shapes=[pltpu.VMEM((tm, D_FF), BF16)],
        compiler_params=pltpu.CompilerParams(
            dimension_semantics=("parallel", "parallel"), vmem_limit_bytes=VMEM_LIMIT),
        name="out_proj_ffn",
    )(x, oa, ob, g1m, sh2, sc2, g2m, goa, gob, wout, gn2, wfi, wfo, gfin)


def _rotate_half_cols(w):
    lead = w.shape[:-1]
    half = B_ROPE_DIM // 2
    w = w.reshape(lead + (-1, 2, half))
    return jnp.stack([-w[..., 1, :], w[..., 0, :]], axis=-2).reshape(lead + (-1,))


def _t5_bucket(dist):
    max_exact = N_BUCKETS // 2
    d = np.maximum(dist, 1).astype(np.float64)
    large = max_exact + (np.log(d / max_exact) / np.log(MAX_DISTANCE / max_exact)
                         * (N_BUCKETS - max_exact)).astype(np.int64)
    large = np.minimum(large, N_BUCKETS - 1)
    return np.where(dist < max_exact, dist, large).astype(np.int32)


def _bias_kernel(bucket_ref, rb_ref, o_ref):
    p = pl.program_id(0)
    for br in range(len(DILATIONS)):
        bucket = bucket_ref[br]
        for hh in range(2):
            tab = jnp.full((BLK, 2 * BLK), NEG, F32)
            for k in range(N_BUCKETS):
                tab = jnp.where(bucket == k, rb_ref[k, 2 * p + hh] * LOG2E, tab)
            o_ref[0, br, hh * BLK:(hh + 1) * BLK, :] = tab


def _bias_tables(rel_bias):
    a = np.arange(BLK)[:, None]
    bk = np.arange(2 * BLK)[None, :]
    steps = BLK + a - bk
    valid = (steps >= 0) & (steps <= BLK)
    bucket = np.stack([np.where(valid, _t5_bucket(np.clip(steps, 0, BLK) * dil), -1)
                       for dil in DILATIONS]).astype(np.int32)
    nbr = len(DILATIONS)
    return pl.pallas_call(
        _bias_kernel,
        grid=(N_PAIRS,),
        in_specs=[pl.BlockSpec((nbr, BLK, 2 * BLK), lambda p: (0, 0, 0)),
                  pl.BlockSpec(memory_space=pltpu.SMEM)],
        out_specs=pl.BlockSpec((1, nbr, 2 * BLK, 2 * BLK), lambda p: (p, 0, 0, 0)),
        out_shape=jax.ShapeDtypeStruct((N_PAIRS, nbr, 2 * BLK, 2 * BLK), F32),
        name="bias_tables",
    )(jnp.asarray(bucket), rel_bias)


def _rope_tables():
    half = B_ROPE_DIM // 2
    inv = ROPE_THETA ** (-jnp.arange(half, dtype=F32) / half)
    ang = jnp.arange(SEQ, dtype=F32)[:, None] * inv[None, :]
    reps = LANES // half
    return jnp.tile(jnp.cos(ang), (1, reps)), jnp.tile(jnp.sin(ang), (1, reps))


def kernel(x, c, w_ada, b_ada, g_norm1, w_in, g_cq, w_uq, g_ckv, w_ukv, rel_bias, g_out_a, g_out_b,
           w_out, g_norm2, w_ffn_in, w_ffn_out, g_final):
    bsz = x.shape[0]
    assert x.shape == (bsz, SEQ, D_MODEL) and w_ada.shape[0] == 1
    l = 0

    mod = _modulation(c, w_ada[l], b_ada[l]).reshape(bsz, N_MOD, 1, D_MODEL)
    sh1, sc1, g1m, sh2, sc2, g2m = (mod[:, i] for i in range(N_MOD))

    wi = w_in[l]
    kr_cols = wi[:, 3 * D_A + Q_LORA + KV_LORA:]
    win = jnp.concatenate([wi[:, :D_A] * (A_HEAD_DIM ** -0.5 * LOG2E), wi[:, D_A:3 * D_A + Q_LORA + KV_LORA],
                           kr_cols, _rotate_half_cols(kr_cols),
                           jnp.zeros((D_MODEL, LANES - 2 * B_ROPE_DIM), F32)], axis=1).astype(BF16)
    wq = (w_uq[l] * ((B_NOPE_DIM + B_ROPE_DIM) ** -0.5 * LOG2E)).reshape(
        Q_LORA, B_HEADS, B_NOPE_DIM + B_ROPE_DIM)
    wq_rope = wq[:, :, B_NOPE_DIM:].reshape(Q_LORA, -1)
    wuq = jnp.concatenate([wq[:, :, :B_NOPE_DIM].reshape(Q_LORA, -1), wq_rope,
                           _rotate_half_cols(wq_rope)], axis=1).astype(BF16)
    wkv = w_ukv[l].reshape(KV_LORA, B_HEADS, B_NOPE_DIM + B_V_DIM)
    wukv = jnp.concatenate([wkv[:, :, :B_NOPE_DIM].reshape(KV_LORA, -1),
                            wkv[:, :, B_NOPE_DIM:].reshape(KV_LORA, -1)], axis=1).astype(BF16)
    cos_t, sin_t = _rope_tables()

    proj = _projections(
        x, sh1, sc1, g_norm1[l].reshape(1, -1), cos_t, sin_t, win, g_cq[l].reshape(1, -1), wuq,
        g_ckv[l].reshape(1, -1), wukv)
    qn, qr, kn, kr, vb = proj[3:8]

    out_a = _dilated_attention(proj[0:3], proj[8:11], proj[11:14], _bias_tables(rel_bias))
    out_b = _latent_attention(qn, qr, kn, kr, vb)

    return _post(x, out_a, out_b, g1m, sh2, sc2, g2m, g_out_a[l].reshape(1, -1),
                 g_out_b[l].reshape(1, -1), w_out[l].astype(BF16), g_norm2[l].reshape(1, -1),
                 w_ffn_in[l].astype(BF16), w_ffn_out[l].astype(BF16), g_final.reshape(1, -1))
```

```python
import functools

import jax
import jax.numpy as jnp
import numpy as np
from jax import lax
from jax.experimental import pallas as pl
from jax.experimental.pallas import tpu as pltpu

F32 = jnp.float32
BF16 = jnp.bfloat16

D_MODEL = 1024
SEQ = 2048
A_HEADS = 8
A_HEAD_DIM = 64
D_A = A_HEADS * A_HEAD_DIM
DILATIONS = (1, 4, 16)
BLK = 128
B_HEADS = 8
B_NOPE_DIM = 64
B_ROPE_DIM = 32
B_V_DIM = 64
D_B = B_HEADS * B_V_DIM
Q_LORA = 384
KV_LORA = 256
ROPE_THETA = 10000.0
N_BUCKETS = 32
MAX_DISTANCE = 2048
D_FF = 2816
N_MOD = 6
EPS = 1e-6
NEG = -1e30
LOG2E = 1.4426950408889634

LANES = 128
N_PAIRS = A_HEADS // 2
VMEM_LIMIT = 56 * 1024 * 1024

_C_QA, _C_KA, _C_VA = 0, D_A, 2 * D_A
_C_CQ = 3 * D_A
_C_CKV = _C_CQ + Q_LORA
_C_KR = _C_CKV + KV_LORA
P_IN2 = _C_KR + LANES

BN_MOD = 1536
TM_PROJ = 1024
SUB_PROJ = 512
TM_POST = 1024
SUB_POST = 512
FF_CHUNK = 256
LOOKAHEAD = 2
PAIRS_PER_STEP = 2
LOOKAHEAD_MLA = 1
QB_MLA = 256
KC_MLA = 512


def _nt_dot(a, b):
    return lax.dot_general(a, b, (((1,), (1,)), ((), ())), preferred_element_type=F32)


def _rms(x, g):
    return x * lax.rsqrt(jnp.mean(x * x, axis=-1, keepdims=True) + EPS) * g


def _mod_kernel(c_ref, w_ref, b_ref, o_ref):
    cond = jax.nn.silu(c_ref[...])
    o_ref[...] = jnp.dot(cond.astype(BF16), w_ref[...].astype(BF16),
                         preferred_element_type=F32) + b_ref[...]


def _modulation(c, w_ada, b_ada):
    bsz = c.shape[0]
    n = w_ada.shape[1]
    bn = BN_MOD
    return pl.pallas_call(
        _mod_kernel,
        grid=(n // bn,),
        in_specs=[pl.BlockSpec((bsz, D_MODEL), lambda j: (0, 0)),
                  pl.BlockSpec((D_MODEL, bn), lambda j: (0, j)),
                  pl.BlockSpec((1, bn), lambda j: (0, j))],
        out_specs=pl.BlockSpec((bsz, bn), lambda j: (0, j)),
        out_shape=jax.ShapeDtypeStruct((bsz, n), F32),
        compiler_params=pltpu.CompilerParams(vmem_limit_bytes=VMEM_LIMIT),
        name="adaln_mod",
    )(c, w_ada, b_ada.reshape(1, n))


def _proj_kernel(x_ref, sh_ref, sc_ref, g1_ref, cos_ref, sin_ref, win_ref, gcq_ref, wuq_ref,
                 gckv_ref, wukv_ref,
                 qa_ref, ka_ref, va_ref, qn_ref, qr_ref, kn_ref, kr_ref, vb_ref,
                 qa4_ref, ka4_ref, va4_ref, qa16_ref, ka16_ref, va16_ref, fa_scr, fb_scr):
    for t in range(TM_PROJ // SUB_PROJ):
        rows = slice(t * SUB_PROJ, (t + 1) * SUB_PROJ)
        h = _rms(x_ref[0, rows, :], g1_ref[...]) * (1.0 + sc_ref[0]) + sh_ref[0]
        hb = h.astype(BF16)

        def proj(lo, hi):
            return jnp.dot(hb, win_ref[:, lo:hi], preferred_element_type=F32)

        def emit_orders(res, nat_ref, d4_ref, d16_ref):
            nat_ref[0, rows, :] = res.astype(BF16)
            fa, fb = fa_scr.at[t], fb_scr.at[t]
            n4, n16 = SUB_PROJ // 4, SUB_PROJ // 16
            for g in range(D_A // LANES):
                lanes = slice(g * LANES, (g + 1) * LANES)
                fa[g] = res[:, lanes]
                for r in range(4):
                    x4 = fa[g, pl.ds(r, n4, stride=4), :]
                    fb[g, r * n4:(r + 1) * n4, :] = x4
                    d4_ref[0, r, t * n4:(t + 1) * n4, lanes] = x4.astype(BF16)
                for j in range(16):
                    x16 = fb[g, pl.ds((j % 4) * n4 + j // 4, n16, stride=4), :]
                    d16_ref[0, j, t * n16:(t + 1) * n16, lanes] = x16.astype(BF16)

        emit_orders(proj(_C_QA, _C_KA), qa_ref, qa4_ref, qa16_ref)
        emit_orders(proj(_C_KA, _C_VA), ka_ref, ka4_ref, ka16_ref)
        emit_orders(proj(_C_VA, _C_CQ), va_ref, va4_ref, va16_ref)

        cos = cos_ref[rows, :]
        sin = sin_ref[rows, :]
        kr = proj(_C_KR, P_IN2)
        nr = B_ROPE_DIM
        kr = kr[:, :nr] * cos[:, :nr] + kr[:, nr:2 * nr] * sin[:, :nr]
        kr_ref[0, rows, :] = jnp.concatenate([kr] * (LANES // nr), axis=1).astype(BF16)

        cq = _rms(proj(_C_CQ, _C_CKV), gcq_ref[...]).astype(BF16)
        q = jnp.dot(cq, wuq_ref[...], preferred_element_type=F32)
        qn_ref[0, rows, :] = q[:, :D_B].astype(BF16)
        cos2 = jnp.concatenate([cos, cos], axis=1)
        sin2 = jnp.concatenate([sin, sin], axis=1)
        n_r = B_HEADS * B_ROPE_DIM
        qr_ref[0, rows, :] = (q[:, D_B:D_B + n_r] * cos2 + q[:, D_B + n_r:] * sin2).astype(BF16)

        ckv = _rms(proj(_C_CKV, _C_KR), gckv_ref[...]).astype(BF16)
        kv = jnp.dot(ckv, wukv_ref[...], preferred_element_type=F32)
        kn_ref[0, rows, :] = kv[:, :D_B].astype(BF16)
        vb_ref[0, rows, :] = kv[:, D_B:].astype(BF16)


def _const_spec(shape):
    nd = len(shape)
    return pl.BlockSpec(shape, lambda *_: (0,) * nd, pipeline_mode=pl.Buffered(1))


def _projections(x, sh1, sc1, g1, cos_t, sin_t, win, gcq, wuq, gckv, wukv):
    bsz = x.shape[0]
    tm = TM_PROJ
    tok = lambda w: pl.BlockSpec((1, tm, w), lambda b, j: (b, j, 0))
    per_b = pl.BlockSpec((1, 1, D_MODEL), lambda b, j: (b, 0, 0))
    tab = pl.BlockSpec((tm, LANES), lambda b, j: (j, 0))
    widths = (D_A, D_A, D_A, D_B, B_HEADS * B_ROPE_DIM, D_B, LANES, D_B)
    res_spec = lambda d: pl.BlockSpec((1, d, tm // d, D_A), lambda b, j: (b, 0, j, 0))
    res_shape = lambda d: jax.ShapeDtypeStruct((bsz, d, SEQ // d, D_A), BF16)
    chain_f32 = pltpu.VMEM((tm // SUB_PROJ, D_A // LANES, SUB_PROJ, LANES), F32)
    return pl.pallas_call(
        _proj_kernel,
        grid=(bsz, SEQ // tm),
        in_specs=[tok(D_MODEL), per_b, per_b, _const_spec((1, D_MODEL)), tab, tab,
                  _const_spec(win.shape), _const_spec((1, Q_LORA)), _const_spec(wuq.shape),
                  _const_spec((1, KV_LORA)), _const_spec(wukv.shape)],
        out_specs=[tok(w) for w in widths] + [res_spec(4)] * 3 + [res_spec(16)] * 3,
        out_shape=[jax.ShapeDtypeStruct((bsz, SEQ, w), BF16) for w in widths]
        + [res_shape(4)] * 3 + [res_shape(16)] * 3,
        scratch_shapes=[chain_f32, chain_f32],
        compiler_params=pltpu.CompilerParams(
            dimension_semantics=("parallel", "parallel"), vmem_limit_bytes=VMEM_LIMIT),
        name="in_proj",
    )(x, sh1, sc1, g1, cos_t, sin_t, win, gcq, wuq, gckv, wukv)


def _stack_rows(ref, c, row):
    return jnp.concatenate([ref[c, 0, pl.ds(row, BLK), :], ref[c, 1, pl.ds(row, BLK), :]], axis=0)


def _softmax_update(s, vb, state):
    nk = s.shape[1]
    m_cur = jnp.max(s, axis=1, keepdims=True)
    if state is None:
        m_new = jnp.broadcast_to(m_cur, (2 * BLK, LANES))
    else:
        m0, l0, acc0 = state
        m_new = jnp.maximum(m0, m_cur)
    m_wide = m_new if nk == LANES else jnp.concatenate([m_new] * (nk // LANES), axis=1)
    p = jnp.exp2(s - m_wide)
    v_ones = jnp.concatenate([vb, jnp.ones((nk, LANES), BF16)], axis=1)
    pv = jnp.dot(p.astype(BF16), v_ones, preferred_element_type=F32)
    acc, l_new = pv[:, :LANES], pv[:, LANES:]
    if state is not None:
        alpha = jnp.exp2(m0 - m_new)
        l_new = alpha * l0 + l_new
        acc = alpha * acc0 + acc
    return m_new, l_new, acc


def _dil_kernel(q1_ref, k1_ref, v1_ref, q4_ref, k4_ref, v4_ref, q16_ref, k16_ref, v16_ref, tab_ref,
                o_ref, st4, st1):
    lane = lax.broadcasted_iota(jnp.int32, (BLK, LANES), 1)
    head_a = lane < A_HEAD_DIM
    n4 = SEQ // 4

    def store_state(dst, start, state):
        for c, x in enumerate(state):
            dst[c, 0, pl.ds(start, BLK, stride=4), :] = x[:BLK]
            dst[c, 1, pl.ds(start, BLK, stride=4), :] = x[BLK:]

    def load_state(src, row):
        return tuple(_stack_rows(src, c, row) for c in range(3))

    def finish(rows, lanes, state):
        _, l, acc = state
        o = acc / l
        o_ref[0, rows, lanes] = jnp.where(head_a, o[:BLK], o[BLK:]).astype(BF16)

    def key_rows(row, first):
        return slice(row, row + BLK) if first else slice(row - BLK, row + BLK)

    def pair_blocks(pp):
        lanes = slice(pp * LANES, (pp + 1) * LANES)
        s4, s1 = st4.at[pp], st1.at[pp]
        out = []
        for j in range(SEQ // BLK):
            sink = functools.partial(store_state, s4, (j % 4) * n4 + j // 4)
            out.append((pp, 2, q16_ref.at[0, j], k16_ref.at[0, j], v16_ref.at[0, j], 0, True, None, 0, sink))
        for n in range(n4 // BLK):
            for r in range(4):
                sink = functools.partial(store_state, s1, r + 4 * BLK * n)
                out.append((pp, 1, q4_ref.at[0, r], k4_ref.at[0, r], v4_ref.at[0, r], n * BLK, n == 0,
                            s4, r * n4 + n * BLK, sink))
        for n in range(SEQ // BLK):
            sink = functools.partial(finish, slice(n * BLK, (n + 1) * BLK), lanes)
            out.append((pp, 0, q1_ref.at[0], k1_ref.at[0], v1_ref.at[0], n * BLK, n == 0, s1, n * BLK, sink))
        return out

    def logits(blk):
        pp, br, q_src, k_src, _, row, first, _, _, _ = blk
        lanes = slice(pp * LANES, (pp + 1) * LANES)
        q = q_src[row:row + BLK, lanes]
        zero = jnp.zeros_like(q)
        qst = jnp.concatenate([jnp.where(head_a, q, zero), jnp.where(head_a, zero, q)], axis=0)
        tab = tab_ref[pp, br, :, BLK:] if first else tab_ref[pp, br]
        return _nt_dot(qst, k_src[key_rows(row, first), lanes]) + tab

    def update(blk, s):
        pp, _, _, _, v_src, row, first, st_src, st_row, sink = blk
        lanes = slice(pp * LANES, (pp + 1) * LANES)
        state = None if st_src is None else load_state(st_src, st_row)
        sink(_softmax_update(s, v_src[key_rows(row, first), lanes], state))

    per_branch = SEQ // BLK
    blocks = sorted((blk for pp in range(PAIRS_PER_STEP) for blk in enumerate(pair_blocks(pp))),
                    key=lambda ib: (ib[0] + per_branch * ib[1][0], ib[1][0]))
    pending = []
    for _, blk in blocks:
        pending.append((blk, logits(blk)))
        if len(pending) > LOOKAHEAD:
            update(*pending.pop(0))
    for item in pending:
        update(*item)


def _dilated_attention(qkv1, qkv4, qkv16, tables):
    bsz = qkv1[0].shape[0]
    w = PAIRS_PER_STEP * LANES
    pairs = pl.BlockSpec((1, SEQ, w), lambda p, b: (b, 0, p))
    pairs_res = lambda d: pl.BlockSpec((1, d, SEQ // d, w), lambda p, b: (b, 0, 0, p))
    state = pltpu.VMEM((PAIRS_PER_STEP, 3, 2, SEQ, LANES), F32)
    return pl.pallas_call(
        _dil_kernel,
        grid=(N_PAIRS // PAIRS_PER_STEP, bsz),
        in_specs=[pairs] * 3 + [pairs_res(4)] * 3 + [pairs_res(16)] * 3 + [
            pl.BlockSpec((PAIRS_PER_STEP, 3, 2 * BLK, 2 * BLK), lambda p, b: (p, 0, 0, 0))],
        out_specs=pairs,
        out_shape=jax.ShapeDtypeStruct((bsz, SEQ, D_A), BF16),
        scratch_shapes=[state, state],
        compiler_params=pltpu.CompilerParams(
            dimension_semantics=("parallel", "parallel"), vmem_limit_bytes=VMEM_LIMIT),
        name="dilated_attn",
    )(*qkv1, *qkv4, *qkv16, tables)


def _mla_kernel(qn_ref, qr_ref, kn_ref, kr_ref, v_ref, o_ref):
    qb = QB_MLA
    kc = KC_MLA
    n_qb = SEQ // qb
    lane = lax.broadcasted_iota(jnp.int32, (qb, LANES), 1)
    head_a = lane < B_NOPE_DIM
    slot = lane // B_ROPE_DIM

    def stacked_q(h, p):
        rows = slice(h * qb, (h + 1) * qb)
        qn = qn_ref[0, rows, p * LANES:(p + 1) * LANES]
        qr = qr_ref[0, rows, (p // 2) * LANES:(p // 2 + 1) * LANES]
        zero = jnp.zeros_like(qn)
        sa = 2 * (p % 2)
        q_a = jnp.concatenate([jnp.where(head_a, qn, zero), jnp.where(slot == sa, qr, zero)], axis=1)
        q_b = jnp.concatenate([jnp.where(head_a, zero, qn), jnp.where(slot == sa + 1, qr, zero)], axis=1)
        return jnp.concatenate([q_a, q_b], axis=0)

    qst = {(h, p): stacked_q(h, p) for h in range(n_qb) for p in range(N_PAIRS)}

    def logits(chain, keys, diagonal):
        cols = slice(chain[1] * LANES, (chain[1] + 1) * LANES)
        s = _nt_dot(qst[chain], jnp.concatenate([kn_ref[0, keys, cols], kr_ref[0, keys, :]], axis=1))
        if diagonal:
            qpos = lax.broadcasted_iota(jnp.int32, (2 * qb, qb), 0) % qb
            kpos = lax.broadcasted_iota(jnp.int32, (2 * qb, qb), 1)
            tail = jnp.where(qpos >= kpos, s[:, -qb:], NEG)
            s = tail if s.shape[1] == qb else jnp.concatenate([s[:, :-qb], tail], axis=1)
        return s

    def update(chain, keys, s, state):
        nk = s.shape[1]
        p = chain[1]
        m_cur = jnp.max(s, axis=1, keepdims=True)
        m_new = jnp.broadcast_to(m_cur, (2 * qb, LANES)) if state is None else jnp.maximum(state[0], m_cur)
        pr = jnp.exp2(s - jnp.concatenate([m_new] * (nk // LANES), axis=1))
        v_ones = jnp.concatenate([v_ref[0, keys, p * LANES:(p + 1) * LANES], jnp.ones((nk, LANES), BF16)],
                                 axis=1)
        pv = jnp.dot(pr.astype(BF16), v_ones, preferred_element_type=F32)
        acc, l_new = pv[:, :LANES], pv[:, LANES:]
        if state is None:
            return m_new, l_new, acc
        alpha = jnp.exp2(state[0] - m_new)
        return m_new, alpha * state[1] + l_new, alpha * state[2] + acc

    ends = [(h + 1) * qb for h in range(n_qb)]
    steps = []
    for lo in range(0, SEQ, kc):
        for h, end in enumerate(ends):
            if lo < end:
                steps += [((h, p), slice(lo, min(lo + kc, end)), end) for p in range(N_PAIRS)]
    states = {chain: None for chain in qst}
    pending = []
    for chain, keys, end in steps:
        pending.append((chain, keys, logits(chain, keys, keys.stop == end)))
        if len(pending) > LOOKAHEAD_MLA:
            c0, k0, s0 = pending.pop(0)
            states[c0] = update(c0, k0, s0, states[c0])
    for c0, k0, s0 in pending:
        states[c0] = update(c0, k0, s0, states[c0])
    for (h, p), (_, l, acc) in states.items():
        o = acc / l
        o_ref[0, h * qb:(h + 1) * qb, p * LANES:(p + 1) * LANES] = jnp.where(
            head_a, o[:qb], o[qb:]).astype(BF16)


def _latent_attention(qn, qr, kn, kr, vb):
    bsz = qn.shape[0]
    blk = lambda w: pl.BlockSpec((1, SEQ, w), lambda b: (b, 0, 0))
    return pl.pallas_call(
        _mla_kernel,
        grid=(bsz,),
        in_specs=[blk(D_B), blk(B_HEADS * B_ROPE_DIM), blk(D_B), blk(LANES), blk(D_B)],
        out_specs=blk(D_B),
        out_shape=jax.ShapeDtypeStruct((bsz, SEQ, D_B), BF16),
        compiler_params=pltpu.CompilerParams(
            dimension_semantics=("parallel",), vmem_limit_bytes=VMEM_LIMIT),
        name="latent_attn",
    )(qn, qr, kn, kr, vb)


def _post_kernel(x_ref, oa_ref, ob_ref, g1_ref, sh2_ref, sc2_ref, g2_ref, goa_ref, gob_ref,
                 wout_ref, gn2_ref, wfi_ref, wfo_ref, gfin_ref, o_ref, a_scr):
    chains = [slice(t * SUB_POST, (t + 1) * SUB_POST) for t in range(TM_POST // SUB_POST)]

    def mixed(rows):
        ya = _rms(oa_ref[0, rows, :].astype(F32), goa_ref[...])
        yb = _rms(ob_ref[0, rows, :].astype(F32), gob_ref[...])
        y = jnp.concatenate([ya, yb], axis=1).astype(BF16)
        return jnp.dot(y, wout_ref[...], preferred_element_type=F32)

    def ffn(rows, x1):
        h2 = (_rms(x1, gn2_ref[...]) * (1.0 + sc2_ref[0]) + sh2_ref[0]).astype(BF16)
        for c in range(D_FF // FF_CHUNK):
            lo = c * FF_CHUNK
            g = jnp.dot(h2, wfi_ref[:, lo:lo + FF_CHUNK], preferred_element_type=F32)
            u = jnp.dot(h2, wfi_ref[:, D_FF + lo:D_FF + lo + FF_CHUNK], preferred_element_type=F32)
            a_scr[rows, lo:lo + FF_CHUNK] = (jax.nn.silu(g) * u).astype(BF16)
        return jnp.dot(a_scr[rows, :], wfo_ref[...], preferred_element_type=F32)

    mix = [mixed(rows) for rows in chains]
    x1 = [x_ref[0, rows, :] + g1_ref[0] * m for rows, m in zip(chains, mix)]
    f = [ffn(rows, x) for rows, x in zip(chains, x1)]
    for rows, x, y in zip(chains, x1, f):
        o_ref[0, rows, :] = _rms(x + g2_ref[0] * y, gfin_ref[...])


def _post(x, oa, ob, g1m, sh2, sc2, g2m, goa, gob, wout, gn2, wfi, wfo, gfin):
    bsz = x.shape[0]
    tm = TM_POST
    tok = lambda w: pl.BlockSpec((1, tm, w), lambda b, j: (b, j, 0))
    per_b = pl.BlockSpec((1, 1, D_MODEL), lambda b, j: (b, 0, 0))
    return pl.pallas_call(
        _post_kernel,
        grid=(bsz, SEQ // tm),
        in_specs=[tok(D_MODEL), tok(D_A), tok(D_B), per_b, per_b, per_b, per_b,
                  _const_spec((1, D_A)), _const_spec((1, D_B)), _const_spec(wout.shape),
                  _const_spec((1, D_MODEL)), _const_spec(wfi.shape), _const_spec(wfo.shape),
                  _const_spec((1, D_MODEL))],
        out_specs=tok(D_MODEL),
        out_shape=jax.ShapeDtypeStruct((bsz, SEQ, D_MODEL), F32),
        scratch_shapes=[pltpu.VMEM((tm, D_FF), BF16)],
        compiler_params=pltpu.CompilerParams(
            dimension_semantics=("parallel", "parallel"), vmem_limit_bytes=VMEM_LIMIT),
        name="out_proj_ffn",
    )(x, oa, ob, g1m, sh2, sc2, g2m, goa, gob, wout, gn2, wfi, wfo, gfin)


def _rotate_half_cols(w):
    lead = w.shape[:-1]
    half = B_ROPE_DIM // 2
    w = w.reshape(lead + (-1, 2, half))
    return jnp.stack([-w[..., 1, :], w[..., 0, :]], axis=-2).reshape(lead + (-1,))


def _t5_bucket(dist):
    max_exact = N_BUCKETS // 2
    d = np.maximum(dist, 1).astype(np.float64)
    large = max_exact + (np.log(d / max_exact) / np.log(MAX_DISTANCE / max_exact)
                         * (N_BUCKETS - max_exact)).astype(np.int64)
    large = np.minimum(large, N_BUCKETS - 1)
    return np.where(dist < max_exact, dist, large).astype(np.int32)


def _bias_kernel(bucket_ref, rb_ref, o_ref):
    p = pl.program_id(0)
    for br in range(len(DILATIONS)):
        bucket = bucket_ref[br]
        tabs = [jnp.full((BLK, 2 * BLK), NEG, F32)] * 2
        for k in range(N_BUCKETS):
            hit = bucket == k
            tabs = [jnp.where(hit, rb_ref[k, 2 * p + hh] * LOG2E, tabs[hh]) for hh in range(2)]
        for hh in range(2):
            o_ref[0, br, hh * BLK:(hh + 1) * BLK, :] = tabs[hh]


def _bias_tables(rel_bias):
    a = np.arange(BLK)[:, None]
    bk = np.arange(2 * BLK)[None, :]
    steps = BLK + a - bk
    valid = (steps >= 0) & (steps <= BLK)
    bucket = np.stack([np.where(valid, _t5_bucket(np.clip(steps, 0, BLK) * dil), -1)
                       for dil in DILATIONS]).astype(np.int32)
    nbr = len(DILATIONS)
    return pl.pallas_call(
        _bias_kernel,
        grid=(N_PAIRS,),
        in_specs=[pl.BlockSpec((nbr, BLK, 2 * BLK), lambda p: (0, 0, 0)),
                  pl.BlockSpec(memory_space=pltpu.SMEM)],
        out_specs=pl.BlockSpec((1, nbr, 2 * BLK, 2 * BLK), lambda p: (p, 0, 0, 0)),
        out_shape=jax.ShapeDtypeStruct((N_PAIRS, nbr, 2 * BLK, 2 * BLK), F32),
        name="bias_tables",
    )(jnp.asarray(bucket), rel_bias)


def _rope_tables():
    half = B_ROPE_DIM // 2
    inv = ROPE_THETA ** (-jnp.arange(half, dtype=F32) / half)
    ang = jnp.arange(SEQ, dtype=F32)[:, None] * inv[None, :]
    reps = LANES // half
    return jnp.tile(jnp.cos(ang), (1, reps)), jnp.tile(jnp.sin(ang), (1, reps))


def kernel(x, c, w_ada, b_ada, g_norm1, w_in, g_cq, w_uq, g_ckv, w_ukv, rel_bias, g_out_a, g_out_b,
           w_out, g_norm2, w_ffn_in, w_ffn_out, g_final):
    bsz = x.shape[0]
    assert x.shape == (bsz, SEQ, D_MODEL) and w_ada.shape[0] == 1
    l = 0

    mod = _modulation(c, w_ada[l], b_ada[l]).reshape(bsz, N_MOD, 1, D_MODEL)
    sh1, sc1, g1m, sh2, sc2, g2m = (mod[:, i] for i in range(N_MOD))

    wi = w_in[l]
    kr_cols = wi[:, 3 * D_A + Q_LORA + KV_LORA:]
    win = jnp.concatenate([wi[:, :D_A] * (A_HEAD_DIM ** -0.5 * LOG2E), wi[:, D_A:3 * D_A + Q_LORA + KV_LORA],
                           kr_cols, _rotate_half_cols(kr_cols),
                           jnp.zeros((D_MODEL, LANES - 2 * B_ROPE_DIM), F32)], axis=1).astype(BF16)
    wq = (w_uq[l] * ((B_NOPE_DIM + B_ROPE_DIM) ** -0.5 * LOG2E)).reshape(
        Q_LORA, B_HEADS, B_NOPE_DIM + B_ROPE_DIM)
    wq_rope = wq[:, :, B_NOPE_DIM:].reshape(Q_LORA, -1)
    wuq = jnp.concatenate([wq[:, :, :B_NOPE_DIM].reshape(Q_LORA, -1), wq_rope,
                           _rotate_half_cols(wq_rope)], axis=1).astype(BF16)
    wkv = w_ukv[l].reshape(KV_LORA, B_HEADS, B_NOPE_DIM + B_V_DIM)
    wukv = jnp.concatenate([wkv[:, :, :B_NOPE_DIM].reshape(KV_LORA, -1),
                            wkv[:, :, B_NOPE_DIM:].reshape(KV_LORA, -1)], axis=1).astype(BF16)
    cos_t, sin_t = _rope_tables()

    proj = _projections(
        x, sh1, sc1, g_norm1[l].reshape(1, -1), cos_t, sin_t, win, g_cq[l].reshape(1, -1), wuq,
        g_ckv[l].reshape(1, -1), wukv)
    qn, qr, kn, kr, vb = proj[3:8]

    out_a = _dilated_attention(proj[0:3], proj[8:11], proj[11:14], _bias_tables(rel_bias))
    out_b = _latent_attention(qn, qr, kn, kr, vb)

    return _post(x, out_a, out_b, g1m, sh2, sc2, g2m, g_out_a[l].reshape(1, -1),
                 g_out_b[l].reshape(1, -1), w_out[l].astype(BF16), g_norm2[l].reshape(1, -1),
                 w_ffn_in[l].astype(BF16), w_ffn_out[l].astype(BF16), g_final.reshape(1, -1))
```

```python
import functools

import jax
import jax.numpy as jnp
import numpy as np
from jax import lax
from jax.experimental import pallas as pl
from jax.experimental.pallas import tpu as pltpu

F32 = jnp.float32
BF16 = jnp.bfloat16

D_MODEL = 1024
SEQ = 2048
A_HEADS = 8
A_HEAD_DIM = 64
D_A = A_HEADS * A_HEAD_DIM
DILATIONS = (1, 4, 16)
BLK = 128
B_HEADS = 8
B_NOPE_DIM = 64
B_ROPE_DIM = 32
B_V_DIM = 64
D_B = B_HEADS * B_V_DIM
Q_LORA = 384
KV_LORA = 256
ROPE_THETA = 10000.0
N_BUCKETS = 32
MAX_DISTANCE = 2048
D_FF = 2816
N_MOD = 6
EPS = 1e-6
NEG = -1e30
LOG2E = 1.4426950408889634

LANES = 128
N_PAIRS = A_HEADS // 2
VMEM_LIMIT = 56 * 1024 * 1024

_C_QA, _C_KA, _C_VA = 0, D_A, 2 * D_A
_C_CQ = 3 * D_A
_C_CKV = _C_CQ + Q_LORA
_C_KR = _C_CKV + KV_LORA
P_IN2 = _C_KR + LANES

BN_MOD = 1536
TM_PROJ = 1024
SUB_PROJ = 512
TM_POST = 1024
SUB_POST = 512
FF_CHUNK = 256
LOOKAHEAD = 2
PAIRS_PER_STEP = 2
LOOKAHEAD_MLA = 1
QB_MLA = 256
KC_MLA = 512


def _nt_dot(a, b):
    return lax.dot_general(a, b, (((1,), (1,)), ((), ())), preferred_element_type=F32)


def _rms(x, g):
    return x * lax.rsqrt(jnp.mean(x * x, axis=-1, keepdims=True) + EPS) * g


def _mod_kernel(c_ref, w_ref, b_ref, o_ref):
    cond = jax.nn.silu(c_ref[...])
    o_ref[...] = jnp.dot(cond.astype(BF16), w_ref[...].astype(BF16),
                         preferred_element_type=F32) + b_ref[...]


def _modulation(c, w_ada, b_ada):
    bsz = c.shape[0]
    n = w_ada.shape[1]
    bn = BN_MOD
    return pl.pallas_call(
        _mod_kernel,
        grid=(n // bn,),
        in_specs=[pl.BlockSpec((bsz, D_MODEL), lambda j: (0, 0)),
                  pl.BlockSpec((D_MODEL, bn), lambda j: (0, j)),
                  pl.BlockSpec((1, bn), lambda j: (0, j))],
        out_specs=pl.BlockSpec((bsz, bn), lambda j: (0, j)),
        out_shape=jax.ShapeDtypeStruct((bsz, n), F32),
        compiler_params=pltpu.CompilerParams(vmem_limit_bytes=VMEM_LIMIT),
        name="adaln_mod",
    )(c, w_ada, b_ada.reshape(1, n))


def _proj_kernel(x_ref, sh_ref, sc_ref, g1_ref, cos_ref, sin_ref, win_ref, gcq_ref, wuq_ref,
                 gckv_ref, wukv_ref,
                 qa_ref, ka_ref, va_ref, qn_ref, qr_ref, kn_ref, kr_ref, vb_ref,
                 qa16_ref, ka16_ref, va16_ref, fa_scr, fb_scr):
    for t in range(TM_PROJ // SUB_PROJ):
        rows = slice(t * SUB_PROJ, (t + 1) * SUB_PROJ)
        h = _rms(x_ref[0, rows, :], g1_ref[...]) * (1.0 + sc_ref[0]) + sh_ref[0]
        hb = h.astype(BF16)

        def proj(lo, hi):
            return jnp.dot(hb, win_ref[:, lo:hi], preferred_element_type=F32)

        def emit_orders(res, blk_ref, d16_ref):
            fa, fb = fa_scr.at[t], fb_scr.at[t]
            n4, n16 = SUB_PROJ // 4, SUB_PROJ // 16
            for g in range(D_A // LANES):
                lanes = slice(g * LANES, (g + 1) * LANES)
                fa[g] = res[:, lanes]
                for r in range(4):
                    fb[g, r * n4:(r + 1) * n4, :] = fa[g, pl.ds(r, n4, stride=4), :]
                for j in range(16):
                    x16 = fb[g, pl.ds((j % 4) * n4 + j // 4, n16, stride=4), :]
                    d16_ref[0, j, t * n16:(t + 1) * n16, lanes] = x16.astype(BF16)
                for nb in range(SUB_PROJ // BLK):
                    for j in range(0, 16, 2):
                        two = [fb[g, pl.ds((jj % 4) * n4 + nb * (BLK // 4) + jj // 4, BLK // 16, stride=4), :]
                               for jj in (j, j + 1)]
                        row = t * SUB_PROJ + nb * BLK + j * (BLK // 16)
                        blk_ref[0, row:row + 2 * (BLK // 16), lanes] = jnp.concatenate(two, axis=0).astype(BF16)

        emit_orders(proj(_C_QA, _C_KA), qa_ref, qa16_ref)
        emit_orders(proj(_C_KA, _C_VA), ka_ref, ka16_ref)
        emit_orders(proj(_C_VA, _C_CQ), va_ref, va16_ref)

        cos = cos_ref[rows, :]
        sin = sin_ref[rows, :]
        kr = proj(_C_KR, P_IN2)
        nr = B_ROPE_DIM
        kr = kr[:, :nr] * cos[:, :nr] + kr[:, nr:2 * nr] * sin[:, :nr]
        kr_ref[0, rows, :] = jnp.concatenate([kr] * (LANES // nr), axis=1).astype(BF16)

        cq = _rms(proj(_C_CQ, _C_CKV), gcq_ref[...]).astype(BF16)
        q = jnp.dot(cq, wuq_ref[...], preferred_element_type=F32)
        qn_ref[0, rows, :] = q[:, :D_B].astype(BF16)
        cos2 = jnp.concatenate([cos, cos], axis=1)
        sin2 = jnp.concatenate([sin, sin], axis=1)
        n_r = B_HEADS * B_ROPE_DIM
        qr_ref[0, rows, :] = (q[:, D_B:D_B + n_r] * cos2 + q[:, D_B + n_r:] * sin2).astype(BF16)

        ckv = _rms(proj(_C_CKV, _C_KR), gckv_ref[...]).astype(BF16)
        kv = jnp.dot(ckv, wukv_ref[...], preferred_element_type=F32)
        kn_ref[0, rows, :] = kv[:, :D_B].astype(BF16)
        vb_ref[0, rows, :] = kv[:, D_B:].astype(BF16)


def _const_spec(shape):
    nd = len(shape)
    return pl.BlockSpec(shape, lambda *_: (0,) * nd, pipeline_mode=pl.Buffered(1))


def _projections(x, sh1, sc1, g1, cos_t, sin_t, win, gcq, wuq, gckv, wukv):
    bsz = x.shape[0]
    tm = TM_PROJ
    tok = lambda w: pl.BlockSpec((1, tm, w), lambda b, j: (b, j, 0))
    per_b = pl.BlockSpec((1, 1, D_MODEL), lambda b, j: (b, 0, 0))
    tab = pl.BlockSpec((tm, LANES), lambda b, j: (j, 0))
    widths = (D_A, D_A, D_A, D_B, B_HEADS * B_ROPE_DIM, D_B, LANES, D_B)
    res_spec = lambda d: pl.BlockSpec((1, d, tm // d, D_A), lambda b, j: (b, 0, j, 0))
    res_shape = lambda d: jax.ShapeDtypeStruct((bsz, d, SEQ // d, D_A), BF16)
    chain_f32 = pltpu.VMEM((tm // SUB_PROJ, D_A // LANES, SUB_PROJ, LANES), F32)
    return pl.pallas_call(
        _proj_kernel,
        grid=(bsz, SEQ // tm),
        in_specs=[tok(D_MODEL), per_b, per_b, _const_spec((1, D_MODEL)), tab, tab,
                  _const_spec(win.shape), _const_spec((1, Q_LORA)), _const_spec(wuq.shape),
                  _const_spec((1, KV_LORA)), _const_spec(wukv.shape)],
        out_specs=[tok(w) for w in widths] + [res_spec(16)] * 3,
        out_shape=[jax.ShapeDtypeStruct((bsz, SEQ, w), BF16) for w in widths]
        + [res_shape(16)] * 3,
        scratch_shapes=[chain_f32, chain_f32],
        compiler_params=pltpu.CompilerParams(
            dimension_semantics=("parallel", "parallel"), vmem_limit_bytes=VMEM_LIMIT),
        name="in_proj",
    )(x, sh1, sc1, g1, cos_t, sin_t, win, gcq, wuq, gckv, wukv)


def _stack_rows(ref, c, row):
    return jnp.concatenate([ref[c, 0, pl.ds(row, BLK), :], ref[c, 1, pl.ds(row, BLK), :]], axis=0)


def _softmax_update(s, vb, state):
    nk = s.shape[1]
    m_cur = jnp.max(s, axis=1, keepdims=True)
    if state is None:
        m_new = jnp.broadcast_to(m_cur, (2 * BLK, LANES))
    else:
        m0, l0, acc0 = state
        m_new = jnp.maximum(m0, m_cur)
    m_wide = m_new if nk == LANES else jnp.concatenate([m_new] * (nk // LANES), axis=1)
    p = jnp.exp2(s - m_wide)
    v_ones = jnp.concatenate([vb, jnp.ones((nk, LANES), BF16)], axis=1)
    pv = jnp.dot(p.astype(BF16), v_ones, preferred_element_type=F32)
    acc, l_new = pv[:, :LANES], pv[:, LANES:]
    if state is not None:
        alpha = jnp.exp2(m0 - m_new)
        l_new = alpha * l0 + l_new
        acc = alpha * acc0 + acc
    return m_new, l_new, acc


def _dil_kernel(q1_ref, k1_ref, v1_ref, q16_ref, k16_ref, v16_ref, tab_ref, o_ref, st4, st1, ost):
    lane = lax.broadcasted_iota(jnp.int32, (BLK, LANES), 1)
    head_a = lane < A_HEAD_DIM
    n4 = SEQ // 4
    quarter, eighth = BLK // 4, BLK // 16

    def halves(state):
        return [(c, h, x[h * BLK:(h + 1) * BLK]) for c, x in enumerate(state) for h in range(2)]

    def sink16(s4, j, state):
        for c, h, x in halves(state):
            for n in range(4):
                dst = (j % 4) * n4 + n * BLK + (j // 4) * quarter
                s4[c, h, dst:dst + quarter, :] = x[n * quarter:(n + 1) * quarter]

    def sink4(s1, r, n, state):
        for c, h, x in halves(state):
            for q in range(4):
                for a in range(4):
                    dst = (4 * n + a) * BLK + (4 * q + r) * eighth
                    s1[c, h, dst:dst + eighth, :] = x[q * quarter + a * eighth:q * quarter + (a + 1) * eighth]

    def load_state(src, row):
        return tuple(_stack_rows(src, c, row) for c in range(3))

    def finish(pp, n, state):
        _, l, acc = state
        o = acc / l
        o = jnp.where(head_a, o[:BLK], o[BLK:])
        stage = ost.at[pp]
        for j in range(16):
            stage[pl.ds(n * BLK + j, eighth, stride=16), :] = o[j * eighth:(j + 1) * eighth]
        o_ref[0, n * BLK:(n + 1) * BLK, pp * LANES:(pp + 1) * LANES] = stage[n * BLK:(n + 1) * BLK, :].astype(BF16)

    def rows16(src16, r, blocks, lanes):
        return jnp.concatenate([src16[0, 4 * q + r, n * quarter:(n + 1) * quarter, lanes]
                                for n in blocks for q in range(4)], axis=0)

    def pair_blocks(pp):
        lanes = slice(pp * LANES, (pp + 1) * LANES)
        s4, s1 = st4.at[pp], st1.at[pp]
        out = []
        for j in range(SEQ // BLK):
            get = lambda ref, j=j: ref[0, j, :, lanes]
            out.append((pp, 2, functools.partial(get, q16_ref), functools.partial(get, k16_ref),
                        functools.partial(get, v16_ref), True, None, 0, functools.partial(sink16, s4, j)))
        for n in range(n4 // BLK):
            for r in range(4):
                keys = (n,) if n == 0 else (n - 1, n)
                out.append((pp, 1, functools.partial(rows16, q16_ref, r, (n,), lanes),
                            functools.partial(rows16, k16_ref, r, keys, lanes),
                            functools.partial(rows16, v16_ref, r, keys, lanes),
                            n == 0, s4, r * n4 + n * BLK, functools.partial(sink4, s1, r, n)))
        for n in range(SEQ // BLK):
            lo = n * BLK if n == 0 else (n - 1) * BLK
            get = lambda ref, a, b: ref[0, a:b, lanes]
            out.append((pp, 0, functools.partial(get, q1_ref, n * BLK, (n + 1) * BLK),
                        functools.partial(get, k1_ref, lo, (n + 1) * BLK),
                        functools.partial(get, v1_ref, lo, (n + 1) * BLK),
                        n == 0, s1, n * BLK, functools.partial(finish, pp, n)))
        return out

    def logits(blk):
        pp, br, q_rows, k_rows, _, first, _, _, _ = blk
        q = q_rows()
        zero = jnp.zeros_like(q)
        qst = jnp.concatenate([jnp.where(head_a, q, zero), jnp.where(head_a, zero, q)], axis=0)
        tab = tab_ref[pp, br, :, BLK:] if first else tab_ref[pp, br]
        return _nt_dot(qst, k_rows()) + tab

    def update(blk, s):
        _, _, _, _, v_rows, _, st_src, st_row, sink = blk
        state = None if st_src is None else load_state(st_src, st_row)
        sink(_softmax_update(s, v_rows(), state))

    per_branch = SEQ // BLK
    blocks = sorted((blk for pp in range(PAIRS_PER_STEP) for blk in enumerate(pair_blocks(pp))),
                    key=lambda ib: (ib[0] + per_branch * ib[1][0], ib[1][0]))
    pending = []
    for _, blk in blocks:
        pending.append((blk, logits(blk)))
        if len(pending) > LOOKAHEAD:
            update(*pending.pop(0))
    for item in pending:
        update(*item)


def _dilated_attention(qkv1, qkv16, tables):
    bsz = qkv1[0].shape[0]
    w = PAIRS_PER_STEP * LANES
    pairs = pl.BlockSpec((1, SEQ, w), lambda p, b: (b, 0, p))
    pairs16 = pl.BlockSpec((1, 16, SEQ // 16, w), lambda p, b: (b, 0, 0, p))
    state = pltpu.VMEM((PAIRS_PER_STEP, 3, 2, SEQ, LANES), F32)
    return pl.pallas_call(
        _dil_kernel,
        grid=(N_PAIRS // PAIRS_PER_STEP, bsz),
        in_specs=[pairs] * 3 + [pairs16] * 3 + [
            pl.BlockSpec((PAIRS_PER_STEP, 3, 2 * BLK, 2 * BLK), lambda p, b: (p, 0, 0, 0))],
        out_specs=pairs,
        out_shape=jax.ShapeDtypeStruct((bsz, SEQ, D_A), BF16),
        scratch_shapes=[state, state, pltpu.VMEM((PAIRS_PER_STEP, SEQ, LANES), F32)],
        compiler_params=pltpu.CompilerParams(
            dimension_semantics=("parallel", "parallel"), vmem_limit_bytes=VMEM_LIMIT),
        name="dilated_attn",
    )(*qkv1, *qkv16, tables)


def _mla_kernel(qn_ref, qr_ref, kn_ref, kr_ref, v_ref, o_ref):
    qb = QB_MLA
    kc = KC_MLA
    n_qb = SEQ // qb
    lane = lax.broadcasted_iota(jnp.int32, (qb, LANES), 1)
    head_a = lane < B_NOPE_DIM
    slot = lane // B_ROPE_DIM

    def stacked_q(h, p):
        rows = slice(h * qb, (h + 1) * qb)
        qn = qn_ref[0, rows, p * LANES:(p + 1) * LANES]
        qr = qr_ref[0, rows, (p // 2) * LANES:(p // 2 + 1) * LANES]
        zero = jnp.zeros_like(qn)
        sa = 2 * (p % 2)
        q_a = jnp.concatenate([jnp.where(head_a, qn, zero), jnp.where(slot == sa, qr, zero)], axis=1)
        q_b = jnp.concatenate([jnp.where(head_a, zero, qn), jnp.where(slot == sa + 1, qr, zero)], axis=1)
        return jnp.concatenate([q_a, q_b], axis=0)

    qst = {(h, p): stacked_q(h, p) for h in range(n_qb) for p in range(N_PAIRS)}

    def logits(chains, keys, diagonal):
        p = chains[0][1]
        cols = slice(p * LANES, (p + 1) * LANES)
        q = qst[chains[0]] if len(chains) == 1 else jnp.concatenate([qst[c] for c in chains], axis=0)
        s = _nt_dot(q, jnp.concatenate([kn_ref[0, keys, cols], kr_ref[0, keys, :]], axis=1))
        if diagonal:
            qpos = lax.broadcasted_iota(jnp.int32, (2 * qb, qb), 0) % qb
            kpos = lax.broadcasted_iota(jnp.int32, (2 * qb, qb), 1)
            tail = jnp.where(qpos >= kpos, s[:, -qb:], NEG)
            s = tail if s.shape[1] == qb else jnp.concatenate([s[:, :-qb], tail], axis=1)
        return s

    def update(chains, keys, s, state):
        rows, nk = s.shape
        p = chains[0][1]
        m_cur = jnp.max(s, axis=1, keepdims=True)
        m_new = jnp.broadcast_to(m_cur, (rows, LANES)) if state is None else jnp.maximum(state[0], m_cur)
        pr = jnp.exp2(s - jnp.concatenate([m_new] * (nk // LANES), axis=1))
        v_ones = jnp.concatenate([v_ref[0, keys, p * LANES:(p + 1) * LANES], jnp.ones((nk, LANES), BF16)],
                                 axis=1)
        pv = jnp.dot(pr.astype(BF16), v_ones, preferred_element_type=F32)
        acc, l_new = pv[:, :LANES], pv[:, LANES:]
        if state is None:
            return m_new, l_new, acc
        alpha = jnp.exp2(state[0] - m_new)
        return m_new, alpha * state[1] + l_new, alpha * state[2] + acc

    per_group = kc // qb
    steps = []
    for lo in range(0, SEQ, kc):
        for g0 in range(0, n_qb, per_group):
            group = tuple(range(g0, g0 + per_group))
            for p in range(N_PAIRS):
                if lo + kc <= g0 * qb:
                    steps.append((tuple((h, p) for h in group), slice(lo, lo + kc), False))
                else:
                    steps += [(((h, p),), slice(lo, min(lo + kc, (h + 1) * qb)), True)
                              for h in group if lo < (h + 1) * qb and lo + kc > g0 * qb]
    states = {chain: None for chain in qst}

    def run(chains, keys, s):
        rows = 2 * qb
        state = None
        if states[chains[0]] is not None:
            state = tuple(jnp.concatenate([states[c][i] for c in chains], axis=0) if len(chains) > 1
                          else states[chains[0]][i] for i in range(3))
        new = update(chains, keys, s, state)
        for k, c in enumerate(chains):
            states[c] = tuple(x[k * rows:(k + 1) * rows] for x in new)

    pending = []
    for chains, keys, diagonal in steps:
        pending.append((chains, keys, logits(chains, keys, diagonal)))
        if len(pending) > LOOKAHEAD_MLA:
            run(*pending.pop(0))
    for item in pending:
        run(*item)
    for (h, p), (_, l, acc) in states.items():
        o = acc / l
        o_ref[0, h * qb:(h + 1) * qb, p * LANES:(p + 1) * LANES] = jnp.where(
            head_a, o[:qb], o[qb:]).astype(BF16)


def _latent_attention(qn, qr, kn, kr, vb):
    bsz = qn.shape[0]
    blk = lambda w: pl.BlockSpec((1, SEQ, w), lambda b: (b, 0, 0))
    return pl.pallas_call(
        _mla_kernel,
        grid=(bsz,),
        in_specs=[blk(D_B), blk(B_HEADS * B_ROPE_DIM), blk(D_B), blk(LANES), blk(D_B)],
        out_specs=blk(D_B),
        out_shape=jax.ShapeDtypeStruct((bsz, SEQ, D_B), BF16),
        compiler_params=pltpu.CompilerParams(
            dimension_semantics=("parallel",), vmem_limit_bytes=VMEM_LIMIT),
        name="latent_attn",
    )(qn, qr, kn, kr, vb)


def _post_kernel(x_ref, oa_ref, ob_ref, g1_ref, sh2_ref, sc2_ref, g2_ref, goa_ref, gob_ref,
                 wout_ref, gn2_ref, wfi_ref, wfo_ref, gfin_ref, o_ref, a_scr):
    chains = [slice(t * SUB_POST, (t + 1) * SUB_POST) for t in range(TM_POST // SUB_POST)]

    def mixed(rows):
        ya = _rms(oa_ref[0, rows, :].astype(F32), goa_ref[...])
        yb = _rms(ob_ref[0, rows, :].astype(F32), gob_ref[...])
        y = jnp.concatenate([ya, yb], axis=1).astype(BF16)
        return jnp.dot(y, wout_ref[...], preferred_element_type=F32)

    def ffn(rows, x1):
        h2 = (_rms(x1, gn2_ref[...]) * (1.0 + sc2_ref[0]) + sh2_ref[0]).astype(BF16)
        for c in range(D_FF // FF_CHUNK):
            lo = c * FF_CHUNK
            g = jnp.dot(h2, wfi_ref[:, lo:lo + FF_CHUNK], preferred_element_type=F32)
            u = jnp.dot(h2, wfi_ref[:, D_FF + lo:D_FF + lo + FF_CHUNK], preferred_element_type=F32)
            a_scr[rows, lo:lo + FF_CHUNK] = (jax.nn.silu(g) * u).astype(BF16)
        return jnp.dot(a_scr[rows, :], wfo_ref[...], preferred_element_type=F32)

    mix = [mixed(rows) for rows in chains]
    x1 = [x_ref[0, rows, :] + g1_ref[0] * m for rows, m in zip(chains, mix)]
    f = [ffn(rows, x) for rows, x in zip(chains, x1)]
    for rows, x, y in zip(chains, x1, f):
        o_ref[0, rows, :] = _rms(x + g2_ref[0] * y, gfin_ref[...])


def _post(x, oa, ob, g1m, sh2, sc2, g2m, goa, gob, wout, gn2, wfi, wfo, gfin):
    bsz = x.shape[0]
    tm = TM_POST
    tok = lambda w: pl.BlockSpec((1, tm, w), lambda b, j: (b, j, 0))
    per_b = pl.BlockSpec((1, 1, D_MODEL), lambda b, j: (b, 0, 0))
    return pl.pallas_call(
        _post_kernel,
        grid=(bsz, SEQ // tm),
        in_specs=[tok(D_MODEL), tok(D_A), tok(D_B), per_b, per_b, per_b, per_b,
                  _const_spec((1, D_A)), _const_spec((1, D_B)), _const_spec(wout.shape),
                  _const_spec((1, D_MODEL)), _const_spec(wfi.shape), _const_spec(wfo.shape),
                  _const_spec((1, D_MODEL))],
        out_specs=tok(D_MODEL),
        out_shape=jax.ShapeDtypeStruct((bsz, SEQ, D_MODEL), F32),
        scratch_shapes=[pltpu.VMEM((tm, D_FF), BF16)],
        compiler_params=pltpu.CompilerParams(
            dimension_semantics=("parallel", "parallel"), vmem_limit_bytes=VMEM_LIMIT),
        name="out_proj_ffn",
    )(x, oa, ob, g1m, sh2, sc2, g2m, goa, gob, wout, gn2, wfi, wfo, gfin)


def _rotate_half_cols(w):
    lead = w.shape[:-1]
    half = B_ROPE_DIM // 2
    w = w.reshape(lead + (-1, 2, half))
    return jnp.stack([-w[..., 1, :], w[..., 0, :]], axis=-2).reshape(lead + (-1,))


def _t5_bucket(dist):
    max_exact = N_BUCKETS // 2
    d = np.maximum(dist, 1).astype(np.float64)
    large = max_exact + (np.log(d / max_exact) / np.log(MAX_DISTANCE / max_exact)
                         * (N_BUCKETS - max_exact)).astype(np.int64)
    large = np.minimum(large, N_BUCKETS - 1)
    return np.where(dist < max_exact, dist, large).astype(np.int32)


def _grouped_order(groups):
    rows = np.arange(BLK)
    per = BLK // groups
    return groups * (rows % per) + rows // per


def _bias_kernel(bucket_ref, rb_ref, o_ref):
    p = pl.program_id(0)
    for br in range(len(DILATIONS)):
        bucket = bucket_ref[br]
        tabs = [jnp.full((BLK, 2 * BLK), NEG, F32)] * 2
        for k in range(N_BUCKETS):
            hit = bucket == k
            tabs = [jnp.where(hit, rb_ref[k, 2 * p + hh] * LOG2E, tabs[hh]) for hh in range(2)]
        for hh in range(2):
            o_ref[0, br, hh * BLK:(hh + 1) * BLK, :] = tabs[hh]


def _bias_tables(rel_bias):
    a = np.arange(BLK)[:, None]
    bk = np.arange(2 * BLK)[None, :]
    steps = BLK + a - bk
    valid = (steps >= 0) & (steps <= BLK)
    bucket = np.stack([np.where(valid, _t5_bucket(np.clip(steps, 0, BLK) * dil), -1)
                       for dil in DILATIONS]).astype(np.int32)
    for br, groups in ((0, 16), (1, 4)):
        seq = _grouped_order(groups)
        bucket[br] = bucket[br][seq][:, np.concatenate([seq, BLK + seq])]
    nbr = len(DILATIONS)
    return pl.pallas_call(
        _bias_kernel,
        grid=(N_PAIRS,),
        in_specs=[pl.BlockSpec((nbr, BLK, 2 * BLK), lambda p: (0, 0, 0)),
                  pl.BlockSpec(memory_space=pltpu.SMEM)],
        out_specs=pl.BlockSpec((1, nbr, 2 * BLK, 2 * BLK), lambda p: (p, 0, 0, 0)),
        out_shape=jax.ShapeDtypeStruct((N_PAIRS, nbr, 2 * BLK, 2 * BLK), F32),
        name="bias_tables",
    )(jnp.asarray(bucket), rel_bias)


def _rope_tables():
    half = B_ROPE_DIM // 2
    inv = ROPE_THETA ** (-jnp.arange(half, dtype=F32) / half)
    ang = jnp.arange(SEQ, dtype=F32)[:, None] * inv[None, :]
    reps = LANES // half
    return jnp.tile(jnp.cos(ang), (1, reps)), jnp.tile(jnp.sin(ang), (1, reps))


def kernel(x, c, w_ada, b_ada, g_norm1, w_in, g_cq, w_uq, g_ckv, w_ukv, rel_bias, g_out_a, g_out_b,
           w_out, g_norm2, w_ffn_in, w_ffn_out, g_final):
    bsz = x.shape[0]
    assert x.shape == (bsz, SEQ, D_MODEL) and w_ada.shape[0] == 1
    l = 0

    mod = _modulation(c, w_ada[l], b_ada[l]).reshape(bsz, N_MOD, 1, D_MODEL)
    sh1, sc1, g1m, sh2, sc2, g2m = (mod[:, i] for i in range(N_MOD))

    wi = w_in[l]
    kr_cols = wi[:, 3 * D_A + Q_LORA + KV_LORA:]
    win = jnp.concatenate([wi[:, :D_A] * (A_HEAD_DIM ** -0.5 * LOG2E), wi[:, D_A:3 * D_A + Q_LORA + KV_LORA],
                           kr_cols, _rotate_half_cols(kr_cols),
                           jnp.zeros((D_MODEL, LANES - 2 * B_ROPE_DIM), F32)], axis=1).astype(BF16)
    wq = (w_uq[l] * ((B_NOPE_DIM + B_ROPE_DIM) ** -0.5 * LOG2E)).reshape(
        Q_LORA, B_HEADS, B_NOPE_DIM + B_ROPE_DIM)
    wq_rope = wq[:, :, B_NOPE_DIM:].reshape(Q_LORA, -1)
    wuq = jnp.concatenate([wq[:, :, :B_NOPE_DIM].reshape(Q_LORA, -1), wq_rope,
                           _rotate_half_cols(wq_rope)], axis=1).astype(BF16)
    wkv = w_ukv[l].reshape(KV_LORA, B_HEADS, B_NOPE_DIM + B_V_DIM)
    wukv = jnp.concatenate([wkv[:, :, :B_NOPE_DIM].reshape(KV_LORA, -1),
                            wkv[:, :, B_NOPE_DIM:].reshape(KV_LORA, -1)], axis=1).astype(BF16)
    cos_t, sin_t = _rope_tables()

    proj = _projections(
        x, sh1, sc1, g_norm1[l].reshape(1, -1), cos_t, sin_t, win, g_cq[l].reshape(1, -1), wuq,
        g_ckv[l].reshape(1, -1), wukv)
    qn, qr, kn, kr, vb = proj[3:8]

    out_a = _dilated_attention(proj[0:3], proj[8:11], _bias_tables(rel_bias))
    out_b = _latent_attention(qn, qr, kn, kr, vb)

    return _post(x, out_a, out_b, g1m, sh2, sc2, g2m, g_out_a[l].reshape(1, -1),
                 g_out_b[l].reshape(1, -1), w_out[l].astype(BF16), g_norm2[l].reshape(1, -1),
                 w_ffn_in[l].astype(BF16), w_ffn_out[l].astype(BF16), g_final.reshape(1, -1))
```

```python
import functools

import jax
import jax.numpy as jnp
import numpy as np
from jax import lax
from jax.experimental import pallas as pl
from jax.experimental.pallas import tpu as pltpu

F32 = jnp.float32
BF16 = jnp.bfloat16

D_MODEL = 1024
SEQ = 2048
A_HEADS = 8
A_HEAD_DIM = 64
D_A = A_HEADS * A_HEAD_DIM
DILATIONS = (1, 4, 16)
BLK = 128
B_HEADS = 8
B_NOPE_DIM = 64
B_ROPE_DIM = 32
B_V_DIM = 64
D_B = B_HEADS * B_V_DIM
Q_LORA = 384
KV_LORA = 256
ROPE_THETA = 10000.0
N_BUCKETS = 32
MAX_DISTANCE = 2048
D_FF = 2816
N_MOD = 6
EPS = 1e-6
NEG = -1e30
LOG2E = 1.4426950408889634

LANES = 128
N_PAIRS = A_HEADS // 2
VMEM_LIMIT = 56 * 1024 * 1024

_C_QA, _C_KA, _C_VA = 0, D_A, 2 * D_A
_C_CQ = 3 * D_A
_C_CKV = _C_CQ + Q_LORA
_C_KR = _C_CKV + KV_LORA
P_IN2 = _C_KR + LANES

BN_MOD = 1536
TM_PROJ = 1024
SUB_PROJ = 512
TM_POST = 1024
SUB_POST = 512
FF_CHUNK = 256
LOOKAHEAD = 2
PAIRS_PER_STEP = 2
LOOKAHEAD_MLA = 1
QB_MLA = 256
KC_MLA = 512


def _nt_dot(a, b):
    return lax.dot_general(a, b, (((1,), (1,)), ((), ())), preferred_element_type=F32)


def _rms(x, g):
    return x * lax.rsqrt(jnp.mean(x * x, axis=-1, keepdims=True) + EPS) * g


def _mod_kernel(c_ref, w_ref, b_ref, o_ref):
    cond = jax.nn.silu(c_ref[...])
    o_ref[...] = jnp.dot(cond.astype(BF16), w_ref[...].astype(BF16),
                         preferred_element_type=F32) + b_ref[...]


def _modulation(c, w_ada, b_ada):
    bsz = c.shape[0]
    n = w_ada.shape[1]
    bn = BN_MOD
    return pl.pallas_call(
        _mod_kernel,
        grid=(n // bn,),
        in_specs=[pl.BlockSpec((bsz, D_MODEL), lambda j: (0, 0)),
                  pl.BlockSpec((D_MODEL, bn), lambda j: (0, j)),
                  pl.BlockSpec((1, bn), lambda j: (0, j))],
        out_specs=pl.BlockSpec((bsz, bn), lambda j: (0, j)),
        out_shape=jax.ShapeDtypeStruct((bsz, n), F32),
        compiler_params=pltpu.CompilerParams(vmem_limit_bytes=VMEM_LIMIT),
        name="adaln_mod",
    )(c, w_ada, b_ada.reshape(1, n))


def _proj_kernel(x_ref, sh_ref, sc_ref, g1_ref, cos_ref, sin_ref, win_ref, gcq_ref, wuq_ref,
                 gckv_ref, wukv_ref,
                 qa_ref, ka_ref, va_ref, qn_ref, qr_ref, kn_ref, kr_ref, vb_ref,
                 qa16_ref, ka16_ref, va16_ref, fa_scr, fb_scr):
    for t in range(TM_PROJ // SUB_PROJ):
        rows = slice(t * SUB_PROJ, (t + 1) * SUB_PROJ)
        h = _rms(x_ref[0, rows, :], g1_ref[...]) * (1.0 + sc_ref[0]) + sh_ref[0]
        hb = h.astype(BF16)

        def proj(lo, hi):
            return jnp.dot(hb, win_ref[:, lo:hi], preferred_element_type=F32)

        def emit_orders(res, blk_ref, d16_ref):
            fa, fb = fa_scr.at[t], fb_scr.at[t]
            n4, n16 = SUB_PROJ // 4, SUB_PROJ // 16
            for g in range(D_A // LANES):
                lanes = slice(g * LANES, (g + 1) * LANES)
                fa[g] = res[:, lanes]
                for r in range(4):
                    fb[g, r * n4:(r + 1) * n4, :] = fa[g, pl.ds(r, n4, stride=4), :]
                for j in range(16):
                    x16 = fb[g, pl.ds((j % 4) * n4 + j // 4, n16, stride=4), :]
                    d16_ref[0, j, t * n16:(t + 1) * n16, lanes] = x16.astype(BF16)
                for nb in range(SUB_PROJ // BLK):
                    for j in range(0, 16, 2):
                        two = [fb[g, pl.ds((jj % 4) * n4 + nb * (BLK // 4) + jj // 4, BLK // 16, stride=4), :]
                               for jj in (j, j + 1)]
                        row = t * SUB_PROJ + nb * BLK + j * (BLK // 16)
                        blk_ref[0, row:row + 2 * (BLK // 16), lanes] = jnp.concatenate(two, axis=0).astype(BF16)

        emit_orders(proj(_C_QA, _C_KA), qa_ref, qa16_ref)
        emit_orders(proj(_C_KA, _C_VA), ka_ref, ka16_ref)
        emit_orders(proj(_C_VA, _C_CQ), va_ref, va16_ref)

        cos = cos_ref[rows, :]
        sin = sin_ref[rows, :]
        kr = proj(_C_KR, P_IN2)
        nr = B_ROPE_DIM
        kr = kr[:, :nr] * cos[:, :nr] + kr[:, nr:2 * nr] * sin[:, :nr]
        kr_ref[0, rows, :] = jnp.concatenate([kr] * (LANES // nr), axis=1).astype(BF16)

        cq = _rms(proj(_C_CQ, _C_CKV), gcq_ref[...]).astype(BF16)
        q = jnp.dot(cq, wuq_ref[...], preferred_element_type=F32)
        qn_ref[0, rows, :] = q[:, :D_B].astype(BF16)
        cos2 = jnp.concatenate([cos, cos], axis=1)
        sin2 = jnp.concatenate([sin, sin], axis=1)
        n_r = B_HEADS * B_ROPE_DIM
        qr_ref[0, rows, :] = (q[:, D_B:D_B + n_r] * cos2 + q[:, D_B + n_r:] * sin2).astype(BF16)

        ckv = _rms(proj(_C_CKV, _C_KR), gckv_ref[...]).astype(BF16)
        kv = jnp.dot(ckv, wukv_ref[...], preferred_element_type=F32)
        kn_ref[0, rows, :] = kv[:, :D_B].astype(BF16)
        vb_ref[0, rows, :] = kv[:, D_B:].astype(BF16)


def _const_spec(shape):
    nd = len(shape)
    return pl.BlockSpec(shape, lambda *_: (0,) * nd, pipeline_mode=pl.Buffered(1))


def _projections(x, sh1, sc1, g1, cos_t, sin_t, win, gcq, wuq, gckv, wukv):
    bsz = x.shape[0]
    tm = TM_PROJ
    tok = lambda w: pl.BlockSpec((1, tm, w), lambda b, j: (b, j, 0))
    per_b = pl.BlockSpec((1, 1, D_MODEL), lambda b, j: (b, 0, 0))
    tab = pl.BlockSpec((tm, LANES), lambda b, j: (j, 0))
    widths = (D_A, D_A, D_A, D_B, B_HEADS * B_ROPE_DIM, D_B, LANES, D_B)
    res_spec = lambda d: pl.BlockSpec((1, d, tm // d, D_A), lambda b, j: (b, 0, j, 0))
    res_shape = lambda d: jax.ShapeDtypeStruct((bsz, d, SEQ // d, D_A), BF16)
    chain_f32 = pltpu.VMEM((tm // SUB_PROJ, D_A // LANES, SUB_PROJ, LANES), F32)
    return pl.pallas_call(
        _proj_kernel,
        grid=(bsz, SEQ // tm),
        in_specs=[tok(D_MODEL), per_b, per_b, _const_spec((1, D_MODEL)), tab, tab,
                  _const_spec(win.shape), _const_spec((1, Q_LORA)), _const_spec(wuq.shape),
                  _const_spec((1, KV_LORA)), _const_spec(wukv.shape)],
        out_specs=[tok(w) for w in widths] + [res_spec(16)] * 3,
        out_shape=[jax.ShapeDtypeStruct((bsz, SEQ, w), BF16) for w in widths]
        + [res_shape(16)] * 3,
        scratch_shapes=[chain_f32, chain_f32],
        compiler_params=pltpu.CompilerParams(
            dimension_semantics=("parallel", "parallel"), vmem_limit_bytes=VMEM_LIMIT),
        name="in_proj",
    )(x, sh1, sc1, g1, cos_t, sin_t, win, gcq, wuq, gckv, wukv)


def _stack_rows(ref, c, row):
    return jnp.concatenate([ref[c, 0, pl.ds(row, BLK), :], ref[c, 1, pl.ds(row, BLK), :]], axis=0)


def _softmax_update(s, vb, state):
    nk = s.shape[1]
    m_cur = jnp.max(s, axis=1, keepdims=True)
    if state is None:
        m_new = jnp.broadcast_to(m_cur, (2 * BLK, LANES))
    else:
        m0, l0, acc0 = state
        m_new = jnp.maximum(m0, m_cur)
    m_wide = m_new if nk == LANES else jnp.concatenate([m_new] * (nk // LANES), axis=1)
    p = jnp.exp2(s - m_wide)
    v_ones = jnp.concatenate([vb, jnp.ones((nk, LANES), BF16)], axis=1)
    pv = jnp.dot(p.astype(BF16), v_ones, preferred_element_type=F32)
    acc, l_new = pv[:, :LANES], pv[:, LANES:]
    if state is not None:
        alpha = jnp.exp2(m0 - m_new)
        l_new = alpha * l0 + l_new
        acc = alpha * acc0 + acc
    return m_new, l_new, acc


def _dil_kernel(q1_ref, k1_ref, v1_ref, q16_ref, k16_ref, v16_ref, tab_ref, o_ref, st4, st1, ost):
    lane = lax.broadcasted_iota(jnp.int32, (BLK, LANES), 1)
    head_a = lane < A_HEAD_DIM
    n4 = SEQ // 4
    quarter, eighth = BLK // 4, BLK // 16

    def halves(state):
        return [(c, h, x[h * BLK:(h + 1) * BLK]) for c, x in enumerate(state) for h in range(2)]

    def sink16(s4, j, state):
        for c, h, x in halves(state):
            for n in range(4):
                dst = (j % 4) * n4 + n * BLK + (j // 4) * quarter
                s4[c, h, dst:dst + quarter, :] = x[n * quarter:(n + 1) * quarter]

    def sink4(s1, r, n, state):
        for c, h, x in halves(state):
            for q in range(4):
                for a in range(4):
                    dst = (4 * n + a) * BLK + (4 * q + r) * eighth
                    s1[c, h, dst:dst + eighth, :] = x[q * quarter + a * eighth:q * quarter + (a + 1) * eighth]

    def load_state(src, row):
        return tuple(_stack_rows(src, c, row) for c in range(3))

    def finish(pp, n, state):
        _, l, acc = state
        o = acc / l
        o = jnp.where(head_a, o[:BLK], o[BLK:])
        stage = ost.at[pp]
        for j in range(16):
            stage[pl.ds(n * BLK + j, eighth, stride=16), :] = o[j * eighth:(j + 1) * eighth]
        o_ref[0, n * BLK:(n + 1) * BLK, pp * LANES:(pp + 1) * LANES] = stage[n * BLK:(n + 1) * BLK, :].astype(BF16)

    def rows16(src16, r, blocks, lanes):
        return jnp.concatenate([src16[0, 4 * q + r, n * quarter:(n + 1) * quarter, lanes]
                                for n in blocks for q in range(4)], axis=0)

    def pair_blocks(pp):
        lanes = slice(pp * LANES, (pp + 1) * LANES)
        s4, s1 = st4.at[pp], st1.at[pp]
        out = []
        for j in range(SEQ // BLK):
            get = lambda ref, j=j: ref[0, j, :, lanes]
            out.append((pp, 2, functools.partial(get, q16_ref), functools.partial(get, k16_ref),
                        functools.partial(get, v16_ref), True, None, 0, functools.partial(sink16, s4, j)))
        for n in range(n4 // BLK):
            for r in range(4):
                keys = (n,) if n == 0 else (n - 1, n)
                out.append((pp, 1, functools.partial(rows16, q16_ref, r, (n,), lanes),
                            functools.partial(rows16, k16_ref, r, keys, lanes),
                            functools.partial(rows16, v16_ref, r, keys, lanes),
                            n == 0, s4, r * n4 + n * BLK, functools.partial(sink4, s1, r, n)))
        for n in range(SEQ // BLK):
            lo = n * BLK if n == 0 else (n - 1) * BLK
            get = lambda ref, a, b: ref[0, a:b, lanes]
            out.append((pp, 0, functools.partial(get, q1_ref, n * BLK, (n + 1) * BLK),
                        functools.partial(get, k1_ref, lo, (n + 1) * BLK),
                        functools.partial(get, v1_ref, lo, (n + 1) * BLK),
                        n == 0, s1, n * BLK, functools.partial(finish, pp, n)))
        return out

    def logits(blk):
        pp, br, q_rows, k_rows, _, first, _, _, _ = blk
        q = q_rows()
        zero = jnp.zeros_like(q)
        qst = jnp.concatenate([jnp.where(head_a, q, zero), jnp.where(head_a, zero, q)], axis=0)
        tab = tab_ref[pp, br, :, BLK:] if first else tab_ref[pp, br]
        return _nt_dot(qst, k_rows()) + tab

    def update(blk, s):
        _, _, _, _, v_rows, _, st_src, st_row, sink = blk
        state = None if st_src is None else load_state(st_src, st_row)
        sink(_softmax_update(s, v_rows(), state))

    per_branch = SEQ // BLK
    blocks = sorted((blk for pp in range(PAIRS_PER_STEP) for blk in enumerate(pair_blocks(pp))),
                    key=lambda ib: (ib[0] + per_branch * ib[1][0], ib[1][0]))
    pending = []
    for _, blk in blocks:
        pending.append((blk, logits(blk)))
        if len(pending) > LOOKAHEAD:
            update(*pending.pop(0))
    for item in pending:
        update(*item)


def _dilated_attention(qkv1, qkv16, tables):
    bsz = qkv1[0].shape[0]
    w = PAIRS_PER_STEP * LANES
    pairs = pl.BlockSpec((1, SEQ, w), lambda p, b: (b, 0, p))
    pairs16 = pl.BlockSpec((1, 16, SEQ // 16, w), lambda p, b: (b, 0, 0, p))
    state = pltpu.VMEM((PAIRS_PER_STEP, 3, 2, SEQ, LANES), F32)
    return pl.pallas_call(
        _dil_kernel,
        grid=(N_PAIRS // PAIRS_PER_STEP, bsz),
        in_specs=[pairs] * 3 + [pairs16] * 3 + [
            pl.BlockSpec((PAIRS_PER_STEP, 3, 2 * BLK, 2 * BLK), lambda p, b: (p, 0, 0, 0))],
        out_specs=pairs,
        out_shape=jax.ShapeDtypeStruct((bsz, SEQ, D_A), BF16),
        scratch_shapes=[state, state, pltpu.VMEM((PAIRS_PER_STEP, SEQ, LANES), F32)],
        compiler_params=pltpu.CompilerParams(
            dimension_semantics=("parallel", "parallel"), vmem_limit_bytes=VMEM_LIMIT),
        name="dilated_attn",
    )(*qkv1, *qkv16, tables)


def _mla_kernel(qn_ref, qr_ref, kn_ref, kr_ref, v_ref, o_ref):
    qb = QB_MLA
    kc = KC_MLA
    n_qb = SEQ // qb
    lane = lax.broadcasted_iota(jnp.int32, (qb, LANES), 1)
    head_a = lane < B_NOPE_DIM
    slot = lane // B_ROPE_DIM

    def stacked_q(h, p):
        rows = slice(h * qb, (h + 1) * qb)
        qn = qn_ref[0, rows, p * LANES:(p + 1) * LANES]
        qr = qr_ref[0, rows, (p // 2) * LANES:(p // 2 + 1) * LANES]
        zero = jnp.zeros_like(qn)
        sa = 2 * (p % 2)
        q_a = jnp.concatenate([jnp.where(head_a, qn, zero), jnp.where(slot == sa, qr, zero)], axis=1)
        q_b = jnp.concatenate([jnp.where(head_a, zero, qn), jnp.where(slot == sa + 1, qr, zero)], axis=1)
        return jnp.concatenate([q_a, q_b], axis=0)

    chains = [(h, p) for h in range(n_qb) for p in range(N_PAIRS)]

    def logits(chain, keys, diagonal):
        cols = slice(chain[1] * LANES, (chain[1] + 1) * LANES)
        s = _nt_dot(stacked_q(*chain), jnp.concatenate([kn_ref[0, keys, cols], kr_ref[0, keys, :]], axis=1))
        if diagonal:
            qpos = lax.broadcasted_iota(jnp.int32, (2 * qb, qb), 0) % qb
            kpos = lax.broadcasted_iota(jnp.int32, (2 * qb, qb), 1)
            tail = jnp.where(qpos >= kpos, s[:, -qb:], NEG)
            s = tail if s.shape[1] == qb else jnp.concatenate([s[:, :-qb], tail], axis=1)
        return s

    def update(chain, keys, s, state):
        nk = s.shape[1]
        p = chain[1]
        m_cur = jnp.max(s, axis=1, keepdims=True)
        m_new = jnp.broadcast_to(m_cur, (2 * qb, LANES)) if state is None else jnp.maximum(state[0], m_cur)
        pr = jnp.exp2(s - jnp.concatenate([m_new] * (nk // LANES), axis=1))
        v_ones = jnp.concatenate([v_ref[0, keys, p * LANES:(p + 1) * LANES], jnp.ones((nk, LANES), BF16)],
                                 axis=1)
        pv = jnp.dot(pr.astype(BF16), v_ones, preferred_element_type=F32)
        acc, l_new = pv[:, :LANES], pv[:, LANES:]
        if state is None:
            return m_new, l_new, acc
        alpha = jnp.exp2(state[0] - m_new)
        return m_new, alpha * state[1] + l_new, alpha * state[2] + acc

    ends = [(h + 1) * qb for h in range(n_qb)]
    steps = []
    for lo in range(0, SEQ, kc):
        for h, end in enumerate(ends):
            if lo < end:
                steps += [((h, p), slice(lo, min(lo + kc, end)), end) for p in range(N_PAIRS)]
    states = {chain: None for chain in chains}
    pending = []
    for chain, keys, end in steps:
        pending.append((chain, keys, logits(chain, keys, keys.stop == end)))
        if len(pending) > LOOKAHEAD_MLA:
            c0, k0, s0 = pending.pop(0)
            states[c0] = update(c0, k0, s0, states[c0])
    for c0, k0, s0 in pending:
        states[c0] = update(c0, k0, s0, states[c0])
    for (h, p), (_, l, acc) in states.items():
        o = acc / l
        o_ref[0, h * qb:(h + 1) * qb, p * LANES:(p + 1) * LANES] = jnp.where(
            head_a, o[:qb], o[qb:]).astype(BF16)


def _latent_attention(qn, qr, kn, kr, vb):
    bsz = qn.shape[0]
    blk = lambda w: pl.BlockSpec((1, SEQ, w), lambda b: (b, 0, 0))
    return pl.pallas_call(
        _mla_kernel,
        grid=(bsz,),
        in_specs=[blk(D_B), blk(B_HEADS * B_ROPE_DIM), blk(D_B), blk(LANES), blk(D_B)],
        out_specs=blk(D_B),
        out_shape=jax.ShapeDtypeStruct((bsz, SEQ, D_B), BF16),
        compiler_params=pltpu.CompilerParams(
            dimension_semantics=("parallel",), vmem_limit_bytes=VMEM_LIMIT),
        name="latent_attn",
    )(qn, qr, kn, kr, vb)


def _post_kernel(x_ref, oa_ref, ob_ref, g1_ref, sh2_ref, sc2_ref, g2_ref, goa_ref, gob_ref,
                 wout_ref, gn2_ref, wfi_ref, wfo_ref, gfin_ref, o_ref, a_scr):
    chains = [slice(t * SUB_POST, (t + 1) * SUB_POST) for t in range(TM_POST // SUB_POST)]

    def mixed(rows):
        ya = _rms(oa_ref[0, rows, :].astype(F32), goa_ref[...])
        yb = _rms(ob_ref[0, rows, :].astype(F32), gob_ref[...])
        y = jnp.concatenate([ya, yb], axis=1).astype(BF16)
        return jnp.dot(y, wout_ref[...], preferred_element_type=F32)

    def ffn(rows, x1):
        h2 = (_rms(x1, gn2_ref[...]) * (1.0 + sc2_ref[0]) + sh2_ref[0]).astype(BF16)
        for c in range(D_FF // FF_CHUNK):
            lo = c * FF_CHUNK
            g = jnp.dot(h2, wfi_ref[:, lo:lo + FF_CHUNK], preferred_element_type=F32)
            u = jnp.dot(h2, wfi_ref[:, D_FF + lo:D_FF + lo + FF_CHUNK], preferred_element_type=F32)
            a_scr[rows, lo:lo + FF_CHUNK] = (jax.nn.silu(g) * u).astype(BF16)
        return jnp.dot(a_scr[rows, :], wfo_ref[...], preferred_element_type=F32)

    mix = [mixed(rows) for rows in chains]
    x1 = [x_ref[0, rows, :] + g1_ref[0] * m for rows, m in zip(chains, mix)]
    f = [ffn(rows, x) for rows, x in zip(chains, x1)]
    for rows, x, y in zip(chains, x1, f):
        o_ref[0, rows, :] = _rms(x + g2_ref[0] * y, gfin_ref[...])


def _post(x, oa, ob, g1m, sh2, sc2, g2m, goa, gob, wout, gn2, wfi, wfo, gfin):
    bsz = x.shape[0]
    tm = TM_POST
    tok = lambda w: pl.BlockSpec((1, tm, w), lambda b, j: (b, j, 0))
    per_b = pl.BlockSpec((1, 1, D_MODEL), lambda b, j: (b, 0, 0))
    return pl.pallas_call(
        _post_kernel,
        grid=(bsz, SEQ // tm),
        in_specs=[tok(D_MODEL), tok(D_A), tok(D_B), per_b, per_b, per_b, per_b,
                  _const_spec((1, D_A)), _const_spec((1, D_B)), _const_spec(wout.shape),
                  _const_spec((1, D_MODEL)), _const_spec(wfi.shape), _const_spec(wfo.shape),
                  _const_spec((1, D_MODEL))],
        out_specs=tok(D_MODEL),
        out_shape=jax.ShapeDtypeStruct((bsz, SEQ, D_MODEL), F32),
        scratch_shapes=[pltpu.VMEM((tm, D_FF), BF16)],
        compiler_params=pltpu.CompilerParams(
            dimension_semantics=("parallel", "parallel"), vmem_limit_bytes=VMEM_LIMIT),
        name="out_proj_ffn",
    )(x, oa, ob, g1m, sh2, sc2, g2m, goa, gob, wout, gn2, wfi, wfo, gfin)


def _rotate_half_cols(w):
    lead = w.shape[:-1]
    half = B_ROPE_DIM // 2
    w = w.reshape(lead + (-1, 2, half))
    return jnp.stack([-w[..., 1, :], w[..., 0, :]], axis=-2).reshape(lead + (-1,))


def _t5_bucket(dist):
    max_exact = N_BUCKETS // 2
    d = np.maximum(dist, 1).astype(np.float64)
    large = max_exact + (np.log(d / max_exact) / np.log(MAX_DISTANCE / max_exact)
                         * (N_BUCKETS - max_exact)).astype(np.int64)
    large = np.minimum(large, N_BUCKETS - 1)
    return np.where(dist < max_exact, dist, large).astype(np.int32)


def _grouped_order(groups):
    rows = np.arange(BLK)
    per = BLK // groups
    return groups * (rows % per) + rows // per


def _bias_kernel(bucket_ref, rb_ref, o_ref):
    p = pl.program_id(0)
    for br in range(len(DILATIONS)):
        bucket = bucket_ref[br]
        tabs = [jnp.full((BLK, 2 * BLK), NEG, F32)] * 2
        for k in range(N_BUCKETS):
            hit = bucket == k
            tabs = [jnp.where(hit, rb_ref[k, 2 * p + hh] * LOG2E, tabs[hh]) for hh in range(2)]
        for hh in range(2):
            o_ref[0, br, hh * BLK:(hh + 1) * BLK, :] = tabs[hh]


def _bias_tables(rel_bias):
    a = np.arange(BLK)[:, None]
    bk = np.arange(2 * BLK)[None, :]
    steps = BLK + a - bk
    valid = (steps >= 0) & (steps <= BLK)
    bucket = np.stack([np.where(valid, _t5_bucket(np.clip(steps, 0, BLK) * dil), -1)
                       for dil in DILATIONS]).astype(np.int32)
    for br, groups in ((0, 16), (1, 4)):
        seq = _grouped_order(groups)
        bucket[br] = bucket[br][seq][:, np.concatenate([seq, BLK + seq])]
    nbr = len(DILATIONS)
    return pl.pallas_call(
        _bias_kernel,
        grid=(N_PAIRS,),
        in_specs=[pl.BlockSpec((nbr, BLK, 2 * BLK), lambda p: (0, 0, 0)),
                  pl.BlockSpec(memory_space=pltpu.SMEM)],
        out_specs=pl.BlockSpec((1, nbr, 2 * BLK, 2 * BLK), lambda p: (p, 0, 0, 0)),
        out_shape=jax.ShapeDtypeStruct((N_PAIRS, nbr, 2 * BLK, 2 * BLK), F32),
        name="bias_tables",
    )(jnp.asarray(bucket), rel_bias)


def _rope_tables():
    half = B_ROPE_DIM // 2
    inv = ROPE_THETA ** (-jnp.arange(half, dtype=F32) / half)
    ang = jnp.arange(SEQ, dtype=F32)[:, None] * inv[None, :]
    reps = LANES // half
    return jnp.tile(jnp.cos(ang), (1, reps)), jnp.tile(jnp.sin(ang), (1, reps))


def kernel(x, c, w_ada, b_ada, g_norm1, w_in, g_cq, w_uq, g_ckv, w_ukv, rel_bias, g_out_a, g_out_b,
           w_out, g_norm2, w_ffn_in, w_ffn_out, g_final):
    bsz = x.shape[0]
    assert x.shape == (bsz, SEQ, D_MODEL) and w_ada.shape[0] == 1
    l = 0

    mod = _modulation(c, w_ada[l], b_ada[l]).reshape(bsz, N_MOD, 1, D_MODEL)
    sh1, sc1, g1m, sh2, sc2, g2m = (mod[:, i] for i in range(N_MOD))

    wi = w_in[l]
    kr_cols = wi[:, 3 * D_A + Q_LORA + KV_LORA:]
    win = jnp.concatenate([wi[:, :D_A] * (A_HEAD_DIM ** -0.5 * LOG2E), wi[:, D_A:3 * D_A + Q_LORA + KV_LORA],
                           kr_cols, _rotate_half_cols(kr_cols),
                           jnp.zeros((D_MODEL, LANES - 2 * B_ROPE_DIM), F32)], axis=1).astype(BF16)
    wq = (w_uq[l] * ((B_NOPE_DIM + B_ROPE_DIM) ** -0.5 * LOG2E)).reshape(
        Q_LORA, B_HEADS, B_NOPE_DIM + B_ROPE_DIM)
    wq_rope = wq[:, :, B_NOPE_DIM:].reshape(Q_LORA, -1)
    wuq = jnp.concatenate([wq[:, :, :B_NOPE_DIM].reshape(Q_LORA, -1), wq_rope,
                           _rotate_half_cols(wq_rope)], axis=1).astype(BF16)
    wkv = w_ukv[l].reshape(KV_LORA, B_HEADS, B_NOPE_DIM + B_V_DIM)
    wukv = jnp.concatenate([wkv[:, :, :B_NOPE_DIM].reshape(KV_LORA, -1),
                            wkv[:, :, B_NOPE_DIM:].reshape(KV_LORA, -1)], axis=1).astype(BF16)
    cos_t, sin_t = _rope_tables()

    proj = _projections(
        x, sh1, sc1, g_norm1[l].reshape(1, -1), cos_t, sin_t, win, g_cq[l].reshape(1, -1), wuq,
        g_ckv[l].reshape(1, -1), wukv)
    qn, qr, kn, kr, vb = proj[3:8]

    out_a = _dilated_attention(proj[0:3], proj[8:11], _bias_tables(rel_bias))
    out_b = _latent_attention(qn, qr, kn, kr, vb)

    return _post(x, out_a, out_b, g1m, sh2, sc2, g2m, g_out_a[l].reshape(1, -1),
                 g_out_b[l].reshape(1, -1), w_out[l].astype(BF16), g_norm2[l].reshape(1, -1),
                 w_ffn_in[l].astype(BF16), w_ffn_out[l].astype(BF16), g_final.reshape(1, -1))
```

```python
import functools

import jax
import jax.numpy as jnp
import numpy as np
from jax import lax
from jax.experimental import pallas as pl
from jax.experimental.pallas import tpu as pltpu

F32 = jnp.float32
BF16 = jnp.bfloat16

D_MODEL = 1024
SEQ = 2048
A_HEADS = 8
A_HEAD_DIM = 64
D_A = A_HEADS * A_HEAD_DIM
DILATIONS = (1, 4, 16)
BLK = 128
B_HEADS = 8
B_NOPE_DIM = 64
B_ROPE_DIM = 32
B_V_DIM = 64
D_B = B_HEADS * B_V_DIM
Q_LORA = 384
KV_LORA = 256
ROPE_THETA = 10000.0
N_BUCKETS = 32
MAX_DISTANCE = 2048
D_FF = 2816
N_MOD = 6
EPS = 1e-6
NEG = -1e30
LOG2E = 1.4426950408889634

LANES = 128
N_PAIRS = A_HEADS // 2
VMEM_LIMIT = 56 * 1024 * 1024

_C_QA, _C_KA, _C_VA = 0, D_A, 2 * D_A
_C_CQ = 3 * D_A
_C_CKV = _C_CQ + Q_LORA
_C_KR = _C_CKV + KV_LORA
P_IN2 = _C_KR + LANES

BN_MOD = 1536
TM_PROJ = 1024
SUB_PROJ = 512
TM_POST = 1024
SUB_POST = 512
FF_CHUNK = 256
LOOKAHEAD = 2
PAIRS_PER_STEP = 2
LOOKAHEAD_MLA = 1
QB_MLA = 256
KC_MLA = 512


def _nt_dot(a, b):
    return lax.dot_general(a, b, (((1,), (1,)), ((), ())), preferred_element_type=F32)


def _rms(x, g):
    return x * lax.rsqrt(jnp.mean(x * x, axis=-1, keepdims=True) + EPS) * g


def _mod_kernel(c_ref, w_ref, b_ref, o_ref):
    cond = jax.nn.silu(c_ref[...])
    o_ref[...] = jnp.dot(cond.astype(BF16), w_ref[...].astype(BF16),
                         preferred_element_type=F32) + b_ref[...]


def _modulation(c, w_ada, b_ada):
    bsz = c.shape[0]
    n = w_ada.shape[1]
    bn = BN_MOD
    return pl.pallas_call(
        _mod_kernel,
        grid=(n // bn,),
        in_specs=[pl.BlockSpec((bsz, D_MODEL), lambda j: (0, 0)),
                  pl.BlockSpec((D_MODEL, bn), lambda j: (0, j)),
                  pl.BlockSpec((1, bn), lambda j: (0, j))],
        out_specs=pl.BlockSpec((bsz, bn), lambda j: (0, j)),
        out_shape=jax.ShapeDtypeStruct((bsz, n), F32),
        compiler_params=pltpu.CompilerParams(vmem_limit_bytes=VMEM_LIMIT),
        name="adaln_mod",
    )(c, w_ada, b_ada.reshape(1, n))


def _proj_kernel(x_ref, sh_ref, sc_ref, g1_ref, cos_ref, sin_ref, win_ref, gcq_ref, wuq_ref,
                 gckv_ref, wukv_ref,
                 qa_ref, ka_ref, va_ref, qn_ref, qr_ref, kn_ref, kr_ref, vb_ref,
                 qa16_ref, ka16_ref, va16_ref, fa_scr):
    for t in range(TM_PROJ // SUB_PROJ):
        rows = slice(t * SUB_PROJ, (t + 1) * SUB_PROJ)
        h = _rms(x_ref[0, rows, :], g1_ref[...]) * (1.0 + sc_ref[0]) + sh_ref[0]
        hb = h.astype(BF16)

        def proj(lo, hi):
            return jnp.dot(hb, win_ref[:, lo:hi], preferred_element_type=F32)

        def emit_orders(res, blk_ref, d16_ref):
            fa = fa_scr.at[t]
            n16 = SUB_PROJ // 16
            for g in range(D_A // LANES):
                lanes = slice(g * LANES, (g + 1) * LANES)
                fa[g] = res[:, lanes]
                eighth = BLK // 16
                for j in range(0, 16, 2):
                    pair = [fa[g, pl.ds(jj, n16, stride=16), :] for jj in (j, j + 1)]
                    for jj, x16 in zip((j, j + 1), pair):
                        d16_ref[0, jj, t * n16:(t + 1) * n16, lanes] = x16.astype(BF16)
                    for nb in range(SUB_PROJ // BLK):
                        two = [x16[nb * eighth:(nb + 1) * eighth] for x16 in pair]
                        row = t * SUB_PROJ + nb * BLK + j * eighth
                        blk_ref[0, row:row + 2 * eighth, lanes] = jnp.concatenate(two, axis=0).astype(BF16)

        emit_orders(proj(_C_QA, _C_KA), qa_ref, qa16_ref)
        emit_orders(proj(_C_KA, _C_VA), ka_ref, ka16_ref)
        emit_orders(proj(_C_VA, _C_CQ), va_ref, va16_ref)

        cos = cos_ref[rows, :]
        sin = sin_ref[rows, :]
        kr = proj(_C_KR, P_IN2)
        nr = B_ROPE_DIM
        kr = kr[:, :nr] * cos[:, :nr] + kr[:, nr:2 * nr] * sin[:, :nr]
        kr_ref[0, rows, :] = jnp.concatenate([kr] * (LANES // nr), axis=1).astype(BF16)

        cq = _rms(proj(_C_CQ, _C_CKV), gcq_ref[...]).astype(BF16)
        q = jnp.dot(cq, wuq_ref[...], preferred_element_type=F32)
        qn_ref[0, rows, :] = q[:, :D_B].astype(BF16)
        cos2 = jnp.concatenate([cos, cos], axis=1)
        sin2 = jnp.concatenate([sin, sin], axis=1)
        n_r = B_HEADS * B_ROPE_DIM
        qr_ref[0, rows, :] = (q[:, D_B:D_B + n_r] * cos2 + q[:, D_B + n_r:] * sin2).astype(BF16)

        ckv = _rms(proj(_C_CKV, _C_KR), gckv_ref[...]).astype(BF16)
        kv = jnp.dot(ckv, wukv_ref[...], preferred_element_type=F32)
        kn_ref[0, rows, :] = kv[:, :D_B].astype(BF16)
        vb_ref[0, rows, :] = kv[:, D_B:].astype(BF16)


def _const_spec(shape):
    nd = len(shape)
    return pl.BlockSpec(shape, lambda *_: (0,) * nd, pipeline_mode=pl.Buffered(1))


def _projections(x, sh1, sc1, g1, cos_t, sin_t, win, gcq, wuq, gckv, wukv):
    bsz = x.shape[0]
    tm = TM_PROJ
    tok = lambda w: pl.BlockSpec((1, tm, w), lambda b, j: (b, j, 0))
    per_b = pl.BlockSpec((1, 1, D_MODEL), lambda b, j: (b, 0, 0))
    tab = pl.BlockSpec((tm, LANES), lambda b, j: (j, 0))
    widths = (D_A, D_A, D_A, D_B, B_HEADS * B_ROPE_DIM, D_B, LANES, D_B)
    res_spec = lambda d: pl.BlockSpec((1, d, tm // d, D_A), lambda b, j: (b, 0, j, 0))
    res_shape = lambda d: jax.ShapeDtypeStruct((bsz, d, SEQ // d, D_A), BF16)
    chain_f32 = pltpu.VMEM((tm // SUB_PROJ, D_A // LANES, SUB_PROJ, LANES), F32)
    return pl.pallas_call(
        _proj_kernel,
        grid=(bsz, SEQ // tm),
        in_specs=[tok(D_MODEL), per_b, per_b, _const_spec((1, D_MODEL)), tab, tab,
                  _const_spec(win.shape), _const_spec((1, Q_LORA)), _const_spec(wuq.shape),
                  _const_spec((1, KV_LORA)), _const_spec(wukv.shape)],
        out_specs=[tok(w) for w in widths] + [res_spec(16)] * 3,
        out_shape=[jax.ShapeDtypeStruct((bsz, SEQ, w), BF16) for w in widths]
        + [res_shape(16)] * 3,
        scratch_shapes=[chain_f32],
        compiler_params=pltpu.CompilerParams(
            dimension_semantics=("parallel", "parallel"), vmem_limit_bytes=VMEM_LIMIT),
        name="in_proj",
    )(x, sh1, sc1, g1, cos_t, sin_t, win, gcq, wuq, gckv, wukv)


def _stack_rows(ref, c, row):
    return jnp.concatenate([ref[c, 0, pl.ds(row, BLK), :], ref[c, 1, pl.ds(row, BLK), :]], axis=0)


def _softmax_update(s, vb, state):
    nk = s.shape[1]
    m_cur = jnp.max(s, axis=1, keepdims=True)
    if state is None:
        m_new = jnp.broadcast_to(m_cur, (2 * BLK, LANES))
    else:
        m0, l0, acc0 = state
        m_new = jnp.maximum(m0, m_cur)
    m_wide = m_new if nk == LANES else jnp.concatenate([m_new] * (nk // LANES), axis=1)
    p = jnp.exp2(s - m_wide)
    v_ones = jnp.concatenate([vb, jnp.ones((nk, LANES), BF16)], axis=1)
    pv = jnp.dot(p.astype(BF16), v_ones, preferred_element_type=F32)
    acc, l_new = pv[:, :LANES], pv[:, LANES:]
    if state is not None:
        alpha = jnp.exp2(m0 - m_new)
        l_new = alpha * l0 + l_new
        acc = alpha * acc0 + acc
    return m_new, l_new, acc


def _dil_kernel(q1_ref, k1_ref, v1_ref, q16_ref, k16_ref, v16_ref, tab_ref, o_ref, st4, st1, ost):
    lane = lax.broadcasted_iota(jnp.int32, (BLK, LANES), 1)
    head_a = lane < A_HEAD_DIM
    n4 = SEQ // 4
    quarter, eighth = BLK // 4, BLK // 16

    def halves(state):
        return [(c, h, x[h * BLK:(h + 1) * BLK]) for c, x in enumerate(state) for h in range(2)]

    def sink16(s4, j, state):
        for c, h, x in halves(state):
            for n in range(4):
                dst = (j % 4) * n4 + n * BLK + (j // 4) * quarter
                s4[c, h, dst:dst + quarter, :] = x[n * quarter:(n + 1) * quarter]

    def sink4(s1, r, n, state):
        for c, h, x in halves(state):
            for q in range(4):
                for a in range(4):
                    dst = (4 * n + a) * BLK + (4 * q + r) * eighth
                    s1[c, h, dst:dst + eighth, :] = x[q * quarter + a * eighth:q * quarter + (a + 1) * eighth]

    def load_state(src, row):
        return tuple(_stack_rows(src, c, row) for c in range(3))

    def finish(pp, n, state):
        _, l, acc = state
        o = acc / l
        o = jnp.where(head_a, o[:BLK], o[BLK:])
        stage = ost.at[pp]
        for j in range(16):
            stage[pl.ds(n * BLK + j, eighth, stride=16), :] = o[j * eighth:(j + 1) * eighth]
        o_ref[0, n * BLK:(n + 1) * BLK, pp * LANES:(pp + 1) * LANES] = stage[n * BLK:(n + 1) * BLK, :].astype(BF16)

    def rows16(src16, r, blocks, lanes):
        return jnp.concatenate([src16[0, 4 * q + r, n * quarter:(n + 1) * quarter, lanes]
                                for n in blocks for q in range(4)], axis=0)

    def pair_blocks(pp):
        lanes = slice(pp * LANES, (pp + 1) * LANES)
        s4, s1 = st4.at[pp], st1.at[pp]
        out = []
        for j in range(SEQ // BLK):
            get = lambda ref, j=j: ref[0, j, :, lanes]
            out.append((pp, 2, functools.partial(get, q16_ref), functools.partial(get, k16_ref),
                        functools.partial(get, v16_ref), True, None, 0, functools.partial(sink16, s4, j)))
        for n in range(n4 // BLK):
            for r in range(4):
                keys = (n,) if n == 0 else (n - 1, n)
                out.append((pp, 1, functools.partial(rows16, q16_ref, r, (n,), lanes),
                            functools.partial(rows16, k16_ref, r, keys, lanes),
                            functools.partial(rows16, v16_ref, r, keys, lanes),
                            n == 0, s4, r * n4 + n * BLK, functools.partial(sink4, s1, r, n)))
        for n in range(SEQ // BLK):
            lo = n * BLK if n == 0 else (n - 1) * BLK
            get = lambda ref, a, b: ref[0, a:b, lanes]
            out.append((pp, 0, functools.partial(get, q1_ref, n * BLK, (n + 1) * BLK),
                        functools.partial(get, k1_ref, lo, (n + 1) * BLK),
                        functools.partial(get, v1_ref, lo, (n + 1) * BLK),
                        n == 0, s1, n * BLK, functools.partial(finish, pp, n)))
        return out

    def logits(blk):
        pp, br, q_rows, k_rows, _, first, _, _, _ = blk
        q = q_rows()
        zero = jnp.zeros_like(q)
        qst = jnp.concatenate([jnp.where(head_a, q, zero), jnp.where(head_a, zero, q)], axis=0)
        tab = tab_ref[pp, br, :, BLK:] if first else tab_ref[pp, br]
        return _nt_dot(qst, k_rows()) + tab

    def update(blk, s):
        _, _, _, _, v_rows, _, st_src, st_row, sink = blk
        state = None if st_src is None else load_state(st_src, st_row)
        sink(_softmax_update(s, v_rows(), state))

    per_branch = SEQ // BLK
    blocks = sorted((blk for pp in range(PAIRS_PER_STEP) for blk in enumerate(pair_blocks(pp))),
                    key=lambda ib: (ib[0] + per_branch * ib[1][0], ib[1][0]))
    pending = []
    for _, blk in blocks:
        pending.append((blk, logits(blk)))
        if len(pending) > LOOKAHEAD:
            update(*pending.pop(0))
    for item in pending:
        update(*item)


def _dilated_attention(qkv1, qkv16, tables):
    bsz = qkv1[0].shape[0]
    w = PAIRS_PER_STEP * LANES
    pairs = pl.BlockSpec((1, SEQ, w), lambda p, b: (b, 0, p))
    pairs16 = pl.BlockSpec((1, 16, SEQ // 16, w), lambda p, b: (b, 0, 0, p))
    state = pltpu.VMEM((PAIRS_PER_STEP, 3, 2, SEQ, LANES), F32)
    return pl.pallas_call(
        _dil_kernel,
        grid=(N_PAIRS // PAIRS_PER_STEP, bsz),
        in_specs=[pairs] * 3 + [pairs16] * 3 + [
            pl.BlockSpec((PAIRS_PER_STEP, 3, 2 * BLK, 2 * BLK), lambda p, b: (p, 0, 0, 0))],
        out_specs=pairs,
        out_shape=jax.ShapeDtypeStruct((bsz, SEQ, D_A), BF16),
        scratch_shapes=[state, state, pltpu.VMEM((PAIRS_PER_STEP, SEQ, LANES), F32)],
        compiler_params=pltpu.CompilerParams(
            dimension_semantics=("parallel", "parallel"), vmem_limit_bytes=VMEM_LIMIT),
        name="dilated_attn",
    )(*qkv1, *qkv16, tables)


def _mla_kernel(qn_ref, qr_ref, kn_ref, kr_ref, v_ref, o_ref):
    qb = QB_MLA
    kc = KC_MLA
    n_qb = SEQ // qb
    lane = lax.broadcasted_iota(jnp.int32, (qb, LANES), 1)
    head_a = lane < B_NOPE_DIM
    slot = lane // B_ROPE_DIM

    def stacked_q(h, p):
        rows = slice(h * qb, (h + 1) * qb)
        qn = qn_ref[0, rows, p * LANES:(p + 1) * LANES]
        qr = qr_ref[0, rows, (p // 2) * LANES:(p // 2 + 1) * LANES]
        zero = jnp.zeros_like(qn)
        sa = 2 * (p % 2)
        q_a = jnp.concatenate([jnp.where(head_a, qn, zero), jnp.where(slot == sa, qr, zero)], axis=1)
        q_b = jnp.concatenate([jnp.where(head_a, zero, qn), jnp.where(slot == sa + 1, qr, zero)], axis=1)
        return jnp.concatenate([q_a, q_b], axis=0)

    chains = [(h, p) for h in range(n_qb) for p in range(N_PAIRS)]

    def logits(chain, keys, diagonal):
        cols = slice(chain[1] * LANES, (chain[1] + 1) * LANES)
        s = _nt_dot(stacked_q(*chain), jnp.concatenate([kn_ref[0, keys, cols], kr_ref[0, keys, :]], axis=1))
        if diagonal:
            qpos = lax.broadcasted_iota(jnp.int32, (2 * qb, qb), 0) % qb
            kpos = lax.broadcasted_iota(jnp.int32, (2 * qb, qb), 1)
            tail = jnp.where(qpos >= kpos, s[:, -qb:], NEG)
            s = tail if s.shape[1] == qb else jnp.concatenate([s[:, :-qb], tail], axis=1)
        return s

    def update(chain, keys, s, state):
        nk = s.shape[1]
        p = chain[1]
        m_cur = jnp.max(s, axis=1, keepdims=True)
        m_new = jnp.broadcast_to(m_cur, (2 * qb, LANES)) if state is None else jnp.maximum(state[0], m_cur)
        pr = jnp.exp2(s - jnp.concatenate([m_new] * (nk // LANES), axis=1))
        v_ones = jnp.concatenate([v_ref[0, keys, p * LANES:(p + 1) * LANES], jnp.ones((nk, LANES), BF16)],
                                 axis=1)
        pv = jnp.dot(pr.astype(BF16), v_ones, preferred_element_type=F32)
        acc, l_new = pv[:, :LANES], pv[:, LANES:]
        if state is None:
            return m_new, l_new, acc
        alpha = jnp.exp2(state[0] - m_new)
        return m_new, alpha * state[1] + l_new, alpha * state[2] + acc

    ends = [(h + 1) * qb for h in range(n_qb)]
    steps = []
    for lo in range(0, SEQ, kc):
        for h, end in enumerate(ends):
            if lo < end:
                steps += [((h, p), slice(lo, min(lo + kc, end)), end) for p in range(N_PAIRS)]
    states = {chain: None for chain in chains}
    pending = []
    for chain, keys, end in steps:
        pending.append((chain, keys, logits(chain, keys, keys.stop == end)))
        if len(pending) > LOOKAHEAD_MLA:
            c0, k0, s0 = pending.pop(0)
            states[c0] = update(c0, k0, s0, states[c0])
    for c0, k0, s0 in pending:
        states[c0] = update(c0, k0, s0, states[c0])
    for (h, p), (_, l, acc) in states.items():
        o = acc / l
        o_ref[0, h * qb:(h + 1) * qb, p * LANES:(p + 1) * LANES] = jnp.where(
            head_a, o[:qb], o[qb:]).astype(BF16)


def _latent_attention(qn, qr, kn, kr, vb):
    bsz = qn.shape[0]
    blk = lambda w: pl.BlockSpec((1, SEQ, w), lambda b: (b, 0, 0))
    return pl.pallas_call(
        _mla_kernel,
        grid=(bsz,),
        in_specs=[blk(D_B), blk(B_HEADS * B_ROPE_DIM), blk(D_B), blk(LANES), blk(D_B)],
        out_specs=blk(D_B),
        out_shape=jax.ShapeDtypeStruct((bsz, SEQ, D_B), BF16),
        compiler_params=pltpu.CompilerParams(
            dimension_semantics=("parallel",), vmem_limit_bytes=VMEM_LIMIT),
        name="latent_attn",
    )(qn, qr, kn, kr, vb)


def _post_kernel(x_ref, oa_ref, ob_ref, g1_ref, sh2_ref, sc2_ref, g2_ref, goa_ref, gob_ref,
                 wout_ref, gn2_ref, wfi_ref, wfo_ref, gfin_ref, o_ref, a_scr):
    chains = [slice(t * SUB_POST, (t + 1) * SUB_POST) for t in range(TM_POST // SUB_POST)]

    def mixed(rows):
        ya = _rms(oa_ref[0, rows, :].astype(F32), goa_ref[...])
        yb = _rms(ob_ref[0, rows, :].astype(F32), gob_ref[...])
        y = jnp.concatenate([ya, yb], axis=1).astype(BF16)
        return jnp.dot(y, wout_ref[...], preferred_element_type=F32)

    def ffn(rows, x1):
        h2 = (_rms(x1, gn2_ref[...]) * (1.0 + sc2_ref[0]) + sh2_ref[0]).astype(BF16)
        for c in range(D_FF // FF_CHUNK):
            lo = c * FF_CHUNK
            g = jnp.dot(h2, wfi_ref[:, lo:lo + FF_CHUNK], preferred_element_type=F32)
            u = jnp.dot(h2, wfi_ref[:, D_FF + lo:D_FF + lo + FF_CHUNK], preferred_element_type=F32)
            a_scr[rows, lo:lo + FF_CHUNK] = (jax.nn.silu(g) * u).astype(BF16)
        return jnp.dot(a_scr[rows, :], wfo_ref[...], preferred_element_type=F32)

    mix = [mixed(rows) for rows in chains]
    x1 = [x_ref[0, rows, :] + g1_ref[0] * m for rows, m in zip(chains, mix)]
    f = [ffn(rows, x) for rows, x in zip(chains, x1)]
    for rows, x, y in zip(chains, x1, f):
        o_ref[0, rows, :] = _rms(x + g2_ref[0] * y, gfin_ref[...])


def _post(x, oa, ob, g1m, sh2, sc2, g2m, goa, gob, wout, gn2, wfi, wfo, gfin):
    bsz = x.shape[0]
    tm = TM_POST
    tok = lambda w: pl.BlockSpec((1, tm, w), lambda b, j: (b, j, 0))
    per_b = pl.BlockSpec((1, 1, D_MODEL), lambda b, j: (b, 0, 0))
    return pl.pallas_call(
        _post_kernel,
        grid=(bsz, SEQ // tm),
        in_specs=[tok(D_MODEL), tok(D_A), tok(D_B), per_b, per_b, per_b, per_b,
                  _const_spec((1, D_A)), _const_spec((1, D_B)), _const_spec(wout.shape),
                  _const_spec((1, D_MODEL)), _const_spec(wfi.shape), _const_spec(wfo.shape),
                  _const_spec((1, D_MODEL))],
        out_specs=tok(D_MODEL),
        out_shape=jax.ShapeDtypeStruct((bsz, SEQ, D_MODEL), F32),
        scratch_shapes=[pltpu.VMEM((tm, D_FF), BF16)],
        compiler_params=pltpu.CompilerParams(
            dimension_semantics=("parallel", "parallel"), vmem_limit_bytes=VMEM_LIMIT),
        name="out_proj_ffn",
    )(x, oa, ob, g1m, sh2, sc2, g2m, goa, gob, wout, gn2, wfi, wfo, gfin)


def _rotate_half_cols(w):
    lead = w.shape[:-1]
    half = B_ROPE_DIM // 2
    w = w.reshape(lead + (-1, 2, half))
    return jnp.stack([-w[..., 1, :], w[..., 0, :]], axis=-2).reshape(lead + (-1,))


def _t5_bucket(dist):
    max_exact = N_BUCKETS // 2
    d = np.maximum(dist, 1).astype(np.float64)
    large = max_exact + (np.log(d / max_exact) / np.log(MAX_DISTANCE / max_exact)
                         * (N_BUCKETS - max_exact)).astype(np.int64)
    large = np.minimum(large, N_BUCKETS - 1)
    return np.where(dist < max_exact, dist, large).astype(np.int32)


def _grouped_order(groups):
    rows = np.arange(BLK)
    per = BLK // groups
    return groups * (rows % per) + rows // per


def _bias_kernel(bucket_ref, rb_ref, o_ref):
    p = pl.program_id(0)
    for br in range(len(DILATIONS)):
        bucket = bucket_ref[br]
        tabs = [jnp.full((BLK, 2 * BLK), NEG, F32)] * 2
        for k in range(N_BUCKETS):
            hit = bucket == k
            tabs = [jnp.where(hit, rb_ref[k, 2 * p + hh] * LOG2E, tabs[hh]) for hh in range(2)]
        for hh in range(2):
            o_ref[0, br, hh * BLK:(hh + 1) * BLK, :] = tabs[hh]


def _bias_tables(rel_bias):
    a = np.arange(BLK)[:, None]
    bk = np.arange(2 * BLK)[None, :]
    steps = BLK + a - bk
    valid = (steps >= 0) & (steps <= BLK)
    bucket = np.stack([np.where(valid, _t5_bucket(np.clip(steps, 0, BLK) * dil), -1)
                       for dil in DILATIONS]).astype(np.int32)
    for br, groups in ((0, 16), (1, 4)):
        seq = _grouped_order(groups)
        bucket[br] = bucket[br][seq][:, np.concatenate([seq, BLK + seq])]
    nbr = len(DILATIONS)
    return pl.pallas_call(
        _bias_kernel,
        grid=(N_PAIRS,),
        in_specs=[pl.BlockSpec((nbr, BLK, 2 * BLK), lambda p: (0, 0, 0)),
                  pl.BlockSpec(memory_space=pltpu.SMEM)],
        out_specs=pl.BlockSpec((1, nbr, 2 * BLK, 2 * BLK), lambda p: (p, 0, 0, 0)),
        out_shape=jax.ShapeDtypeStruct((N_PAIRS, nbr, 2 * BLK, 2 * BLK), F32),
        name="bias_tables",
    )(jnp.asarray(bucket), rel_bias)


def _rope_tables():
    half = B_ROPE_DIM // 2
    inv = ROPE_THETA ** (-jnp.arange(half, dtype=F32) / half)
    ang = jnp.arange(SEQ, dtype=F32)[:, None] * inv[None, :]
    reps = LANES // half
    return jnp.tile(jnp.cos(ang), (1, reps)), jnp.tile(jnp.sin(ang), (1, reps))


def kernel(x, c, w_ada, b_ada, g_norm1, w_in, g_cq, w_uq, g_ckv, w_ukv, rel_bias, g_out_a, g_out_b,
           w_out, g_norm2, w_ffn_in, w_ffn_out, g_final):
    bsz = x.shape[0]
    assert x.shape == (bsz, SEQ, D_MODEL) and w_ada.shape[0] == 1
    l = 0

    mod = _modulation(c, w_ada[l], b_ada[l]).reshape(bsz, N_MOD, 1, D_MODEL)
    sh1, sc1, g1m, sh2, sc2, g2m = (mod[:, i] for i in range(N_MOD))

    wi = w_in[l]
    kr_cols = wi[:, 3 * D_A + Q_LORA + KV_LORA:]
    win = jnp.concatenate([wi[:, :D_A] * (A_HEAD_DIM ** -0.5 * LOG2E), wi[:, D_A:3 * D_A + Q_LORA + KV_LORA],
                           kr_cols, _rotate_half_cols(kr_cols),
                           jnp.zeros((D_MODEL, LANES - 2 * B_ROPE_DIM), F32)], axis=1).astype(BF16)
    wq = (w_uq[l] * ((B_NOPE_DIM + B_ROPE_DIM) ** -0.5 * LOG2E)).reshape(
        Q_LORA, B_HEADS, B_NOPE_DIM + B_ROPE_DIM)
    wq_rope = wq[:, :, B_NOPE_DIM:].reshape(Q_LORA, -1)
    wuq = jnp.concatenate([wq[:, :, :B_NOPE_DIM].reshape(Q_LORA, -1), wq_rope,
                           _rotate_half_cols(wq_rope)], axis=1).astype(BF16)
    wkv = w_ukv[l].reshape(KV_LORA, B_HEADS, B_NOPE_DIM + B_V_DIM)
    wukv = jnp.concatenate([wkv[:, :, :B_NOPE_DIM].reshape(KV_LORA, -1),
                            wkv[:, :, B_NOPE_DIM:].reshape(KV_LORA, -1)], axis=1).astype(BF16)
    cos_t, sin_t = _rope_tables()

    proj = _projections(
        x, sh1, sc1, g_norm1[l].reshape(1, -1), cos_t, sin_t, win, g_cq[l].reshape(1, -1), wuq,
        g_ckv[l].reshape(1, -1), wukv)
    qn, qr, kn, kr, vb = proj[3:8]

    out_a = _dilated_attention(proj[0:3], proj[8:11], _bias_tables(rel_bias))
    out_b = _latent_attention(qn, qr, kn, kr, vb)

    return _post(x, out_a, out_b, g1m, sh2, sc2, g2m, g_out_a[l].reshape(1, -1),
                 g_out_b[l].reshape(1, -1), w_out[l].astype(BF16), g_norm2[l].reshape(1, -1),
                 w_ffn_in[l].astype(BF16), w_ffn_out[l].astype(BF16), g_final.reshape(1, -1))
```

```python
import functools

import jax
import jax.numpy as jnp
import numpy as np
from jax import lax
from jax.experimental import pallas as pl
from jax.experimental.pallas import tpu as pltpu

F32 = jnp.float32
BF16 = jnp.bfloat16

D_MODEL = 1024
SEQ = 2048
A_HEADS = 8
A_HEAD_DIM = 64
D_A = A_HEADS * A_HEAD_DIM
DILATIONS = (1, 4, 16)
BLK = 128
B_HEADS = 8
B_NOPE_DIM = 64
B_ROPE_DIM = 32
B_V_DIM = 64
D_B = B_HEADS * B_V_DIM
Q_LORA = 384
KV_LORA = 256
ROPE_THETA = 10000.0
N_BUCKETS = 32
MAX_DISTANCE = 2048
D_FF = 2816
N_MOD = 6
EPS = 1e-6
NEG = -1e30
LOG2E = 1.4426950408889634

LANES = 128
N_PAIRS = A_HEADS // 2
VMEM_LIMIT = 56 * 1024 * 1024

_C_QA, _C_KA, _C_VA = 0, D_A, 2 * D_A
_C_CQ = 3 * D_A
_C_CKV = _C_CQ + Q_LORA
_C_KR = _C_CKV + KV_LORA
P_IN2 = _C_KR + LANES

BN_MOD = 1536
TM_PROJ = 1024
SUB_PROJ = 512
TM_POST = 1024
SUB_POST = 512
FF_CHUNK = 256
LOOKAHEAD = 2
PAIRS_PER_STEP = 2
LOOKAHEAD_MLA = 1
QB_MLA = 256
KC_MLA = 512


def _nt_dot(a, b):
    return lax.dot_general(a, b, (((1,), (1,)), ((), ())), preferred_element_type=F32)


def _rms(x, g):
    return x * lax.rsqrt(jnp.mean(x * x, axis=-1, keepdims=True) + EPS) * g


def _mod_kernel(c_ref, w_ref, b_ref, o_ref):
    cond = jax.nn.silu(c_ref[...])
    o_ref[...] = jnp.dot(cond.astype(BF16), w_ref[...].astype(BF16),
                         preferred_element_type=F32) + b_ref[...]


def _modulation(c, w_ada, b_ada):
    bsz = c.shape[0]
    n = w_ada.shape[1]
    bn = BN_MOD
    return pl.pallas_call(
        _mod_kernel,
        grid=(n // bn,),
        in_specs=[pl.BlockSpec((bsz, D_MODEL), lambda j: (0, 0)),
                  pl.BlockSpec((D_MODEL, bn), lambda j: (0, j)),
                  pl.BlockSpec((1, bn), lambda j: (0, j))],
        out_specs=pl.BlockSpec((bsz, bn), lambda j: (0, j)),
        out_shape=jax.ShapeDtypeStruct((bsz, n), F32),
        compiler_params=pltpu.CompilerParams(vmem_limit_bytes=VMEM_LIMIT),
        name="adaln_mod",
    )(c, w_ada, b_ada.reshape(1, n))


def _proj_kernel(x_ref, sh_ref, sc_ref, g1_ref, cos_ref, sin_ref, win_ref, gcq_ref, wuq_ref,
                 gckv_ref, wukv_ref,
                 qa_ref, ka_ref, va_ref, qn_ref, qr_ref, kn_ref, kr_ref, vb_ref,
                 qa16_ref, ka16_ref, va16_ref, fa_scr, fb_scr):
    for t in range(TM_PROJ // SUB_PROJ):
        rows = slice(t * SUB_PROJ, (t + 1) * SUB_PROJ)
        h = _rms(x_ref[0, rows, :], g1_ref[...]) * (1.0 + sc_ref[0]) + sh_ref[0]
        hb = h.astype(BF16)

        def proj(lo, hi):
            return jnp.dot(hb, win_ref[:, lo:hi], preferred_element_type=F32)

        def emit_orders(res, blk_ref, d16_ref):
            fa, fb = fa_scr.at[t], fb_scr.at[t]
            n4, n16 = SUB_PROJ // 4, SUB_PROJ // 16
            for g in range(D_A // LANES):
                lanes = slice(g * LANES, (g + 1) * LANES)
                fa[g] = res[:, lanes]
                for r in range(4):
                    fb[g, r * n4:(r + 1) * n4, :] = fa[g, pl.ds(r, n4, stride=4), :]
                for j in range(16):
                    x16 = fb[g, pl.ds((j % 4) * n4 + j // 4, n16, stride=4), :]
                    d16_ref[0, j, t * n16:(t + 1) * n16, lanes] = x16.astype(BF16)
                for nb in range(SUB_PROJ // BLK):
                    for j in range(0, 16, 2):
                        two = [fb[g, pl.ds((jj % 4) * n4 + nb * (BLK // 4) + jj // 4, BLK // 16, stride=4), :]
                               for jj in (j, j + 1)]
                        row = t * SUB_PROJ + nb * BLK + j * (BLK // 16)
                        blk_ref[0, row:row + 2 * (BLK // 16), lanes] = jnp.concatenate(two, axis=0).astype(BF16)

        emit_orders(proj(_C_QA, _C_KA), qa_ref, qa16_ref)
        emit_orders(proj(_C_KA, _C_VA), ka_ref, ka16_ref)
        emit_orders(proj(_C_VA, _C_CQ), va_ref, va16_ref)

        cos = cos_ref[rows, :]
        sin = sin_ref[rows, :]
        kr = proj(_C_KR, P_IN2)
        nr = B_ROPE_DIM
        kr = kr[:, :nr] * cos[:, :nr] + kr[:, nr:2 * nr] * sin[:, :nr]
        kr_ref[0, rows, :] = jnp.concatenate([kr] * (LANES // nr), axis=1).astype(BF16)

        cq = _rms(proj(_C_CQ, _C_CKV), gcq_ref[...]).astype(BF16)
        q = jnp.dot(cq, wuq_ref[...], preferred_element_type=F32)
        qn_ref[0, rows, :] = q[:, :D_B].astype(BF16)
        cos2 = jnp.concatenate([cos, cos], axis=1)
        sin2 = jnp.concatenate([sin, sin], axis=1)
        n_r = B_HEADS * B_ROPE_DIM
        qr_ref[0, rows, :] = (q[:, D_B:D_B + n_r] * cos2 + q[:, D_B + n_r:] * sin2).astype(BF16)

        ckv = _rms(proj(_C_CKV, _C_KR), gckv_ref[...]).astype(BF16)
        kv = jnp.dot(ckv, wukv_ref[...], preferred_element_type=F32)
        kn_ref[0, rows, :] = kv[:, :D_B].astype(BF16)
        vb_ref[0, rows, :] = kv[:, D_B:].astype(BF16)


def _const_spec(shape):
    nd = len(shape)
    return pl.BlockSpec(shape, lambda *_: (0,) * nd, pipeline_mode=pl.Buffered(1))


def _projections(x, sh1, sc1, g1, cos_t, sin_t, win, gcq, wuq, gckv, wukv):
    bsz = x.shape[0]
    tm = TM_PROJ
    tok = lambda w: pl.BlockSpec((1, tm, w), lambda b, j: (b, j, 0))
    per_b = pl.BlockSpec((1, 1, D_MODEL), lambda b, j: (b, 0, 0))
    tab = pl.BlockSpec((tm, LANES), lambda b, j: (j, 0))
    widths = (D_A, D_A, D_A, D_B, B_HEADS * B_ROPE_DIM, D_B, LANES, D_B)
    res_spec = lambda d: pl.BlockSpec((1, d, tm // d, D_A), lambda b, j: (b, 0, j, 0))
    res_shape = lambda d: jax.ShapeDtypeStruct((bsz, d, SEQ // d, D_A), BF16)
    chain_f32 = pltpu.VMEM((tm // SUB_PROJ, D_A // LANES, SUB_PROJ, LANES), F32)
    return pl.pallas_call(
        _proj_kernel,
        grid=(bsz, SEQ // tm),
        in_specs=[tok(D_MODEL), per_b, per_b, _const_spec((1, D_MODEL)), tab, tab,
                  _const_spec(win.shape), _const_spec((1, Q_LORA)), _const_spec(wuq.shape),
                  _const_spec((1, KV_LORA)), _const_spec(wukv.shape)],
        out_specs=[tok(w) for w in widths] + [res_spec(16)] * 3,
        out_shape=[jax.ShapeDtypeStruct((bsz, SEQ, w), BF16) for w in widths]
        + [res_shape(16)] * 3,
        scratch_shapes=[chain_f32, chain_f32],
        compiler_params=pltpu.CompilerParams(
            dimension_semantics=("parallel", "parallel"), vmem_limit_bytes=VMEM_LIMIT),
        name="in_proj",
    )(x, sh1, sc1, g1, cos_t, sin_t, win, gcq, wuq, gckv, wukv)


def _stack_rows(ref, c, row):
    return jnp.concatenate([ref[c, 0, pl.ds(row, BLK), :], ref[c, 1, pl.ds(row, BLK), :]], axis=0)


def _softmax_update(s, vb, state):
    nk = s.shape[1]
    m_cur = jnp.max(s, axis=1, keepdims=True)
    if state is None:
        m_new = jnp.broadcast_to(m_cur, (2 * BLK, LANES))
    else:
        m0, l0, acc0 = state
        m_new = jnp.maximum(m0, m_cur)
    m_wide = m_new if nk == LANES else jnp.concatenate([m_new] * (nk // LANES), axis=1)
    p = jnp.exp2(s - m_wide)
    v_ones = jnp.concatenate([vb, jnp.ones((nk, LANES), BF16)], axis=1)
    pv = jnp.dot(p.astype(BF16), v_ones, preferred_element_type=F32)
    acc, l_new = pv[:, :LANES], pv[:, LANES:]
    if state is not None:
        alpha = jnp.exp2(m0 - m_new)
        l_new = alpha * l0 + l_new
        acc = alpha * acc0 + acc
    return m_new, l_new, acc


def _dil_kernel(q1_ref, k1_ref, v1_ref, q16_ref, k16_ref, v16_ref, tab_ref, o_ref, st4, st1, ost):
    lane = lax.broadcasted_iota(jnp.int32, (BLK, LANES), 1)
    head_a = lane < A_HEAD_DIM
    n4 = SEQ // 4
    quarter, eighth = BLK // 4, BLK // 16

    def halves(state):
        return [(c, h, x[h * BLK:(h + 1) * BLK]) for c, x in enumerate(state) for h in range(2)]

    def sink16(s4, j, state):
        for c, h, x in halves(state):
            for n in range(4):
                dst = (j % 4) * n4 + n * BLK + (j // 4) * quarter
                s4[c, h, dst:dst + quarter, :] = x[n * quarter:(n + 1) * quarter]

    def sink4(s1, r, n, state):
        for c, h, x in halves(state):
            for q in range(4):
                for a in range(4):
                    dst = (4 * n + a) * BLK + (4 * q + r) * eighth
                    s1[c, h, dst:dst + eighth, :] = x[q * quarter + a * eighth:q * quarter + (a + 1) * eighth]

    def load_state(src, row):
        return tuple(_stack_rows(src, c, row) for c in range(3))

    def finish(pp, n, state):
        _, l, acc = state
        o = jnp.where(head_a, acc[:BLK], acc[BLK:]) / jnp.where(head_a, l[:BLK], l[BLK:])
        stage = ost.at[pp]
        for j in range(16):
            stage[pl.ds(n * BLK + j, eighth, stride=16), :] = o[j * eighth:(j + 1) * eighth]
        o_ref[0, n * BLK:(n + 1) * BLK, pp * LANES:(pp + 1) * LANES] = stage[n * BLK:(n + 1) * BLK, :].astype(BF16)

    def rows16(src16, r, blocks, lanes):
        return jnp.concatenate([src16[0, 4 * q + r, n * quarter:(n + 1) * quarter, lanes]
                                for n in blocks for q in range(4)], axis=0)

    def pair_blocks(pp):
        lanes = slice(pp * LANES, (pp + 1) * LANES)
        s4, s1 = st4.at[pp], st1.at[pp]
        out = []
        for j in range(SEQ // BLK):
            get = lambda ref, j=j: ref[0, j, :, lanes]
            out.append((pp, 2, functools.partial(get, q16_ref), functools.partial(get, k16_ref),
                        functools.partial(get, v16_ref), True, None, 0, functools.partial(sink16, s4, j)))
        for n in range(n4 // BLK):
            for r in range(4):
                keys = (n,) if n == 0 else (n - 1, n)
                out.append((pp, 1, functools.partial(rows16, q16_ref, r, (n,), lanes),
                            functools.partial(rows16, k16_ref, r, keys, lanes),
                            functools.partial(rows16, v16_ref, r, keys, lanes),
                            n == 0, s4, r * n4 + n * BLK, functools.partial(sink4, s1, r, n)))
        for n in range(SEQ // BLK):
            lo = n * BLK if n == 0 else (n - 1) * BLK
            get = lambda ref, a, b: ref[0, a:b, lanes]
            out.append((pp, 0, functools.partial(get, q1_ref, n * BLK, (n + 1) * BLK),
                        functools.partial(get, k1_ref, lo, (n + 1) * BLK),
                        functools.partial(get, v1_ref, lo, (n + 1) * BLK),
                        n == 0, s1, n * BLK, functools.partial(finish, pp, n)))
        return out

    def logits(blk):
        pp, br, q_rows, k_rows, _, first, _, _, _ = blk
        q = q_rows()
        zero = jnp.zeros_like(q)
        qst = jnp.concatenate([jnp.where(head_a, q, zero), jnp.where(head_a, zero, q)], axis=0)
        tab = tab_ref[pp, br, :, BLK:] if first else tab_ref[pp, br]
        return _nt_dot(qst, k_rows()) + tab

    def update(blk, s):
        _, _, _, _, v_rows, _, st_src, st_row, sink = blk
        state = None if st_src is None else load_state(st_src, st_row)
        sink(_softmax_update(s, v_rows(), state))

    per_branch = SEQ // BLK
    blocks = sorted((blk for pp in range(PAIRS_PER_STEP) for blk in enumerate(pair_blocks(pp))),
                    key=lambda ib: (ib[0] + per_branch * ib[1][0], ib[1][0]))
    pending = []
    for _, blk in blocks:
        pending.append((blk, logits(blk)))
        if len(pending) > LOOKAHEAD:
            update(*pending.pop(0))
    for item in pending:
        update(*item)


def _dilated_attention(qkv1, qkv16, tables):
    bsz = qkv1[0].shape[0]
    w = PAIRS_PER_STEP * LANES
    pairs = pl.BlockSpec((1, SEQ, w), lambda p, b: (b, 0, p))
    pairs16 = pl.BlockSpec((1, 16, SEQ // 16, w), lambda p, b: (b, 0, 0, p))
    state = pltpu.VMEM((PAIRS_PER_STEP, 3, 2, SEQ, LANES), F32)
    return pl.pallas_call(
        _dil_kernel,
        grid=(N_PAIRS // PAIRS_PER_STEP, bsz),
        in_specs=[pairs] * 3 + [pairs16] * 3 + [
            pl.BlockSpec((PAIRS_PER_STEP, 3, 2 * BLK, 2 * BLK), lambda p, b: (p, 0, 0, 0))],
        out_specs=pairs,
        out_shape=jax.ShapeDtypeStruct((bsz, SEQ, D_A), BF16),
        scratch_shapes=[state, state, pltpu.VMEM((PAIRS_PER_STEP, SEQ, LANES), F32)],
        compiler_params=pltpu.CompilerParams(
            dimension_semantics=("parallel", "parallel"), vmem_limit_bytes=VMEM_LIMIT),
        name="dilated_attn",
    )(*qkv1, *qkv16, tables)


def _mla_kernel(qn_ref, qr_ref, kn_ref, kr_ref, v_ref, o_ref):
    qb = QB_MLA
    kc = KC_MLA
    n_qb = SEQ // qb
    lane = lax.broadcasted_iota(jnp.int32, (qb, LANES), 1)
    head_a = lane < B_NOPE_DIM
    slot = lane // B_ROPE_DIM

    def stacked_q(h, p):
        rows = slice(h * qb, (h + 1) * qb)
        qn = qn_ref[0, rows, p * LANES:(p + 1) * LANES]
        qr = qr_ref[0, rows, (p // 2) * LANES:(p // 2 + 1) * LANES]
        zero = jnp.zeros_like(qn)
        sa = 2 * (p % 2)
        q_a = jnp.concatenate([jnp.where(head_a, qn, zero), jnp.where(slot == sa, qr, zero)], axis=1)
        q_b = jnp.concatenate([jnp.where(head_a, zero, qn), jnp.where(slot == sa + 1, qr, zero)], axis=1)
        return jnp.concatenate([q_a, q_b], axis=0)

    chains = [(h, p) for h in range(n_qb) for p in range(N_PAIRS)]

    def logits(chain, keys, diagonal):
        cols = slice(chain[1] * LANES, (chain[1] + 1) * LANES)
        s = _nt_dot(stacked_q(*chain), jnp.concatenate([kn_ref[0, keys, cols], kr_ref[0, keys, :]], axis=1))
        if diagonal:
            qpos = lax.broadcasted_iota(jnp.int32, (2 * qb, qb), 0) % qb
            kpos = lax.broadcasted_iota(jnp.int32, (2 * qb, qb), 1)
            tail = jnp.where(qpos >= kpos, s[:, -qb:], NEG)
            s = tail if s.shape[1] == qb else jnp.concatenate([s[:, :-qb], tail], axis=1)
        return s

    def update(chain, keys, s, state):
        nk = s.shape[1]
        p = chain[1]
        m_cur = jnp.max(s, axis=1, keepdims=True)
        m_new = jnp.broadcast_to(m_cur, (2 * qb, LANES)) if state is None else jnp.maximum(state[0], m_cur)
        pr = jnp.exp2(s - jnp.concatenate([m_new] * (nk // LANES), axis=1))
        v_ones = jnp.concatenate([v_ref[0, keys, p * LANES:(p + 1) * LANES], jnp.ones((nk, LANES), BF16)],
                                 axis=1)
        pv = jnp.dot(pr.astype(BF16), v_ones, preferred_element_type=F32)
        acc, l_new = pv[:, :LANES], pv[:, LANES:]
        if state is None:
            return m_new, l_new, acc
        alpha = jnp.exp2(state[0] - m_new)
        return m_new, alpha * state[1] + l_new, alpha * state[2] + acc

    ends = [(h + 1) * qb for h in range(n_qb)]
    steps = []
    for lo in range(0, SEQ, kc):
        for h, end in enumerate(ends):
            if lo < end:
                steps += [((h, p), slice(lo, min(lo + kc, end)), end) for p in range(N_PAIRS)]
    states = {chain: None for chain in chains}
    pending = []
    for chain, keys, end in steps:
        pending.append((chain, keys, logits(chain, keys, keys.stop == end)))
        if len(pending) > LOOKAHEAD_MLA:
            c0, k0, s0 = pending.pop(0)
            states[c0] = update(c0, k0, s0, states[c0])
    for c0, k0, s0 in pending:
        states[c0] = update(c0, k0, s0, states[c0])
    for (h, p), (_, l, acc) in states.items():
        o = jnp.where(head_a, acc[:qb], acc[qb:]) / jnp.where(head_a, l[:qb], l[qb:])
        o_ref[0, h * qb:(h + 1) * qb, p * LANES:(p + 1) * LANES] = o.astype(BF16)


def _latent_attention(qn, qr, kn, kr, vb):
    bsz = qn.shape[0]
    blk = lambda w: pl.BlockSpec((1, SEQ, w), lambda b: (b, 0, 0))
    return pl.pallas_call(
        _mla_kernel,
        grid=(bsz,),
        in_specs=[blk(D_B), blk(B_HEADS * B_ROPE_DIM), blk(D_B), blk(LANES), blk(D_B)],
        out_specs=blk(D_B),
        out_shape=jax.ShapeDtypeStruct((bsz, SEQ, D_B), BF16),
        compiler_params=pltpu.CompilerParams(
            dimension_semantics=("parallel",), vmem_limit_bytes=VMEM_LIMIT),
        name="latent_attn",
    )(qn, qr, kn, kr, vb)


def _post_kernel(x_ref, oa_ref, ob_ref, g1_ref, sh2_ref, sc2_ref, g2_ref, goa_ref, gob_ref,
                 wout_ref, gn2_ref, wfi_ref, wfo_ref, gfin_ref, o_ref, a_scr):
    chains = [slice(t * SUB_POST, (t + 1) * SUB_POST) for t in range(TM_POST // SUB_POST)]

    def mixed(rows):
        ya = _rms(oa_ref[0, rows, :].astype(F32), goa_ref[...])
        yb = _rms(ob_ref[0, rows, :].astype(F32), gob_ref[...])
        y = jnp.concatenate([ya, yb], axis=1).astype(BF16)
        return jnp.dot(y, wout_ref[...], preferred_element_type=F32)

    def ffn(rows, x1):
        h2 = (_rms(x1, gn2_ref[...]) * (1.0 + sc2_ref[0]) + sh2_ref[0]).astype(BF16)
        for c in range(D_FF // FF_CHUNK):
            lo = c * FF_CHUNK
            g = jnp.dot(h2, wfi_ref[:, lo:lo + FF_CHUNK], preferred_element_type=F32)
            u = jnp.dot(h2, wfi_ref[:, D_FF + lo:D_FF + lo + FF_CHUNK], preferred_element_type=F32)
            a_scr[rows, lo:lo + FF_CHUNK] = (jax.nn.silu(g) * u).astype(BF16)
        return jnp.dot(a_scr[rows, :], wfo_ref[...], preferred_element_type=F32)

    mix = [mixed(rows) for rows in chains]
    x1 = [x_ref[0, rows, :] + g1_ref[0] * m for rows, m in zip(chains, mix)]
    f = [ffn(rows, x) for rows, x in zip(chains, x1)]
    for rows, x, y in zip(chains, x1, f):
        o_ref[0, rows, :] = _rms(x + g2_ref[0] * y, gfin_ref[...])


def _post(x, oa, ob, g1m, sh2, sc2, g2m, goa, gob, wout, gn2, wfi, wfo, gfin):
    bsz = x.shape[0]
    tm = TM_POST
    tok = lambda w: pl.BlockSpec((1, tm, w), lambda b, j: (b, j, 0))
    per_b = pl.BlockSpec((1, 1, D_MODEL), lambda b, j: (b, 0, 0))
    return pl.pallas_call(
        _post_kernel,
        grid=(bsz, SEQ // tm),
        in_specs=[tok(D_MODEL), tok(D_A), tok(D_B), per_b, per_b, per_b, per_b,
                  _const_spec((1, D_A)), _const_spec((1, D_B)), _const_spec(wout.shape),
                  _const_spec((1, D_MODEL)), _const_spec(wfi.shape), _const_spec(wfo.shape),
                  _const_spec((1, D_MODEL))],
        out_specs=tok(D_MODEL),
        out_shape=jax.ShapeDtypeStruct((bsz, SEQ, D_MODEL), F32),
        scratch_shapes=[pltpu.VMEM((tm, D_FF), BF16)],
        compiler_params=pltpu.CompilerParams(
            dimension_semantics=("parallel", "parallel"), vmem_limit_bytes=VMEM_LIMIT),
        name="out_proj_ffn",
    )(x, oa, ob, g1m, sh2, sc2, g2m, goa, gob, wout, gn2, wfi, wfo, gfin)


def _rotate_half_cols(w):
    lead = w.shape[:-1]
    half = B_ROPE_DIM // 2
    w = w.reshape(lead + (-1, 2, half))
    return jnp.stack([-w[..., 1, :], w[..., 0, :]], axis=-2).reshape(lead + (-1,))


def _t5_bucket(dist):
    max_exact = N_BUCKETS // 2
    d = np.maximum(dist, 1).astype(np.float64)
    large = max_exact + (np.log(d / max_exact) / np.log(MAX_DISTANCE / max_exact)
                         * (N_BUCKETS - max_exact)).astype(np.int64)
    large = np.minimum(large, N_BUCKETS - 1)
    return np.where(dist < max_exact, dist, large).astype(np.int32)


def _grouped_order(groups):
    rows = np.arange(BLK)
    per = BLK // groups
    return groups * (rows % per) + rows // per


def _bias_kernel(bucket_ref, rb_ref, o_ref):
    p = pl.program_id(0)
    for br in range(len(DILATIONS)):
        bucket = bucket_ref[br]
        tabs = [jnp.full((BLK, 2 * BLK), NEG, F32)] * 2
        for k in range(N_BUCKETS):
            hit = bucket == k
            tabs = [jnp.where(hit, rb_ref[k, 2 * p + hh] * LOG2E, tabs[hh]) for hh in range(2)]
        for hh in range(2):
            o_ref[0, br, hh * BLK:(hh + 1) * BLK, :] = tabs[hh]


def _bias_tables(rel_bias):
    a = np.arange(BLK)[:, None]
    bk = np.arange(2 * BLK)[None, :]
    steps = BLK + a - bk
    valid = (steps >= 0) & (steps <= BLK)
    bucket = np.stack([np.where(valid, _t5_bucket(np.clip(steps, 0, BLK) * dil), -1)
                       for dil in DILATIONS]).astype(np.int32)
    for br, groups in ((0, 16), (1, 4)):
        seq = _grouped_order(groups)
        bucket[br] = bucket[br][seq][:, np.concatenate([seq, BLK + seq])]
    nbr = len(DILATIONS)
    return pl.pallas_call(
        _bias_kernel,
        grid=(N_PAIRS,),
        in_specs=[pl.BlockSpec((nbr, BLK, 2 * BLK), lambda p: (0, 0, 0)),
                  pl.BlockSpec(memory_space=pltpu.SMEM)],
        out_specs=pl.BlockSpec((1, nbr, 2 * BLK, 2 * BLK), lambda p: (p, 0, 0, 0)),
        out_shape=jax.ShapeDtypeStruct((N_PAIRS, nbr, 2 * BLK, 2 * BLK), F32),
        name="bias_tables",
    )(jnp.asarray(bucket), rel_bias)


def _rope_tables():
    half = B_ROPE_DIM // 2
    inv = ROPE_THETA ** (-jnp.arange(half, dtype=F32) / half)
    ang = jnp.arange(SEQ, dtype=F32)[:, None] * inv[None, :]
    reps = LANES // half
    return jnp.tile(jnp.cos(ang), (1, reps)), jnp.tile(jnp.sin(ang), (1, reps))


def kernel(x, c, w_ada, b_ada, g_norm1, w_in, g_cq, w_uq, g_ckv, w_ukv, rel_bias, g_out_a, g_out_b,
           w_out, g_norm2, w_ffn_in, w_ffn_out, g_final):
    bsz = x.shape[0]
    assert x.shape == (bsz, SEQ, D_MODEL) and w_ada.shape[0] == 1
    l = 0

    mod = _modulation(c, w_ada[l], b_ada[l]).reshape(bsz, N_MOD, 1, D_MODEL)
    sh1, sc1, g1m, sh2, sc2, g2m = (mod[:, i] for i in range(N_MOD))

    wi = w_in[l]
    kr_cols = wi[:, 3 * D_A + Q_LORA + KV_LORA:]
    win = jnp.concatenate([wi[:, :D_A] * (A_HEAD_DIM ** -0.5 * LOG2E), wi[:, D_A:3 * D_A + Q_LORA + KV_LORA],
                           kr_cols, _rotate_half_cols(kr_cols),
                           jnp.zeros((D_MODEL, LANES - 2 * B_ROPE_DIM), F32)], axis=1).astype(BF16)
    wq = (w_uq[l] * ((B_NOPE_DIM + B_ROPE_DIM) ** -0.5 * LOG2E)).reshape(
        Q_LORA, B_HEADS, B_NOPE_DIM + B_ROPE_DIM)
    wq_rope = wq[:, :, B_NOPE_DIM:].reshape(Q_LORA, -1)
    wuq = jnp.concatenate([wq[:, :, :B_NOPE_DIM].reshape(Q_LORA, -1), wq_rope,
                           _rotate_half_cols(wq_rope)], axis=1).astype(BF16)
    wkv = w_ukv[l].reshape(KV_LORA, B_HEADS, B_NOPE_DIM + B_V_DIM)
    wukv = jnp.concatenate([wkv[:, :, :B_NOPE_DIM].reshape(KV_LORA, -1),
                            wkv[:, :, B_NOPE_DIM:].reshape(KV_LORA, -1)], axis=1).astype(BF16)
    cos_t, sin_t = _rope_tables()

    proj = _projections(
        x, sh1, sc1, g_norm1[l].reshape(1, -1), cos_t, sin_t, win, g_cq[l].reshape(1, -1), wuq,
        g_ckv[l].reshape(1, -1), wukv)
    qn, qr, kn, kr, vb = proj[3:8]

    out_a = _dilated_attention(proj[0:3], proj[8:11], _bias_tables(rel_bias))
    out_b = _latent_attention(qn, qr, kn, kr, vb)

    return _post(x, out_a, out_b, g1m, sh2, sc2, g2m, g_out_a[l].reshape(1, -1),
                 g_out_b[l].reshape(1, -1), w_out[l].astype(BF16), g_norm2[l].reshape(1, -1),
                 w_ffn_in[l].astype(BF16), w_ffn_out[l].astype(BF16), g_final.reshape(1, -1))
```
